```python
import math
import jax, jax.numpy as jnp
from jax import lax
import numpy as np

D_MODEL = 1024
BATCH = 2
SEQ = 8192
DEPTH = 1

MEM_LEN = 256
ATT_W = D_MODEL // 2
ATT_HD = 64
ATT_HEADS = ATT_W // ATT_HD
DILATED = ((128, 1), (512, 4), (2048, 16))
BLK = 128
ML_W = D_MODEL // 2
ML_HEADS = 4
ML_HD = ML_W // ML_HEADS
CHUNK = 128
CONV_K = 4
MIX_W = ATT_W + ML_W
SPLITS = [ATT_W, 2 * ATT_W, 3 * ATT_W, 3 * ATT_W + 2 * ML_W, 3 * ATT_W + 3 * ML_W, 3 * ATT_W + 4 * ML_W]
W_IN = 3 * ATT_W + 4 * ML_W + 2 * ML_HEADS
XA_HEADS = 4
XA_HD = D_MODEL // XA_HEADS
D_FF = ((8 * D_MODEL // 3 + 255) // 256) * 256
REL_BUCKETS = 32
REL_MAX_DIST = 2048
ALPHA = (2 * DEPTH) ** 0.25
BETA = (8 * DEPTH) ** -0.25
LN_EPS = 1e-5
NEG = -1e30

kernel_name = 'hymba_dilated_mlstm_macaron_deepnorm'


def layer_norm(x, g, b):
    xf = x.astype(jnp.float32)
    mu = jnp.mean(xf, -1, keepdims=True)
    var = jnp.mean(jnp.square(xf - mu), -1, keepdims=True)
    y = (xf - mu) * lax.rsqrt(var + LN_EPS)
    return (y * g.astype(jnp.float32) + b.astype(jnp.float32)).astype(x.dtype)


def swiglu(x, w_gate, w_up, w_down):
    return (jax.nn.silu(x @ w_gate) * (x @ w_up)) @ w_down


def t5_bucket(dist):
    exact = REL_BUCKETS // 2
    df = jnp.maximum(dist, 1).astype(jnp.float32)
    large = exact + (jnp.log(df / exact) / math.log(REL_MAX_DIST / exact) * (REL_BUCKETS - exact)).astype(jnp.int32)
    large = jnp.minimum(large, REL_BUCKETS - 1)
    return jnp.where(dist < exact, dist, large)


def band_offsets():
    qi = jnp.arange(BLK)[:, None]
    ki = jnp.arange(2 * BLK)[None, :]
    return qi + BLK - ki, ki


def branch_bias(rel_table, dilation, n_keys):
    off, _ = band_offsets()
    bucket = t5_bucket(dilation * jnp.clip(off, 0, n_keys))
    return jnp.transpose(rel_table.astype(jnp.float32)[bucket], (2, 0, 1))


def banded_attention(q, k, v, bias, n_keys):
    B, H, R, M, E = q.shape
    nb = M // BLK
    blocks = lambda t: t.reshape(B, H, R, nb, BLK, E)
    shift = lambda t: jnp.concatenate([jnp.zeros_like(t[:, :, :, :1]), t[:, :, :, :-1]], axis=3)
    qb, kb, vb = blocks(q), blocks(k), blocks(v)
    kk = jnp.concatenate([shift(kb), kb], axis=4)
    vv = jnp.concatenate([shift(vb), vb], axis=4)
    logits = jnp.einsum('bhrnqe,bhrnke->bhrnqk', qb, kk).astype(jnp.float32) * (ATT_HD ** -0.5)
    logits = logits + bias[:, None, None]
    off, ki = band_offsets()
    band = (off >= 0) & (off <= n_keys)
    valid = band[None] & ((jnp.arange(nb)[:, None, None] > 0) | (ki[None] >= BLK))
    logits = jnp.where(valid, logits, NEG)
    mx = jnp.max(logits, -1, keepdims=True)
    p = jnp.exp(logits - mx)
    s = jnp.sum(p, -1)
    o = jnp.einsum('bhrnqk,bhrnke->bhrnqe', p, vv) / s[..., None]
    lse = mx[..., 0] + jnp.log(s)
    return o.reshape(B, H, R, M, E), lse.reshape(B, H, R, M)


def dilated_branch(q, k, v, rel_table, window, dilation):
    B, H, S, E = q.shape
    span = dilation * BLK
    s_pad = -(-S // span) * span
    m_len = s_pad // dilation
    n_keys = window // dilation

    def fold(t):
        t = jnp.pad(t, ((0, 0), (0, 0), (0, s_pad - S), (0, 0)))
        return t.reshape(B, H, m_len, dilation, E).transpose(0, 1, 3, 2, 4)

    o, lse = banded_attention(fold(q), fold(k), fold(v), branch_bias(rel_table, dilation, n_keys), n_keys)
    o = o.transpose(0, 1, 3, 2, 4).reshape(B, H, s_pad, E)[:, :, :S]
    lse = lse.transpose(0, 1, 3, 2).reshape(B, H, s_pad)[:, :, :S]
    return o, lse


def dilated_attention(q, k, v, rel_table):
    outs, lses = [], []
    for window, dilation in DILATED:
        o, l = dilated_branch(q, k, v, rel_table, window, dilation)
        outs.append(o)
        lses.append(l)
    wts = jax.nn.softmax(jnp.stack(lses, 0), axis=0)
    return jnp.sum(wts[..., None] * jnp.stack(outs, 0), 0)


def causal_conv(x, w, b):
    S = x.shape[1]
    xp = jnp.pad(x, ((0, 0), (CONV_K - 1, 0), (0, 0)))
    y = b
    for j in range(CONV_K):
        y = y + xp[:, j:j + S] * w[j]
    return y


def mlstm(q, k, v, ig, fg):
    B, H, S, E = q.shape
    nc = S // CHUNK
    ch = lambda t: t.reshape(B, H, nc, CHUNK, E)
    q, k, v = ch(q), ch(k) * (E ** -0.5), ch(v)
    ig = ig.astype(jnp.float32).reshape(B, H, nc, CHUNK)
    logf = jax.nn.log_sigmoid(fg.astype(jnp.float32)).reshape(B, H, nc, CHUNK)
    b = jnp.cumsum(logf, -1)
    g = b[..., -1]
    a = g[..., None] - b + ig
    m_loc = jnp.max(a, -1)
    wa = jnp.exp(a - m_loc[..., None])
    c_loc = jnp.einsum('bhcl,bhcld,bhcle->bhcde', wa, v, k)
    n_loc = jnp.einsum('bhcl,bhcle->bhce', wa, k)

    def step(carry, inp):
        c, n, m = carry
        g_c, cl, nl, ml = inp
        m_new = jnp.maximum(g_c + m, ml)
        sp = jnp.exp(g_c + m - m_new)
        sl = jnp.exp(ml - m_new)
        c_new = sp[..., None, None] * c + sl[..., None, None] * cl
        n_new = sp[..., None] * n + sl[..., None] * nl
        return (c_new, n_new, m_new), (c, n, m)

    init = (jnp.zeros((B, H, E, E), jnp.float32), jnp.zeros((B, H, E), jnp.float32), jnp.zeros((B, H), jnp.float32))
    mv = lambda t: jnp.moveaxis(t, 2, 0)
    _, (c_prev, n_prev, m_prev) = lax.scan(step, init, (mv(g), mv(c_loc), mv(n_loc), mv(m_loc)))
    c_prev = jnp.moveaxis(c_prev, 0, 2)
    n_prev = jnp.moveaxis(n_prev, 0, 2)
    m_prev = jnp.moveaxis(m_prev, 0, 2)

    causal = jnp.tril(jnp.ones((CHUNK, CHUNK), bool))
    d_log = jnp.where(causal, b[..., :, None] - b[..., None, :] + ig[..., None, :], -jnp.inf)
    e_log = b + m_prev[..., None]
    m_t = jnp.maximum(e_log, jnp.max(d_log, -1))
    d_w = jnp.exp(d_log - m_t[..., None])
    e_w = jnp.exp(e_log - m_t)
    s_qk = jnp.einsum('bhcte,bhcse->bhcts', q, k) * d_w
    num = e_w[..., None] * jnp.einsum('bhcde,bhcte->bhctd', c_prev, q) + jnp.einsum('bhcts,bhcsd->bhctd', s_qk, v)
    den = e_w * jnp.einsum('bhce,bhcte->bhct', n_prev, q) + jnp.sum(s_qk, -1)
    h = num / jnp.maximum(jnp.abs(den), jnp.exp(-m_t))[..., None]
    return h.reshape(B, H, S, E)


def head_norm(h, g, n_heads):
    B, S, W = h.shape
    hf = h.astype(jnp.float32).reshape(B, S, n_heads, W // n_heads)
    mu = jnp.mean(hf, -1, keepdims=True)
    var = jnp.mean(jnp.square(hf - mu), -1, keepdims=True)
    y = ((hf - mu) * lax.rsqrt(var + LN_EPS)).reshape(B, S, W)
    return y * g.astype(jnp.float32)


def hybrid_mixer(x, w_in, conv_w, conv_b, ig_b, fg_b, ml_g, w_out, rel_table):
    B, S, _ = x.shape
    proj = x @ w_in
    qa, ka, va, qk_m, vm, om, gates = jnp.split(proj, SPLITS, axis=-1)
    heads = lambda t, h: t.reshape(B, S, h, -1).transpose(0, 2, 1, 3)
    att = dilated_attention(heads(qa, ATT_HEADS), heads(ka, ATT_HEADS), heads(va, ATT_HEADS), rel_table)
    att = att.transpose(0, 2, 1, 3).reshape(B, S, ATT_W).astype(x.dtype)
    qk_m = jax.nn.silu(causal_conv(qk_m, conv_w, conv_b))
    qm, km = jnp.split(qk_m, 2, axis=-1)
    ig = (gates[..., :ML_HEADS] + ig_b).transpose(0, 2, 1)
    fg = (gates[..., ML_HEADS:] + fg_b).transpose(0, 2, 1)
    h = mlstm(heads(qm, ML_HEADS), heads(km, ML_HEADS), heads(vm, ML_HEADS), ig, fg)
    h = h.transpose(0, 2, 1, 3).reshape(B, S, ML_W)
    h = jax.nn.sigmoid(om.astype(jnp.float32)) * h
    h = head_norm(h, ml_g, ML_HEADS).astype(x.dtype)
    return jnp.concatenate([att, h], axis=-1) @ w_out


def memory_attention(x, mem, wq, wkv, wo):
    B, S, _ = x.shape
    L = mem.shape[1]
    q = (x @ wq).reshape(B, S, XA_HEADS, XA_HD)
    kv = (mem @ wkv).reshape(B, L, 2, XA_HEADS, XA_HD)
    k, v = kv[:, :, 0], kv[:, :, 1]
    logits = jnp.einsum('bshe,bmhe->bhsm', q, k).astype(jnp.float32) * (XA_HD ** -0.5)
    p = jax.nn.softmax(logits, axis=-1)
    o = jnp.einsum('bhsm,bmhe->bshe', p, v).reshape(B, S, D_MODEL).astype(x.dtype)
    return o @ wo


def setup_inputs(seed: int = 0) -> dict:
    key = jax.random.key(seed)
    ks = jax.random.split(key, 20)
    f32 = jnp.float32
    nrm = lambda k, shape, scale: jax.random.normal(k, shape, f32) * scale
    return {
        'x': nrm(ks[0], (BATCH, SEQ, D_MODEL), 1.0),
        'mem': nrm(ks[1], (BATCH, MEM_LEN, D_MODEL), 1.0),
        'rel_bias': nrm(ks[2], (REL_BUCKETS, ATT_HEADS), 0.1),
        'ln_g': 1.0 + nrm(ks[3], (DEPTH, 4, D_MODEL), 0.02),
        'ln_b': nrm(ks[4], (DEPTH, 4, D_MODEL), 0.02),
        'ffn_w_gate': nrm(ks[5], (DEPTH, 2, D_MODEL, D_FF), D_MODEL ** -0.5),
        'ffn_w_up': nrm(ks[6], (DEPTH, 2, D_MODEL, D_FF), D_MODEL ** -0.5),
        'ffn_w_down': nrm(ks[7], (DEPTH, 2, D_FF, D_MODEL), BETA * D_FF ** -0.5),
        'w_in': nrm(ks[8], (DEPTH, D_MODEL, W_IN), D_MODEL ** -0.5),
        'conv_w': nrm(ks[9], (DEPTH, CONV_K, 2 * ML_W), CONV_K ** -0.5),
        'conv_b': nrm(ks[10], (DEPTH, 2 * ML_W), 0.02),
        'ig_bias': nrm(ks[11], (DEPTH, ML_HEADS), 0.1),
        'fg_bias': jnp.linspace(3.0, 6.0, ML_HEADS, dtype=f32)[None] + nrm(ks[12], (DEPTH, ML_HEADS), 0.1),
        'ml_norm_g': 1.0 + nrm(ks[13], (DEPTH, ML_W), 0.02),
        'w_out': nrm(ks[14], (DEPTH, MIX_W, D_MODEL), BETA * MIX_W ** -0.5),
        'xq_w': nrm(ks[15], (DEPTH, D_MODEL, D_MODEL), D_MODEL ** -0.5),
        'xkv_w': nrm(ks[16], (DEPTH, D_MODEL, 2 * D_MODEL), D_MODEL ** -0.5),
        'xo_w': nrm(ks[17], (DEPTH, D_MODEL, D_MODEL), BETA * D_MODEL ** -0.5),
    }


def reference(x, mem, rel_bias, ln_g, ln_b, ffn_w_gate, ffn_w_up, ffn_w_down, w_in, conv_w, conv_b,
              ig_bias, fg_bias, ml_norm_g, w_out, xq_w, xkv_w, xo_w):
    for l in range(DEPTH):
        x = layer_norm(ALPHA * x + 0.5 * swiglu(x, ffn_w_gate[l, 0], ffn_w_up[l, 0], ffn_w_down[l, 0]), ln_g[l, 0], ln_b[l, 0])
        x = layer_norm(ALPHA * x + hybrid_mixer(x, w_in[l], conv_w[l], conv_b[l], ig_bias[l], fg_bias[l],
                                                ml_norm_g[l], w_out[l], rel_bias), ln_g[l, 1], ln_b[l, 1])
        x = layer_norm(ALPHA * x + memory_attention(x, mem, xq_w[l], xkv_w[l], xo_w[l]), ln_g[l, 2], ln_b[l, 2])
        x = layer_norm(ALPHA * x + 0.5 * swiglu(x, ffn_w_gate[l, 1], ffn_w_up[l, 1], ffn_w_down[l, 1]), ln_g[l, 3], ln_b[l, 3])
    return x
```

```python
import functools
import math

import jax
import jax.numpy as jnp
from jax import lax
from jax.experimental import pallas as pl
from jax.experimental.pallas import tpu as pltpu

F32 = jnp.float32
BF16 = jnp.bfloat16

ATT_HD = 64
ATT_HEADS = 8
DILATED = ((128, 1), (512, 4), (2048, 16))
BLK = 128
ML_HEADS = 4
CHUNK = 128
CONV_K = 4
XA_HEADS = 4
REL_BUCKETS = 32
REL_MAX_DIST = 2048
DEPTH = 1
ALPHA = (2 * DEPTH) ** 0.25
LN_EPS = 1e-5
NEG = -1e30

LANES = 128
VMEM_LIMIT = 56 * 1024 * 1024
ROW_TILE = 256
ATT_ORDER = (2, 1, 0)


def _const_spec(shape):
    nd = len(shape)
    return pl.BlockSpec(shape, lambda *_: (0,) * nd, pipeline_mode=pl.Buffered(1))


def _layer_norm(y, g, b):
    mu = jnp.mean(y, -1, keepdims=True)
    yc = y - mu
    var = jnp.mean(yc * yc, -1, keepdims=True)
    return yc * lax.rsqrt(var + LN_EPS) * g + b


def _silu(x):
    return x * jax.nn.sigmoid(x)


def _swiglu(xb, wg_ref, wu_ref, wd_ref):
    g = jnp.dot(xb, wg_ref[...], preferred_element_type=F32)
    u = jnp.dot(xb, wu_ref[...], preferred_element_type=F32)
    h = (_silu(g) * u).astype(BF16)
    return jnp.dot(h, wd_ref[...], preferred_element_type=F32)


def _bias_kernel(rel_ref, bkt_ref, out_ref):
    bkt = bkt_ref[0]
    for h in range(ATT_HEADS):
        acc = jnp.full(bkt.shape, NEG, F32)
        for b in range(REL_BUCKETS):
            acc = jnp.where(bkt == b, rel_ref[b, h], acc)
        out_ref[0, h] = acc


def _bias_tables(rel_bias):
    qi = jnp.arange(BLK)[:, None]
    ki = jnp.arange(2 * BLK)[None, :]
    off = qi + BLK - ki
    exact = REL_BUCKETS // 2
    tabs = []
    for window, dil in DILATED:
        n_keys = window // dil
        dist = dil * jnp.clip(off, 0, n_keys)
        df = jnp.maximum(dist, 1).astype(F32)
        large = exact + (jnp.log(df / exact) / math.log(REL_MAX_DIST / exact)
                         * (REL_BUCKETS - exact)).astype(jnp.int32)
        large = jnp.minimum(large, REL_BUCKETS - 1)
        bucket = jnp.where(dist < exact, dist, large)
        band = (off >= 0) & (off <= n_keys)
        tabs.append(jnp.where(band, bucket, -1))
    bkt = jnp.stack(tabs, 0).astype(jnp.int32)
    nd = len(DILATED)
    return pl.pallas_call(
        _bias_kernel,
        grid=(nd,),
        in_specs=[pl.BlockSpec(memory_space=pltpu.SMEM),
                  pl.BlockSpec((1, BLK, 2 * BLK), lambda d: (d, 0, 0))],
        out_specs=pl.BlockSpec((1, ATT_HEADS, BLK, 2 * BLK), lambda d: (d, 0, 0, 0)),
        out_shape=jax.ShapeDtypeStruct((nd, ATT_HEADS, BLK, 2 * BLK), F32),
        name="bias",
    )(rel_bias.astype(F32), bkt)


def _memkv_kernel(mem_ref, wkt_ref, wv_ref, kt_ref, v_ref):
    mb = mem_ref[...].astype(BF16)
    kt = lax.dot_general(wkt_ref[...], mb, (((1,), (1,)), ((), ())),
                         preferred_element_type=F32)
    kt_ref[...] = kt.astype(BF16)
    v_ref[...] = jnp.dot(mb, wv_ref[...], preferred_element_type=F32).astype(BF16)


def _memkv(mem, wkt, wv):
    B, L, D = mem.shape
    return pl.pallas_call(
        _memkv_kernel,
        grid=(B,),
        in_specs=[pl.BlockSpec((None, L, D), lambda b: (b, 0, 0)),
                  _const_spec((D, D)), _const_spec((D, D))],
        out_specs=[pl.BlockSpec((None, D, L), lambda b: (b, 0, 0)),
                   pl.BlockSpec((None, L, D), lambda b: (b, 0, 0))],
        out_shape=[jax.ShapeDtypeStruct((B, D, L), BF16),
                   jax.ShapeDtypeStruct((B, L, D), BF16)],
        compiler_params=pltpu.CompilerParams(vmem_limit_bytes=VMEM_LIMIT),
        name="memkv",
    )(mem, wkt, wv)


def _ffn_in_kernel(x_ref, wg_ref, wu_ref, wd_ref, g_ref, b_ref,
                   wqkv_ref, wqkm_ref, wvm_ref, wom_ref, wgt_ref, wgtt_ref,
                   x1_ref, qkv_ref, qkm_ref, vm_ref, om_ref, gates_ref, gatest_ref):
    x = x_ref[...]
    y = ALPHA * x + 0.5 * _swiglu(x.astype(BF16), wg_ref, wu_ref, wd_ref)
    x1 = _layer_norm(y, g_ref[...], b_ref[...])
    x1_ref[...] = x1
    xb = x1.astype(BF16)
    qkv_ref[...] = jnp.dot(xb, wqkv_ref[...], preferred_element_type=F32)
    qkm_ref[...] = jnp.dot(xb, wqkm_ref[...], preferred_element_type=F32)
    vm_ref[...] = jnp.dot(xb, wvm_ref[...], preferred_element_type=F32).astype(BF16)
    om_ref[...] = jnp.dot(xb, wom_ref[...], preferred_element_type=F32)
    gates_ref[...] = jnp.dot(xb, wgt_ref[...], preferred_element_type=F32)
    gatest_ref[...] = lax.dot_general(wgtt_ref[...], xb, (((1,), (1,)), ((), ())),
                                      preferred_element_type=F32)


def _ffn_in(x2d, wg, wu, wd, g, b, wqkv, wqkm, wvm, wom, wgt, wgtt):
    T, D = x2d.shape
    tm = ROW_TILE
    row = lambda w: pl.BlockSpec((tm, w), lambda i: (i, 0))
    n_att, n_qkm, n_ml, n_g = wqkv.shape[1], wqkm.shape[1], wvm.shape[1], wgt.shape[1]
    weights = (wg, wu, wd, g, b, wqkv, wqkm, wvm, wom, wgt, wgtt)
    return pl.pallas_call(
        _ffn_in_kernel,
        grid=(T // tm,),
        in_specs=[row(D)] + [_const_spec(w.shape) for w in weights],
        out_specs=[row(D), row(n_att), row(n_qkm), row(n_ml), row(n_ml), row(n_g),
                   pl.BlockSpec((n_g, tm), lambda i: (0, i))],
        out_shape=[jax.ShapeDtypeStruct((T, D), F32),
                   jax.ShapeDtypeStruct((T, n_att), F32),
                   jax.ShapeDtypeStruct((T, n_qkm), F32),
                   jax.ShapeDtypeStruct((T, n_ml), BF16),
                   jax.ShapeDtypeStruct((T, n_ml), F32),
                   jax.ShapeDtypeStruct((T, n_g), F32),
                   jax.ShapeDtypeStruct((n_g, T), F32)],
        compiler_params=pltpu.CompilerParams(
            dimension_semantics=("parallel",), vmem_limit_bytes=VMEM_LIMIT),
        name="ffn_in",
    )(x2d, *weights)


def _attn_kernel(q_ref, k_ref, v_ref, bias_ref, out_ref, acc_s, m_s, l_s):
    S = q_ref.shape[0]
    lane = lax.broadcasted_iota(jnp.int32, (BLK, LANES), 1)
    head0 = lane < ATT_HD
    prev_half = lax.broadcasted_iota(jnp.int32, (1, 2 * BLK), 1) < BLK
    zeros_kv = jnp.zeros((BLK, LANES), BF16)

    for step, di in enumerate(ATT_ORDER):
        dil = DILATED[di][1]
        nb = S // (BLK * dil)

        def block(n, carry, r, dil=dil, di=di, step=step):
            kp, vp = carry
            start = r + n * (BLK * dil)
            if dil == 1:
                rows = pl.ds(pl.multiple_of(start, BLK), BLK)
            else:
                rows = pl.ds(start, BLK, stride=dil)
            q = q_ref[rows, :]
            kc = k_ref[rows, :].astype(BF16)
            vc = v_ref[rows, :].astype(BF16)
            kk = jnp.concatenate([kp, kc], axis=0)
            vv = jnp.concatenate([vp, vc], axis=0)
            first = jnp.where(jnp.logical_and(prev_half, n == 0), NEG, 0.0)
            ms, ls, os_ = [], [], []
            for h in range(2):
                sel = head0 if h == 0 else jnp.logical_not(head0)
                qh = jnp.where(sel, q, 0.0).astype(BF16)
                lg = lax.dot_general(qh, kk, (((1,), (1,)), ((), ())),
                                     preferred_element_type=F32)
                lg = lg + bias_ref[di, h] + first
                mh = jnp.max(lg, -1, keepdims=True)
                p = jnp.exp(lg - mh)
                ls.append(jnp.sum(p, -1, keepdims=True))
                ms.append(mh)
                os_.append(jnp.dot(p.astype(BF16), vv, preferred_element_type=F32))
            m_b = jnp.where(head0, ms[0], ms[1])
            l_b = jnp.where(head0, ls[0], ls[1])
            o_b = jnp.where(head0, os_[0], os_[1])
            if step > 0:
                m_o = m_s[rows, :]
                m_n = jnp.maximum(m_o, m_b)
                e_o = jnp.exp(m_o - m_n)
                e_b = jnp.exp(m_b - m_n)
                l_b = l_s[rows, :] * e_o + l_b * e_b
                o_b = acc_s[rows, :] * e_o + o_b * e_b
                m_b = m_n
            if step < len(ATT_ORDER) - 1:
                m_s[rows, :] = m_b
                l_s[rows, :] = l_b
                acc_s[rows, :] = o_b
            else:
                out_ref[rows, :] = (o_b / l_b).astype(out_ref.dtype)
            return kc, vc

        def residue(r, _, nb=nb, block=block):
            lax.fori_loop(0, nb, functools.partial(block, r=r), (zeros_kv, zeros_kv))
            return 0

        lax.fori_loop(0, dil, residue, 0)


def _attention(qkv, bias, B, S):
    n_pairs = ATT_HEADS * ATT_HD // LANES
    qkv3 = qkv.reshape(B, S, qkv.shape[-1])
    col = lambda off: pl.BlockSpec((None, S, LANES), lambda b, p: (b, 0, off + p))
    assert DILATED[ATT_ORDER[-1]][1] == 1
    return pl.pallas_call(
        _attn_kernel,
        grid=(B, n_pairs),
        in_specs=[col(0), col(n_pairs), col(2 * n_pairs),
                  pl.BlockSpec((len(DILATED), 2, BLK, 2 * BLK), lambda b, p: (0, p, 0, 0))],
        out_specs=pl.BlockSpec((None, S, LANES), lambda b, p: (b, 0, p)),
        out_shape=jax.ShapeDtypeStruct((B, S, n_pairs * LANES), BF16),
        scratch_shapes=[pltpu.VMEM((S, LANES), F32)] * 3,
        compiler_params=pltpu.CompilerParams(
            dimension_semantics=("parallel", "parallel"), vmem_limit_bytes=VMEM_LIMIT),
        name="attn",
    )(qkv3, qkv3, qkv3, bias)


def _split3(a):
    hi = a.astype(BF16)
    r1 = a - hi.astype(F32)
    mid = r1.astype(BF16)
    lo = (r1 - mid.astype(F32)).astype(BF16)
    return hi, mid, lo


def _log_sigmoid(x):
    return jnp.minimum(x, 0.0) - jnp.log1p(jnp.exp(-jnp.abs(x)))


def _mlstm_kernel(qk_ref, v_ref, o_ref, g_ref, gt_ref, cw_ref, cb_ref, gbr_ref, gbc_ref,
                  mlg_ref, tri_ref, out_ref, xbuf, ct_s, n_s, m_s):
    c = pl.program_id(1)
    E = LANES
    n_qk = qk_ref.shape[1] // 2

    @pl.when(c == 0)
    def _():
        xbuf[0:8, :] = jnp.zeros((8, xbuf.shape[1]), F32)
        ct_s[...] = jnp.zeros(ct_s.shape, F32)
        n_s[...] = jnp.zeros(n_s.shape, F32)
        m_s[...] = jnp.zeros(m_s.shape, F32)

    @pl.when(c > 0)
    def _():
        xbuf[0:8, :] = xbuf[CHUNK:CHUNK + 8, :]

    xbuf[8:8 + CHUNK, :] = qk_ref[...]
    y = cb_ref[...]
    for j in range(CONV_K):
        off = 8 - (CONV_K - 1) + j
        y = y + xbuf[off:off + CHUNK, :] * cw_ref[j:j + 1, :]
    qk = _silu(y)

    gc = g_ref[...] + gbr_ref[...]
    gr = gt_ref[...] + gbc_ref[...]
    tri = tri_ref[...]
    bc_all = sum(jnp.dot(tri, part, preferred_element_type=F32)
                 for part in _split3(_log_sigmoid(gc)))
    br_all = sum(lax.dot_general(part, tri, (((1,), (1,)), ((), ())),
                                 preferred_element_type=F32)
                 for part in _split3(_log_sigmoid(gr)))
    ti = lax.broadcasted_iota(jnp.int32, (CHUNK, CHUNK), 0)
    si = lax.broadcasted_iota(jnp.int32, (CHUNK, CHUNK), 1)
    causal = si <= ti

    for h in range(ML_HEADS):
        ig_c = gc[:, h:h + 1]
        b_c = bc_all[:, ML_HEADS + h:ML_HEADS + h + 1]
        ig_r = gr[h:h + 1, :]
        b_r = br_all[ML_HEADS + h:ML_HEADS + h + 1, :]
        g_tot = b_r[:, CHUNK - 1:CHUNK]
        q = qk[:, h * E:(h + 1) * E]
        k = qk[:, n_qk + h * E:n_qk + (h + 1) * E] * (E ** -0.5)
        v = v_ref[:, h * E:(h + 1) * E]
        ct_prev = ct_s[h]
        n_prev = n_s[h]
        m_prev = m_s[h]
        m_prev1 = m_prev[:, 0:1]

        m_loc = jnp.max(g_tot - b_r + ig_r, -1, keepdims=True)
        wa_c = jnp.exp(g_tot - b_c + ig_c - m_loc)
        kw = wa_c * k
        cl_t = jnp.dot(kw.T.astype(BF16), v, preferred_element_type=F32)
        n_loc = jnp.sum(kw, 0, keepdims=True)

        m_new = jnp.maximum(g_tot + m_prev, m_loc)
        sp = jnp.exp(g_tot + m_prev - m_new)
        sl = jnp.exp(m_loc - m_new)
        ct_s[h] = sp[:, 0:1] * ct_prev + sl[:, 0:1] * cl_t
        n_s[h] = sp * n_prev + sl * n_loc
        m_s[h] = m_new

        d_log = jnp.where(causal, b_c - b_r + ig_r, -jnp.inf)
        e_log = b_c + m_prev1
        m_t = jnp.maximum(e_log, jnp.max(d_log, -1, keepdims=True))
        d_w = jnp.exp(d_log - m_t)
        e_w = jnp.exp(e_log - m_t)
        qb = q.astype(BF16)
        s_qk = lax.dot_general(qb, k.astype(BF16), (((1,), (1,)), ((), ())),
                               preferred_element_type=F32) * d_w
        num = (e_w * jnp.dot(qb, ct_prev.astype(BF16), preferred_element_type=F32)
               + jnp.dot(s_qk.astype(BF16), v, preferred_element_type=F32))
        den = (e_w * jnp.sum(q * n_prev, -1, keepdims=True)
               + jnp.sum(s_qk, -1, keepdims=True))
        hh = num / jnp.maximum(jnp.abs(den), jnp.exp(-m_t))

        hg = jax.nn.sigmoid(o_ref[:, h * E:(h + 1) * E]) * hh
        mu = jnp.mean(hg, -1, keepdims=True)
        hc = hg - mu
        var = jnp.mean(hc * hc, -1, keepdims=True)
        yn = hc * lax.rsqrt(var + LN_EPS) * mlg_ref[:, h * E:(h + 1) * E]
        out_ref[:, h * E:(h + 1) * E] = yn.astype(out_ref.dtype)


def _mlstm(qkm, vm, om, gates, gatest, conv_w, conv_b, gb_row, gb_col, ml_g, B, S):
    nc = S // CHUNK
    W2 = qkm.shape[1]
    W = vm.shape[1]
    G = gates.shape[1]
    tri = jnp.tril(jnp.ones((CHUNK, CHUNK), F32)).astype(BF16)
    row = lambda w: pl.BlockSpec((CHUNK, w), lambda b, c: (b * nc + c, 0))
    consts = (conv_w, conv_b, gb_row, gb_col, ml_g, tri)
    return pl.pallas_call(
        _mlstm_kernel,
        grid=(B, nc),
        in_specs=[row(W2), row(W), row(W), row(G),
                  pl.BlockSpec((G, CHUNK), lambda b, c: (0, b * nc + c))]
                 + [_const_spec(a.shape) for a in consts],
        out_specs=row(W),
        out_shape=jax.ShapeDtypeStruct((B * S, W), BF16),
        scratch_shapes=[pltpu.VMEM((CHUNK + 8, W2), F32),
                        pltpu.VMEM((ML_HEADS, LANES, LANES), F32),
                        pltpu.VMEM((ML_HEADS, 1, LANES), F32),
                        pltpu.VMEM((ML_HEADS, 1, LANES), F32)],
        compiler_params=pltpu.CompilerParams(
            dimension_semantics=("parallel", "arbitrary"), vmem_limit_bytes=VMEM_LIMIT),
        name="mlstm",
    )(qkm, vm, om, gates, gatest, *consts)


def _tail_kernel(x1_ref, att_ref, hm_ref, kt_ref, v_ref, woa_ref, wom_ref, wq_ref, wo_ref,
                 wg_ref, wu_ref, wd_ref, g_ref, b_ref, out_ref):
    mix = (jnp.dot(att_ref[...], woa_ref[...], preferred_element_type=F32)
           + jnp.dot(hm_ref[...], wom_ref[...], preferred_element_type=F32))
    x2 = _layer_norm(ALPHA * x1_ref[...] + mix, g_ref[0:1, :], b_ref[0:1, :])

    q = jnp.dot(x2.astype(BF16), wq_ref[...], preferred_element_type=F32)
    hd = q.shape[1] // XA_HEADS
    heads = []
    for h in range(XA_HEADS):
        qh = q[:, h * hd:(h + 1) * hd].astype(BF16)
        lg = jnp.dot(qh, kt_ref[h * hd:(h + 1) * hd, :], preferred_element_type=F32)
        p = jnp.exp(lg - jnp.max(lg, -1, keepdims=True))
        s = jnp.sum(p, -1, keepdims=True)
        o = jnp.dot(p.astype(BF16), v_ref[:, h * hd:(h + 1) * hd],
                    preferred_element_type=F32)
        heads.append((o / s).astype(BF16))
    xa = jnp.dot(jnp.concatenate(heads, axis=1), wo_ref[...], preferred_element_type=F32)
    x3 = _layer_norm(ALPHA * x2 + xa, g_ref[1:2, :], b_ref[1:2, :])

    y = ALPHA * x3 + 0.5 * _swiglu(x3.astype(BF16), wg_ref, wu_ref, wd_ref)
    out_ref[...] = _layer_norm(y, g_ref[2:3, :], b_ref[2:3, :])


def _tail(x1, att, hm, kt, v, woa, wom, wq, wo, wg, wu, wd, g, b, S):
    T, D = x1.shape
    tm = ROW_TILE
    per_batch = S // tm
    row = lambda w: pl.BlockSpec((tm, w), lambda i: (i, 0))
    L = v.shape[1]
    weights = (woa, wom, wq, wo, wg, wu, wd, g, b)
    return pl.pallas_call(
        _tail_kernel,
        grid=(T // tm,),
        in_specs=[row(D), row(att.shape[1]), row(hm.shape[1]),
                  pl.BlockSpec((None, D, L), lambda i: (i // per_batch, 0, 0)),
                  pl.BlockSpec((None, L, D), lambda i: (i // per_batch, 0, 0))]
                 + [_const_spec(w.shape) for w in weights],
        out_specs=row(D),
        out_shape=jax.ShapeDtypeStruct((T, D), F32),
        compiler_params=pltpu.CompilerParams(
            dimension_semantics=("parallel",), vmem_limit_bytes=VMEM_LIMIT),
        name="tail",
    )(x1, att, hm, kt, v, *weights)


def kernel(x, mem, rel_bias, ln_g, ln_b, ffn_w_gate, ffn_w_up, ffn_w_down, w_in, conv_w, conv_b,
           ig_bias, fg_bias, ml_norm_g, w_out, xq_w, xkv_w, xo_w):
    B, S, D = x.shape
    att_w = ATT_HEADS * ATT_HD
    ml_w = ML_HEADS * LANES
    bias = _bias_tables(rel_bias)
    xf = x.reshape(B * S, D)
    for l in range(DEPTH):
        bf = lambda a: a.astype(BF16)
        wi = w_in[l]
        c0 = 3 * att_w
        scale_q = jnp.concatenate([jnp.full((att_w,), ATT_HD ** -0.5, F32),
                                   jnp.ones((2 * att_w,), F32)])
        wqkv = bf(wi[:, :c0] * scale_q)
        wqkm = bf(wi[:, c0:c0 + 2 * ml_w])
        wvm = bf(wi[:, c0 + 2 * ml_w:c0 + 3 * ml_w])
        wom = bf(wi[:, c0 + 3 * ml_w:c0 + 4 * ml_w])
        wgt = bf(wi[:, c0 + 4 * ml_w:])
        gb = jnp.concatenate([ig_bias[l], fg_bias[l]]).astype(F32)

        x1, qkv, qkm, vm, om, gates, gatest = _ffn_in(
            xf, bf(ffn_w_gate[l, 0]), bf(ffn_w_up[l, 0]), bf(ffn_w_down[l, 0]),
            ln_g[l, 0][None], ln_b[l, 0][None], wqkv, wqkm, wvm, wom, wgt, wgt.T)
        att = _attention(qkv, bias, B, S).reshape(B * S, att_w)
        hm = _mlstm(qkm, vm, om, gates, gatest, conv_w[l], conv_b[l][None], gb[None], gb[:, None],
                    ml_norm_g[l][None], B, S)

        hd = D // XA_HEADS
        kt, v = _memkv(mem, bf(xkv_w[l][:, :D].T), bf(xkv_w[l][:, D:]))
        xf = _tail(x1, att, hm, kt, v, bf(w_out[l][:att_w]), bf(w_out[l][att_w:]),
                   bf(xq_w[l] * hd ** -0.5), bf(xo_w[l]),
                   bf(ffn_w_gate[l, 1]), bf(ffn_w_up[l, 1]), bf(ffn_w_down[l, 1]),
                   ln_g[l, 1:4], ln_b[l, 1:4], S)
    return xf.reshape(B, S, D)
```

```python
import functools
import math

import jax
import jax.numpy as jnp
from jax import lax
from jax.experimental import pallas as pl
from jax.experimental.pallas import tpu as pltpu

F32 = jnp.float32
BF16 = jnp.bfloat16

ATT_HD = 64
ATT_HEADS = 8
DILATED = ((128, 1), (512, 4), (2048, 16))
BLK = 128
ML_HEADS = 4
CHUNK = 128
CONV_K = 4
XA_HEADS = 4
REL_BUCKETS = 32
REL_MAX_DIST = 2048
DEPTH = 1
ALPHA = (2 * DEPTH) ** 0.25
LN_EPS = 1e-5
NEG = -1e30

LANES = 128
VMEM_LIMIT = 56 * 1024 * 1024
ROW_TILE = 256
ATT_ORDER = (2, 1, 0)
ATT_UNROLL = 8


def _const_spec(shape):
    nd = len(shape)
    return pl.BlockSpec(shape, lambda *_: (0,) * nd, pipeline_mode=pl.Buffered(1))


def _layer_norm(y, g, b):
    mu = jnp.mean(y, -1, keepdims=True)
    yc = y - mu
    var = jnp.mean(yc * yc, -1, keepdims=True)
    return yc * lax.rsqrt(var + LN_EPS) * g + b


def _silu(x):
    return x * jax.nn.sigmoid(x)


def _swiglu(xb, wg_ref, wu_ref, wd_ref):
    g = jnp.dot(xb, wg_ref[...], preferred_element_type=F32)
    u = jnp.dot(xb, wu_ref[...], preferred_element_type=F32)
    h = (_silu(g) * u).astype(BF16)
    return jnp.dot(h, wd_ref[...], preferred_element_type=F32)


def _bias_kernel(rel_ref, bkt_ref, out_ref):
    bkt = bkt_ref[0]
    prev_half = lax.broadcasted_iota(jnp.int32, bkt.shape, 1) < BLK
    for h in range(ATT_HEADS):
        acc = jnp.full(bkt.shape, NEG, F32)
        for b in range(REL_BUCKETS):
            acc = jnp.where(bkt == b, rel_ref[b, h], acc)
        out_ref[0, 0, h] = acc
        out_ref[1, 0, h] = jnp.where(prev_half, NEG, acc)


def _bias_tables(rel_bias):
    qi = jnp.arange(BLK)[:, None]
    ki = jnp.arange(2 * BLK)[None, :]
    off = qi + BLK - ki
    exact = REL_BUCKETS // 2
    tabs = []
    for window, dil in DILATED:
        n_keys = window // dil
        dist = dil * jnp.clip(off, 0, n_keys)
        df = jnp.maximum(dist, 1).astype(F32)
        large = exact + (jnp.log(df / exact) / math.log(REL_MAX_DIST / exact)
                         * (REL_BUCKETS - exact)).astype(jnp.int32)
        large = jnp.minimum(large, REL_BUCKETS - 1)
        bucket = jnp.where(dist < exact, dist, large)
        band = (off >= 0) & (off <= n_keys)
        tabs.append(jnp.where(band, bucket, -1))
    bkt = jnp.stack(tabs, 0).astype(jnp.int32)
    nd = len(DILATED)
    return pl.pallas_call(
        _bias_kernel,
        grid=(nd,),
        in_specs=[pl.BlockSpec(memory_space=pltpu.SMEM),
                  pl.BlockSpec((1, BLK, 2 * BLK), lambda d: (d, 0, 0))],
        out_specs=pl.BlockSpec((2, 1, ATT_HEADS, BLK, 2 * BLK), lambda d: (0, d, 0, 0, 0)),
        out_shape=jax.ShapeDtypeStruct((2, nd, ATT_HEADS, BLK, 2 * BLK), F32),
        name="bias",
    )(rel_bias.astype(F32), bkt)


def _memkv_kernel(mem_ref, wkt_ref, wv_ref, kt_ref, v_ref):
    mb = mem_ref[...].astype(BF16)
    kt = lax.dot_general(wkt_ref[...], mb, (((1,), (1,)), ((), ())),
                         preferred_element_type=F32)
    kt_ref[...] = kt.astype(BF16)
    v_ref[...] = jnp.dot(mb, wv_ref[...], preferred_element_type=F32).astype(BF16)


def _memkv(mem, wkt, wv):
    B, L, D = mem.shape
    return pl.pallas_call(
        _memkv_kernel,
        grid=(B,),
        in_specs=[pl.BlockSpec((None, L, D), lambda b: (b, 0, 0)),
                  _const_spec((D, D)), _const_spec((D, D))],
        out_specs=[pl.BlockSpec((None, D, L), lambda b: (b, 0, 0)),
                   pl.BlockSpec((None, L, D), lambda b: (b, 0, 0))],
        out_shape=[jax.ShapeDtypeStruct((B, D, L), BF16),
                   jax.ShapeDtypeStruct((B, L, D), BF16)],
        compiler_params=pltpu.CompilerParams(vmem_limit_bytes=VMEM_LIMIT),
        name="memkv",
    )(mem, wkt, wv)


def _ffn_in_kernel(x_ref, wg_ref, wu_ref, wd_ref, g_ref, b_ref,
                   wqkv_ref, wqkm_ref, wvm_ref, wom_ref, wgt_ref, wgtt_ref,
                   x1_ref, qkv_ref, qkm_ref, vm_ref, om_ref, gates_ref, gatest_ref):
    x = x_ref[...]
    y = ALPHA * x + 0.5 * _swiglu(x.astype(BF16), wg_ref, wu_ref, wd_ref)
    x1 = _layer_norm(y, g_ref[...], b_ref[...])
    x1_ref[...] = x1
    xb = x1.astype(BF16)
    qkv_ref[...] = jnp.dot(xb, wqkv_ref[...], preferred_element_type=F32)
    qkm_ref[...] = jnp.dot(xb, wqkm_ref[...], preferred_element_type=F32)
    vm_ref[...] = jnp.dot(xb, wvm_ref[...], preferred_element_type=F32).astype(BF16)
    om_ref[...] = jnp.dot(xb, wom_ref[...], preferred_element_type=F32)
    gates_ref[...] = jnp.dot(xb, wgt_ref[...], preferred_element_type=F32)
    gatest_ref[...] = lax.dot_general(wgtt_ref[...], xb, (((1,), (1,)), ((), ())),
                                      preferred_element_type=F32)


def _ffn_in(x2d, wg, wu, wd, g, b, wqkv, wqkm, wvm, wom, wgt, wgtt):
    T, D = x2d.shape
    tm = ROW_TILE
    row = lambda w: pl.BlockSpec((tm, w), lambda i: (i, 0))
    n_att, n_qkm, n_ml, n_g = wqkv.shape[1], wqkm.shape[1], wvm.shape[1], wgt.shape[1]
    weights = (wg, wu, wd, g, b, wqkv, wqkm, wvm, wom, wgt, wgtt)
    return pl.pallas_call(
        _ffn_in_kernel,
        grid=(T // tm,),
        in_specs=[row(D)] + [_const_spec(w.shape) for w in weights],
        out_specs=[row(D), row(n_att), row(n_qkm), row(n_ml), row(n_ml), row(n_g),
                   pl.BlockSpec((n_g, tm), lambda i: (0, i))],
        out_shape=[jax.ShapeDtypeStruct((T, D), F32),
                   jax.ShapeDtypeStruct((T, n_att), F32),
                   jax.ShapeDtypeStruct((T, n_qkm), F32),
                   jax.ShapeDtypeStruct((T, n_ml), BF16),
                   jax.ShapeDtypeStruct((T, n_ml), F32),
                   jax.ShapeDtypeStruct((T, n_g), F32),
                   jax.ShapeDtypeStruct((n_g, T), F32)],
        compiler_params=pltpu.CompilerParams(
            dimension_semantics=("parallel",), vmem_limit_bytes=VMEM_LIMIT),
        name="ffn_in",
    )(x2d, *weights)


def _attn_block(q_ref, k_ref, v_ref, bias_ref, out_ref, acc_s, m_s, l_s, j, step):
    di = ATT_ORDER[step]
    dil = DILATED[di][1]
    lane = lax.broadcasted_iota(jnp.int32, (BLK, LANES), 1)
    head0 = lane < ATT_HD
    r = j % dil
    n = j // dil
    span = BLK * dil

    def rows_at(nn):
        start = r + nn * span
        if dil == 1:
            return pl.ds(pl.multiple_of(start, BLK), BLK)
        return pl.ds(start, BLK, stride=dil)

    rows = rows_at(n)
    prev = rows_at(jnp.maximum(n - 1, 0))
    q = q_ref[rows, :]
    kk = jnp.concatenate([k_ref[prev, :].astype(BF16), k_ref[rows, :].astype(BF16)], axis=0)
    vv = jnp.concatenate([v_ref[prev, :].astype(BF16), v_ref[rows, :].astype(BF16)], axis=0)
    first = jnp.where(n == 0, 1, 0)
    ms, ls, os_ = [], [], []
    for h in range(2):
        sel = head0 if h == 0 else jnp.logical_not(head0)
        qh = jnp.where(sel, q, 0.0).astype(BF16)
        lg = lax.dot_general(qh, kk, (((1,), (1,)), ((), ())), preferred_element_type=F32)
        lg = lg + bias_ref[first, di, h]
        mh = jnp.max(lg, -1, keepdims=True)
        p = jnp.exp(lg - mh)
        ls.append(jnp.sum(p, -1, keepdims=True))
        ms.append(mh)
        os_.append(jnp.dot(p.astype(BF16), vv, preferred_element_type=F32))
    m_b = jnp.where(head0, ms[0], ms[1])
    l_b = jnp.where(head0, ls[0], ls[1])
    o_b = jnp.where(head0, os_[0], os_[1])
    if step > 0:
        m_o = m_s[rows, :]
        m_n = jnp.maximum(m_o, m_b)
        e_o = jnp.exp(m_o - m_n)
        e_b = jnp.exp(m_b - m_n)
        l_b = l_s[rows, :] * e_o + l_b * e_b
        o_b = acc_s[rows, :] * e_o + o_b * e_b
        m_b = m_n
    if step < len(ATT_ORDER) - 1:
        m_s[rows, :] = m_b
        l_s[rows, :] = l_b
        acc_s[rows, :] = o_b
    else:
        out_ref[rows, :] = (o_b / l_b).astype(out_ref.dtype)


def _attn_kernel(q_ref, k_ref, v_ref, bias_ref, out_ref, acc_s, m_s, l_s):
    n_blocks = q_ref.shape[0] // BLK
    for step in range(len(ATT_ORDER)):
        def body(i, _, step=step):
            for u in range(ATT_UNROLL):
                _attn_block(q_ref, k_ref, v_ref, bias_ref, out_ref, acc_s, m_s, l_s,
                            i * ATT_UNROLL + u, step)
            return 0
        lax.fori_loop(0, n_blocks // ATT_UNROLL, body, 0)


def _attention(qkv, bias, B, S):
    n_pairs = ATT_HEADS * ATT_HD // LANES
    qkv3 = qkv.reshape(B, S, qkv.shape[-1])
    col = lambda off: pl.BlockSpec((None, S, LANES), lambda b, p: (b, 0, off + p))
    assert DILATED[ATT_ORDER[-1]][1] == 1
    return pl.pallas_call(
        _attn_kernel,
        grid=(B, n_pairs),
        in_specs=[col(0), col(n_pairs), col(2 * n_pairs),
                  pl.BlockSpec((2, len(DILATED), 2, BLK, 2 * BLK),
                               lambda b, p: (0, 0, p, 0, 0))],
        out_specs=pl.BlockSpec((None, S, LANES), lambda b, p: (b, 0, p)),
        out_shape=jax.ShapeDtypeStruct((B, S, n_pairs * LANES), BF16),
        scratch_shapes=[pltpu.VMEM((S, LANES), F32)] * 3,
        compiler_params=pltpu.CompilerParams(
            dimension_semantics=("parallel", "parallel"), vmem_limit_bytes=VMEM_LIMIT),
        name="attn",
    )(qkv3, qkv3, qkv3, bias)


def _split3(a):
    hi = a.astype(BF16)
    r1 = a - hi.astype(F32)
    mid = r1.astype(BF16)
    lo = (r1 - mid.astype(F32)).astype(BF16)
    return hi, mid, lo


def _log_sigmoid(x):
    return jnp.minimum(x, 0.0) - jnp.log1p(jnp.exp(-jnp.abs(x)))


def _mlstm_kernel(qk_ref, v_ref, o_ref, g_ref, gt_ref, cw_ref, cb_ref, gbr_ref, gbc_ref,
                  mlg_ref, tri_ref, out_ref, xbuf, ct_s, n_s, m_s):
    c = pl.program_id(1)
    E = LANES
    n_qk = qk_ref.shape[1] // 2

    @pl.when(c == 0)
    def _():
        xbuf[0:8, :] = jnp.zeros((8, xbuf.shape[1]), F32)
        ct_s[...] = jnp.zeros(ct_s.shape, F32)
        n_s[...] = jnp.zeros(n_s.shape, F32)
        m_s[...] = jnp.zeros(m_s.shape, F32)

    @pl.when(c > 0)
    def _():
        xbuf[0:8, :] = xbuf[CHUNK:CHUNK + 8, :]

    xbuf[8:8 + CHUNK, :] = qk_ref[...]
    y = cb_ref[...]
    for j in range(CONV_K):
        off = 8 - (CONV_K - 1) + j
        y = y + xbuf[off:off + CHUNK, :] * cw_ref[j:j + 1, :]
    qk = _silu(y)

    gc = g_ref[...] + gbr_ref[...]
    gr = gt_ref[...] + gbc_ref[...]
    tri = tri_ref[...]
    bc_all = sum(jnp.dot(tri, part, preferred_element_type=F32)
                 for part in _split3(_log_sigmoid(gc)))
    br_all = sum(lax.dot_general(part, tri, (((1,), (1,)), ((), ())),
                                 preferred_element_type=F32)
                 for part in _split3(_log_sigmoid(gr)))
    ti = lax.broadcasted_iota(jnp.int32, (CHUNK, CHUNK), 0)
    si = lax.broadcasted_iota(jnp.int32, (CHUNK, CHUNK), 1)
    causal = si <= ti

    for h in range(ML_HEADS):
        ig_c = gc[:, h:h + 1]
        b_c = bc_all[:, ML_HEADS + h:ML_HEADS + h + 1]
        ig_r = gr[h:h + 1, :]
        b_r = br_all[ML_HEADS + h:ML_HEADS + h + 1, :]
        g_tot = b_r[:, CHUNK - 1:CHUNK]
        q = qk[:, h * E:(h + 1) * E]
        k = qk[:, n_qk + h * E:n_qk + (h + 1) * E] * (E ** -0.5)
        v = v_ref[:, h * E:(h + 1) * E]
        ct_prev = ct_s[h]
        n_prev = n_s[h]
        m_prev = m_s[h]
        m_prev1 = m_prev[:, 0:1]

        m_loc = jnp.max(g_tot - b_r + ig_r, -1, keepdims=True)
        wa_c = jnp.exp(g_tot - b_c + ig_c - m_loc)
        kw = wa_c * k
        cl_t = jnp.dot(kw.T.astype(BF16), v, preferred_element_type=F32)
        n_loc = jnp.sum(kw, 0, keepdims=True)

        m_new = jnp.maximum(g_tot + m_prev, m_loc)
        sp = jnp.exp(g_tot + m_prev - m_new)
        sl = jnp.exp(m_loc - m_new)
        ct_s[h] = sp[:, 0:1] * ct_prev + sl[:, 0:1] * cl_t
        n_s[h] = sp * n_prev + sl * n_loc
        m_s[h] = m_new

        d_log = jnp.where(causal, b_c - b_r + ig_r, -jnp.inf)
        e_log = b_c + m_prev1
        m_t = jnp.maximum(e_log, jnp.max(d_log, -1, keepdims=True))
        d_w = jnp.exp(d_log - m_t)
        e_w = jnp.exp(e_log - m_t)
        qb = q.astype(BF16)
        s_qk = lax.dot_general(qb, k.astype(BF16), (((1,), (1,)), ((), ())),
                               preferred_element_type=F32) * d_w
        num = (e_w * jnp.dot(qb, ct_prev.astype(BF16), preferred_element_type=F32)
               + jnp.dot(s_qk.astype(BF16), v, preferred_element_type=F32))
        den = (e_w * jnp.sum(q * n_prev, -1, keepdims=True)
               + jnp.sum(s_qk, -1, keepdims=True))
        hh = num / jnp.maximum(jnp.abs(den), jnp.exp(-m_t))

        hg = jax.nn.sigmoid(o_ref[:, h * E:(h + 1) * E]) * hh
        mu = jnp.mean(hg, -1, keepdims=True)
        hc = hg - mu
        var = jnp.mean(hc * hc, -1, keepdims=True)
        yn = hc * lax.rsqrt(var + LN_EPS) * mlg_ref[:, h * E:(h + 1) * E]
        out_ref[:, h * E:(h + 1) * E] = yn.astype(out_ref.dtype)


def _mlstm(qkm, vm, om, gates, gatest, conv_w, conv_b, gb_row, gb_col, ml_g, B, S):
    nc = S // CHUNK
    W2 = qkm.shape[1]
    W = vm.shape[1]
    G = gates.shape[1]
    tri = jnp.tril(jnp.ones((CHUNK, CHUNK), F32)).astype(BF16)
    row = lambda w: pl.BlockSpec((CHUNK, w), lambda b, c: (b * nc + c, 0))
    consts = (conv_w, conv_b, gb_row, gb_col, ml_g, tri)
    return pl.pallas_call(
        _mlstm_kernel,
        grid=(B, nc),
        in_specs=[row(W2), row(W), row(W), row(G),
                  pl.BlockSpec((G, CHUNK), lambda b, c: (0, b * nc + c))]
                 + [_const_spec(a.shape) for a in consts],
        out_specs=row(W),
        out_shape=jax.ShapeDtypeStruct((B * S, W), BF16),
        scratch_shapes=[pltpu.VMEM((CHUNK + 8, W2), F32),
                        pltpu.VMEM((ML_HEADS, LANES, LANES), F32),
                        pltpu.VMEM((ML_HEADS, 1, LANES), F32),
                        pltpu.VMEM((ML_HEADS, 1, LANES), F32)],
        compiler_params=pltpu.CompilerParams(
            dimension_semantics=("parallel", "arbitrary"), vmem_limit_bytes=VMEM_LIMIT),
        name="mlstm",
    )(qkm, vm, om, gates, gatest, *consts)


def _tail_kernel(x1_ref, att_ref, hm_ref, kt_ref, v_ref, woa_ref, wom_ref, wq_ref, wo_ref,
                 wg_ref, wu_ref, wd_ref, g_ref, b_ref, out_ref):
    mix = (jnp.dot(att_ref[...], woa_ref[...], preferred_element_type=F32)
           + jnp.dot(hm_ref[...], wom_ref[...], preferred_element_type=F32))
    x2 = _layer_norm(ALPHA * x1_ref[...] + mix, g_ref[0:1, :], b_ref[0:1, :])

    q = jnp.dot(x2.astype(BF16), wq_ref[...], preferred_element_type=F32)
    hd = q.shape[1] // XA_HEADS
    heads = []
    for h in range(XA_HEADS):
        qh = q[:, h * hd:(h + 1) * hd].astype(BF16)
        lg = jnp.dot(qh, kt_ref[h * hd:(h + 1) * hd, :], preferred_element_type=F32)
        p = jnp.exp(lg - jnp.max(lg, -1, keepdims=True))
        s = jnp.sum(p, -1, keepdims=True)
        o = jnp.dot(p.astype(BF16), v_ref[:, h * hd:(h + 1) * hd],
                    preferred_element_type=F32)
        heads.append((o / s).astype(BF16))
    xa = jnp.dot(jnp.concatenate(heads, axis=1), wo_ref[...], preferred_element_type=F32)
    x3 = _layer_norm(ALPHA * x2 + xa, g_ref[1:2, :], b_ref[1:2, :])

    y = ALPHA * x3 + 0.5 * _swiglu(x3.astype(BF16), wg_ref, wu_ref, wd_ref)
    out_ref[...] = _layer_norm(y, g_ref[2:3, :], b_ref[2:3, :])


def _tail(x1, att, hm, kt, v, woa, wom, wq, wo, wg, wu, wd, g, b, S):
    T, D = x1.shape
    tm = ROW_TILE
    per_batch = S // tm
    row = lambda w: pl.BlockSpec((tm, w), lambda i: (i, 0))
    L = v.shape[1]
    weights = (woa, wom, wq, wo, wg, wu, wd, g, b)
    return pl.pallas_call(
        _tail_kernel,
        grid=(T // tm,),
        in_specs=[row(D), row(att.shape[1]), row(hm.shape[1]),
                  pl.BlockSpec((None, D, L), lambda i: (i // per_batch, 0, 0)),
                  pl.BlockSpec((None, L, D), lambda i: (i // per_batch, 0, 0))]
                 + [_const_spec(w.shape) for w in weights],
        out_specs=row(D),
        out_shape=jax.ShapeDtypeStruct((T, D), F32),
        compiler_params=pltpu.CompilerParams(
            dimension_semantics=("parallel",), vmem_limit_bytes=VMEM_LIMIT),
        name="tail",
    )(x1, att, hm, kt, v, *weights)


def kernel(x, mem, rel_bias, ln_g, ln_b, ffn_w_gate, ffn_w_up, ffn_w_down, w_in, conv_w, conv_b,
           ig_bias, fg_bias, ml_norm_g, w_out, xq_w, xkv_w, xo_w):
    B, S, D = x.shape
    att_w = ATT_HEADS * ATT_HD
    ml_w = ML_HEADS * LANES
    bias = _bias_tables(rel_bias)
    xf = x.reshape(B * S, D)
    for l in range(DEPTH):
        bf = lambda a: a.astype(BF16)
        wi = w_in[l]
        c0 = 3 * att_w
        scale_q = jnp.concatenate([jnp.full((att_w,), ATT_HD ** -0.5, F32),
                                   jnp.ones((2 * att_w,), F32)])
        wqkv = bf(wi[:, :c0] * scale_q)
        wqkm = bf(wi[:, c0:c0 + 2 * ml_w])
        wvm = bf(wi[:, c0 + 2 * ml_w:c0 + 3 * ml_w])
        wom = bf(wi[:, c0 + 3 * ml_w:c0 + 4 * ml_w])
        wgt = bf(wi[:, c0 + 4 * ml_w:])
        gb = jnp.concatenate([ig_bias[l], fg_bias[l]]).astype(F32)

        x1, qkv, qkm, vm, om, gates, gatest = _ffn_in(
            xf, bf(ffn_w_gate[l, 0]), bf(ffn_w_up[l, 0]), bf(ffn_w_down[l, 0]),
            ln_g[l, 0][None], ln_b[l, 0][None], wqkv, wqkm, wvm, wom, wgt, wgt.T)
        att = _attention(qkv, bias, B, S).reshape(B * S, att_w)
        hm = _mlstm(qkm, vm, om, gates, gatest, conv_w[l], conv_b[l][None], gb[None], gb[:, None],
                    ml_norm_g[l][None], B, S)

        hd = D // XA_HEADS
        kt, v = _memkv(mem, bf(xkv_w[l][:, :D].T), bf(xkv_w[l][:, D:]))
        xf = _tail(x1, att, hm, kt, v, bf(w_out[l][:att_w]), bf(w_out[l][att_w:]),
                   bf(xq_w[l] * hd ** -0.5), bf(xo_w[l]),
                   bf(ffn_w_gate[l, 1]), bf(ffn_w_up[l, 1]), bf(ffn_w_down[l, 1]),
                   ln_g[l, 1:4], ln_b[l, 1:4], S)
    return xf.reshape(B, S, D)
```

```python
import functools
import math

import jax
import jax.numpy as jnp
from jax import lax
from jax.experimental import pallas as pl
from jax.experimental.pallas import tpu as pltpu

F32 = jnp.float32
BF16 = jnp.bfloat16

ATT_HD = 64
ATT_HEADS = 8
DILATED = ((128, 1), (512, 4), (2048, 16))
BLK = 128
ML_HEADS = 4
CHUNK = 128
CONV_K = 4
XA_HEADS = 4
REL_BUCKETS = 32
REL_MAX_DIST = 2048
DEPTH = 1
ALPHA = (2 * DEPTH) ** 0.25
LN_EPS = 1e-5
NEG = -1e30

LANES = 128
VMEM_LIMIT = 56 * 1024 * 1024
ROW_TILE = 512
ATT_ORDER = (2, 1, 0)
ATT_UNROLL = 8
ML_CHUNKS_PER_STEP = 4


def _const_spec(shape):
    nd = len(shape)
    return pl.BlockSpec(shape, lambda *_: (0,) * nd, pipeline_mode=pl.Buffered(1))


def _layer_norm(y, g, b):
    mu = jnp.mean(y, -1, keepdims=True)
    yc = y - mu
    var = jnp.mean(yc * yc, -1, keepdims=True)
    return yc * lax.rsqrt(var + LN_EPS) * g + b


def _silu(x):
    return x * jax.nn.sigmoid(x)


def _swiglu(xb, wg_ref, wu_ref, wd_ref):
    g = jnp.dot(xb, wg_ref[...], preferred_element_type=F32)
    u = jnp.dot(xb, wu_ref[...], preferred_element_type=F32)
    h = (_silu(g) * u).astype(BF16)
    return jnp.dot(h, wd_ref[...], preferred_element_type=F32)


def _bias_kernel(rel_ref, bkt_ref, out_ref):
    bkt = bkt_ref[0]
    prev_half = lax.broadcasted_iota(jnp.int32, bkt.shape, 1) < BLK
    for h in range(ATT_HEADS):
        acc = jnp.full(bkt.shape, NEG, F32)
        for b in range(REL_BUCKETS):
            acc = jnp.where(bkt == b, rel_ref[b, h], acc)
        out_ref[0, 0, h] = acc
        out_ref[1, 0, h] = jnp.where(prev_half, NEG, acc)


def _bias_tables(rel_bias):
    qi = jnp.arange(BLK)[:, None]
    ki = jnp.arange(2 * BLK)[None, :]
    off = qi + BLK - ki
    exact = REL_BUCKETS // 2
    tabs = []
    for window, dil in DILATED:
        n_keys = window // dil
        dist = dil * jnp.clip(off, 0, n_keys)
        df = jnp.maximum(dist, 1).astype(F32)
        large = exact + (jnp.log(df / exact) / math.log(REL_MAX_DIST / exact)
                         * (REL_BUCKETS - exact)).astype(jnp.int32)
        large = jnp.minimum(large, REL_BUCKETS - 1)
        bucket = jnp.where(dist < exact, dist, large)
        band = (off >= 0) & (off <= n_keys)
        tabs.append(jnp.where(band, bucket, -1))
    bkt = jnp.stack(tabs, 0).astype(jnp.int32)
    nd = len(DILATED)
    return pl.pallas_call(
        _bias_kernel,
        grid=(nd,),
        in_specs=[pl.BlockSpec(memory_space=pltpu.SMEM),
                  pl.BlockSpec((1, BLK, 2 * BLK), lambda d: (d, 0, 0))],
        out_specs=pl.BlockSpec((2, 1, ATT_HEADS, BLK, 2 * BLK), lambda d: (0, d, 0, 0, 0)),
        out_shape=jax.ShapeDtypeStruct((2, nd, ATT_HEADS, BLK, 2 * BLK), F32),
        name="bias",
    )(rel_bias.astype(F32), bkt)


def _memkv_kernel(mem_ref, wkt_ref, wv_ref, kt_ref, v_ref):
    mb = mem_ref[...].astype(BF16)
    kt = lax.dot_general(wkt_ref[...], mb, (((1,), (1,)), ((), ())),
                         preferred_element_type=F32)
    kt_ref[...] = kt.astype(BF16)
    v_ref[...] = jnp.dot(mb, wv_ref[...], preferred_element_type=F32).astype(BF16)


def _memkv(mem, wkt, wv):
    B, L, D = mem.shape
    return pl.pallas_call(
        _memkv_kernel,
        grid=(B,),
        in_specs=[pl.BlockSpec((None, L, D), lambda b: (b, 0, 0)),
                  _const_spec((D, D)), _const_spec((D, D))],
        out_specs=[pl.BlockSpec((None, D, L), lambda b: (b, 0, 0)),
                   pl.BlockSpec((None, L, D), lambda b: (b, 0, 0))],
        out_shape=[jax.ShapeDtypeStruct((B, D, L), BF16),
                   jax.ShapeDtypeStruct((B, L, D), BF16)],
        compiler_params=pltpu.CompilerParams(vmem_limit_bytes=VMEM_LIMIT),
        name="memkv",
    )(mem, wkt, wv)


def _ffn_in_kernel(x_ref, wg_ref, wu_ref, wd_ref, g_ref, b_ref,
                   wqkv_ref, wqkm_ref, wvm_ref, wom_ref, wgtt_ref,
                   x1_ref, qkv_ref, qkm_ref, vm_ref, om_ref, gatest_ref):
    x = x_ref[...]
    y = ALPHA * x + 0.5 * _swiglu(x.astype(BF16), wg_ref, wu_ref, wd_ref)
    x1 = _layer_norm(y, g_ref[...], b_ref[...])
    x1_ref[...] = x1
    xb = x1.astype(BF16)
    qkv_ref[...] = jnp.dot(xb, wqkv_ref[...], preferred_element_type=F32)
    qkm_ref[...] = jnp.dot(xb, wqkm_ref[...], preferred_element_type=F32)
    vm_ref[...] = jnp.dot(xb, wvm_ref[...], preferred_element_type=F32).astype(BF16)
    om_ref[...] = jnp.dot(xb, wom_ref[...], preferred_element_type=F32)
    gatest_ref[...] = lax.dot_general(wgtt_ref[...], xb, (((1,), (1,)), ((), ())),
                                      preferred_element_type=F32)


def _ffn_in(x2d, wg, wu, wd, g, b, wqkv, wqkm, wvm, wom, wgtt):
    T, D = x2d.shape
    tm = ROW_TILE
    row = lambda w: pl.BlockSpec((tm, w), lambda i: (i, 0))
    n_att, n_qkm, n_ml, n_g = wqkv.shape[1], wqkm.shape[1], wvm.shape[1], wgtt.shape[0]
    weights = (wg, wu, wd, g, b, wqkv, wqkm, wvm, wom, wgtt)
    return pl.pallas_call(
        _ffn_in_kernel,
        grid=(T // tm,),
        in_specs=[row(D)] + [_const_spec(w.shape) for w in weights],
        out_specs=[row(D), row(n_att), row(n_qkm), row(n_ml), row(n_ml),
                   pl.BlockSpec((n_g, tm), lambda i: (0, i))],
        out_shape=[jax.ShapeDtypeStruct((T, D), F32),
                   jax.ShapeDtypeStruct((T, n_att), F32),
                   jax.ShapeDtypeStruct((T, n_qkm), F32),
                   jax.ShapeDtypeStruct((T, n_ml), BF16),
                   jax.ShapeDtypeStruct((T, n_ml), F32),
                   jax.ShapeDtypeStruct((n_g, T), F32)],
        compiler_params=pltpu.CompilerParams(
            dimension_semantics=("parallel",), vmem_limit_bytes=VMEM_LIMIT),
        name="ffn_in",
    )(x2d, *weights)


def _attn_block(q_ref, k_ref, v_ref, bias_ref, out_ref, acc_s, m_s, l_s, j, step):
    di = ATT_ORDER[step]
    dil = DILATED[di][1]
    lane = lax.broadcasted_iota(jnp.int32, (BLK, LANES), 1)
    head0 = lane < ATT_HD
    r = j % dil
    n = j // dil
    span = BLK * dil

    def rows_at(nn):
        start = r + nn * span
        if dil == 1:
            return pl.ds(pl.multiple_of(start, BLK), BLK)
        return pl.ds(start, BLK, stride=dil)

    rows = rows_at(n)
    prev = rows_at(jnp.maximum(n - 1, 0))
    q = q_ref[rows, :]
    kk = jnp.concatenate([k_ref[prev, :].astype(BF16), k_ref[rows, :].astype(BF16)], axis=0)
    vv = jnp.concatenate([v_ref[prev, :].astype(BF16), v_ref[rows, :].astype(BF16)], axis=0)
    first = jnp.where(n == 0, 1, 0)
    ms, ls, os_ = [], [], []
    for h in range(2):
        sel = head0 if h == 0 else jnp.logical_not(head0)
        qh = jnp.where(sel, q, 0.0).astype(BF16)
        lg = lax.dot_general(qh, kk, (((1,), (1,)), ((), ())), preferred_element_type=F32)
        lg = lg + bias_ref[first, di, h]
        mh = jnp.max(lg, -1, keepdims=True)
        p = jnp.exp(lg - mh)
        ls.append(jnp.sum(p, -1, keepdims=True))
        ms.append(mh)
        os_.append(jnp.dot(p.astype(BF16), vv, preferred_element_type=F32))
    m_b = jnp.where(head0, ms[0], ms[1])
    l_b = jnp.where(head0, ls[0], ls[1])
    o_b = jnp.where(head0, os_[0], os_[1])
    if step > 0:
        m_o = m_s[rows, :]
        m_n = jnp.maximum(m_o, m_b)
        e_o = jnp.exp(m_o - m_n)
        e_b = jnp.exp(m_b - m_n)
        l_b = l_s[rows, :] * e_o + l_b * e_b
        o_b = acc_s[rows, :] * e_o + o_b * e_b
        m_b = m_n
    if step < len(ATT_ORDER) - 1:
        m_s[rows, :] = m_b
        l_s[rows, :] = l_b
        acc_s[rows, :] = o_b
    else:
        out_ref[rows, :] = (o_b / l_b).astype(out_ref.dtype)


def _attn_kernel(q_ref, k_ref, v_ref, bias_ref, out_ref, acc_s, m_s, l_s):
    n_blocks = q_ref.shape[0] // BLK
    for step in range(len(ATT_ORDER)):
        def body(i, _, step=step):
            for u in range(ATT_UNROLL):
                _attn_block(q_ref, k_ref, v_ref, bias_ref, out_ref, acc_s, m_s, l_s,
                            i * ATT_UNROLL + u, step)
            return 0
        lax.fori_loop(0, n_blocks // ATT_UNROLL, body, 0)


def _attention(qkv, bias, B, S):
    n_pairs = ATT_HEADS * ATT_HD // LANES
    qkv3 = qkv.reshape(B, S, qkv.shape[-1])
    col = lambda off: pl.BlockSpec((None, S, LANES), lambda b, p: (b, 0, off + p))
    assert DILATED[ATT_ORDER[-1]][1] == 1
    return pl.pallas_call(
        _attn_kernel,
        grid=(B, n_pairs),
        in_specs=[col(0), col(n_pairs), col(2 * n_pairs),
                  pl.BlockSpec((2, len(DILATED), 2, BLK, 2 * BLK),
                               lambda b, p: (0, 0, p, 0, 0))],
        out_specs=pl.BlockSpec((None, S, LANES), lambda b, p: (b, 0, p)),
        out_shape=jax.ShapeDtypeStruct((B, S, n_pairs * LANES), BF16),
        scratch_shapes=[pltpu.VMEM((S, LANES), F32)] * 3,
        compiler_params=pltpu.CompilerParams(
            dimension_semantics=("parallel", "parallel"), vmem_limit_bytes=VMEM_LIMIT),
        name="attn",
    )(qkv3, qkv3, qkv3, bias)


def _split3(a):
    hi = a.astype(BF16)
    r1 = a - hi.astype(F32)
    mid = r1.astype(BF16)
    lo = (r1 - mid.astype(F32)).astype(BF16)
    return hi, mid, lo


def _log_sigmoid(x):
    return jnp.minimum(x, 0.0) - jnp.log1p(jnp.exp(-jnp.abs(x)))


def _prefix_max_lanes(x):
    lane = lax.broadcasted_iota(jnp.int32, x.shape, 1)
    sh = 1
    while sh < x.shape[1]:
        x = jnp.maximum(x, jnp.where(lane >= sh, pltpu.roll(x, sh, axis=1), -jnp.inf))
        sh *= 2
    return x


def _mlstm_kernel(qk_ref, v_ref, o_ref, gt_ref, cw_ref, cb_ref, gbc_ref, mlg_ref, tri_ref,
                  out_ref, xbuf, ybuf, ct_s, nb_s, m_s):
    step = pl.program_id(1)
    E = LANES
    H = ML_HEADS
    rows = qk_ref.shape[0]
    half = rows // 2
    n_slab = qk_ref.shape[1] // LANES
    n_chunks = rows // CHUNK
    assert n_slab == 2 * H
    assert n_chunks * 4 * 2 * H == LANES

    @pl.when(step == 0)
    def _():
        xbuf[:, 0:8, :] = jnp.zeros((n_slab, 8, LANES), F32)
        ct_s[...] = jnp.zeros(ct_s.shape, F32)
        nb_s[...] = jnp.zeros(nb_s.shape, F32)
        m_s[...] = jnp.zeros(m_s.shape, F32)

    @pl.when(step > 0)
    def _():
        xbuf[:, 0:8, :] = xbuf[:, rows:rows + 8, :]

    for c in range(n_slab):
        xbuf[c, 8:8 + rows, :] = qk_ref[:, c * LANES:(c + 1) * LANES]
    for c in range(n_slab):
        cl = slice(c * LANES, (c + 1) * LANES)
        for par in range(2):
            acc = cb_ref[:, cl]
            for j in range(CONV_K):
                off = 8 - (CONV_K - 1) + j + par
                acc = acc + xbuf[c, pl.ds(off, half, stride=2), :] * cw_ref[j:j + 1, cl]
            act = _silu(acc)
            if c >= H:
                act = act * (E ** -0.5)
            ybuf[c, pl.ds(par, half, stride=2), :] = act

    tri = tri_ref[...]
    head_row = lax.broadcasted_iota(jnp.int32, (2 * H, CHUNK), 0) < H
    m_prev = m_s[...]
    factors, u_rows, scales = [], [], []
    for i in range(n_chunks):
        gr = gt_ref[:, i * CHUNK:(i + 1) * CHUNK] + gbc_ref[...]
        b_all = sum(lax.dot_general(part, tri, (((1,), (1,)), ((), ())),
                                    preferred_element_type=F32)
                    for part in _split3(_log_sigmoid(gr)))
        b = jnp.where(head_row, pltpu.roll(b_all, H, axis=0), 0.0)
        u = jnp.where(head_row, gr - b, 0.0)
        cm = _prefix_max_lanes(u)
        u_max = jnp.broadcast_to(cm[:, CHUNK - 1:CHUNK], cm.shape)
        g_tot = jnp.broadcast_to(b[:, CHUNK - 1:CHUNK], b.shape)
        mm = jnp.maximum(m_prev, u_max)
        scales.append((jnp.exp(m_prev - mm), jnp.exp(u_max - mm)))
        m_in = jnp.maximum(m_prev, cm)
        factors += [m_in, jnp.exp(m_prev - m_in), jnp.exp(u - u_max), jnp.exp(-(b + m_in))]
        u_rows.append(u)
        m_prev = g_tot + mm
    m_s[...] = m_prev
    cols = jnp.concatenate(factors, axis=0).T

    ti = lax.broadcasted_iota(jnp.int32, (CHUNK, CHUNK), 0)
    si = lax.broadcasted_iota(jnp.int32, (CHUNK, CHUNK), 1)
    causal = si <= ti
    ones = jnp.ones((CHUNK, E), BF16)

    for h in range(H):
        ct = ct_s[h]
        nb = nb_s[h]
        for i in range(n_chunks):
            r0 = i * CHUNK
            col = lambda q_, i=i, h=h: cols[:, i * 8 * H + q_ * 2 * H + h:
                                            i * 8 * H + q_ * 2 * H + h + 1]
            m_in, e_w, wa, e_m = col(0), col(1), col(2), col(3)
            q = ybuf[h, r0:r0 + CHUNK, :]
            k = ybuf[H + h, r0:r0 + CHUNK, :]
            v1 = jnp.concatenate([v_ref[r0:r0 + CHUNK, h * E:(h + 1) * E], ones], axis=1)
            qb = q.astype(BF16)

            d_w = jnp.where(causal, jnp.exp(u_rows[i][h:h + 1, :] - m_in), 0.0)
            s_qk = lax.dot_general(qb, k.astype(BF16), (((1,), (1,)), ((), ())),
                                   preferred_element_type=F32) * d_w
            st = jnp.concatenate([ct.astype(BF16), nb.astype(BF16)], axis=1)
            inter = jnp.dot(qb, st, preferred_element_type=F32)
            intra = jnp.dot(s_qk.astype(BF16), v1, preferred_element_type=F32)
            num = e_w * inter[:, :E] + intra[:, :E]
            den = e_w * inter[:, E:] + intra[:, E:]
            hh = num / jnp.maximum(jnp.abs(den), e_m)

            hg = jax.nn.sigmoid(o_ref[r0:r0 + CHUNK, h * E:(h + 1) * E]) * hh
            mu = jnp.mean(hg, -1, keepdims=True)
            hc = hg - mu
            var = jnp.mean(hc * hc, -1, keepdims=True)
            yn = hc * lax.rsqrt(var + LN_EPS) * mlg_ref[:, h * E:(h + 1) * E]
            out_ref[r0:r0 + CHUNK, h * E:(h + 1) * E] = yn.astype(out_ref.dtype)

            kw = (wa * k).T.astype(BF16)
            loc = jnp.dot(kw, v1, preferred_element_type=F32)
            sp, sl = scales[i]
            sp_h, sl_h = sp[h:h + 1, :], sl[h:h + 1, :]
            ct = sp_h * ct + sl_h * loc[:, :E]
            nb = sp_h * nb + sl_h * loc[:, E:]
        ct_s[h] = ct
        nb_s[h] = nb


def _mlstm(qkm, vm, om, gatest, conv_w, conv_b, gb_col, ml_g, B, S):
    rows = ML_CHUNKS_PER_STEP * CHUNK
    ns = S // rows
    W2 = qkm.shape[1]
    W = vm.shape[1]
    G = gatest.shape[0]
    tri = jnp.tril(jnp.ones((CHUNK, CHUNK), F32)).astype(BF16)
    row = lambda w: pl.BlockSpec((rows, w), lambda b, c: (b * ns + c, 0))
    consts = (conv_w, conv_b, gb_col, ml_g, tri)
    return pl.pallas_call(
        _mlstm_kernel,
        grid=(B, ns),
        in_specs=[row(W2), row(W), row(W),
                  pl.BlockSpec((G, rows), lambda b, c: (0, b * ns + c))]
                 + [_const_spec(a.shape) for a in consts],
        out_specs=row(W),
        out_shape=jax.ShapeDtypeStruct((B * S, W), BF16),
        scratch_shapes=[pltpu.VMEM((W2 // LANES, rows + 8, LANES), F32),
                        pltpu.VMEM((W2 // LANES, rows, LANES), F32),
                        pltpu.VMEM((ML_HEADS, LANES, LANES), F32),
                        pltpu.VMEM((ML_HEADS, LANES, LANES), F32),
                        pltpu.VMEM((2 * ML_HEADS, LANES), F32)],
        compiler_params=pltpu.CompilerParams(
            dimension_semantics=("parallel", "arbitrary"), vmem_limit_bytes=VMEM_LIMIT),
        name="mlstm",
    )(qkm, vm, om, gatest, *consts)


def _tail_kernel(x1_ref, att_ref, hm_ref, kt_ref, v_ref, woa_ref, wom_ref, wq_ref, wo_ref,
                 wg_ref, wu_ref, wd_ref, g_ref, b_ref, out_ref):
    mix = (jnp.dot(att_ref[...], woa_ref[...], preferred_element_type=F32)
           + jnp.dot(hm_ref[...], wom_ref[...], preferred_element_type=F32))
    x2 = _layer_norm(ALPHA * x1_ref[...] + mix, g_ref[0:1, :], b_ref[0:1, :])

    q = jnp.dot(x2.astype(BF16), wq_ref[...], preferred_element_type=F32)
    hd = q.shape[1] // XA_HEADS
    heads = []
    for h in range(XA_HEADS):
        qh = q[:, h * hd:(h + 1) * hd].astype(BF16)
        lg = jnp.dot(qh, kt_ref[h * hd:(h + 1) * hd, :], preferred_element_type=F32)
        p = jnp.exp(lg - jnp.max(lg, -1, keepdims=True))
        s = jnp.sum(p, -1, keepdims=True)
        o = jnp.dot(p.astype(BF16), v_ref[:, h * hd:(h + 1) * hd],
                    preferred_element_type=F32)
        heads.append((o / s).astype(BF16))
    xa = jnp.dot(jnp.concatenate(heads, axis=1), wo_ref[...], preferred_element_type=F32)
    x3 = _layer_norm(ALPHA * x2 + xa, g_ref[1:2, :], b_ref[1:2, :])

    y = ALPHA * x3 + 0.5 * _swiglu(x3.astype(BF16), wg_ref, wu_ref, wd_ref)
    out_ref[...] = _layer_norm(y, g_ref[2:3, :], b_ref[2:3, :])


def _tail(x1, att, hm, kt, v, woa, wom, wq, wo, wg, wu, wd, g, b, S):
    T, D = x1.shape
    tm = ROW_TILE
    per_batch = S // tm
    row = lambda w: pl.BlockSpec((tm, w), lambda i: (i, 0))
    L = v.shape[1]
    weights = (woa, wom, wq, wo, wg, wu, wd, g, b)
    return pl.pallas_call(
        _tail_kernel,
        grid=(T // tm,),
        in_specs=[row(D), row(att.shape[1]), row(hm.shape[1]),
                  pl.BlockSpec((None, D, L), lambda i: (i // per_batch, 0, 0)),
                  pl.BlockSpec((None, L, D), lambda i: (i // per_batch, 0, 0))]
                 + [_const_spec(w.shape) for w in weights],
        out_specs=row(D),
        out_shape=jax.ShapeDtypeStruct((T, D), F32),
        compiler_params=pltpu.CompilerParams(
            dimension_semantics=("parallel",), vmem_limit_bytes=VMEM_LIMIT),
        name="tail",
    )(x1, att, hm, kt, v, *weights)


def kernel(x, mem, rel_bias, ln_g, ln_b, ffn_w_gate, ffn_w_up, ffn_w_down, w_in, conv_w, conv_b,
           ig_bias, fg_bias, ml_norm_g, w_out, xq_w, xkv_w, xo_w):
    B, S, D = x.shape
    att_w = ATT_HEADS * ATT_HD
    ml_w = ML_HEADS * LANES
    bias = _bias_tables(rel_bias)
    xf = x.reshape(B * S, D)
    for l in range(DEPTH):
        bf = lambda a: a.astype(BF16)
        wi = w_in[l]
        c0 = 3 * att_w
        scale_q = jnp.concatenate([jnp.full((att_w,), ATT_HD ** -0.5, F32),
                                   jnp.ones((2 * att_w,), F32)])
        wqkv = bf(wi[:, :c0] * scale_q)
        wqkm = bf(wi[:, c0:c0 + 2 * ml_w])
        wvm = bf(wi[:, c0 + 2 * ml_w:c0 + 3 * ml_w])
        wom = bf(wi[:, c0 + 3 * ml_w:c0 + 4 * ml_w])
        wgt = bf(wi[:, c0 + 4 * ml_w:])
        gb = jnp.concatenate([ig_bias[l], fg_bias[l]]).astype(F32)

        x1, qkv, qkm, vm, om, gatest = _ffn_in(
            xf, bf(ffn_w_gate[l, 0]), bf(ffn_w_up[l, 0]), bf(ffn_w_down[l, 0]),
            ln_g[l, 0][None], ln_b[l, 0][None], wqkv, wqkm, wvm, wom, wgt.T)
        att = _attention(qkv, bias, B, S).reshape(B * S, att_w)
        hm = _mlstm(qkm, vm, om, gatest, conv_w[l], conv_b[l][None], gb[:, None],
                    ml_norm_g[l][None], B, S)

        hd = D // XA_HEADS
        kt, v = _memkv(mem, bf(xkv_w[l][:, :D].T), bf(xkv_w[l][:, D:]))
        xf = _tail(x1, att, hm, kt, v, bf(w_out[l][:att_w]), bf(w_out[l][att_w:]),
                   bf(xq_w[l] * hd ** -0.5), bf(xo_w[l]),
                   bf(ffn_w_gate[l, 1]), bf(ffn_w_up[l, 1]), bf(ffn_w_down[l, 1]),
                   ln_g[l, 1:4], ln_b[l, 1:4], S)
    return xf.reshape(B, S, D)
```

```python
import functools
import math

import jax
import jax.numpy as jnp
from jax import lax
from jax.experimental import pallas as pl
from jax.experimental.pallas import tpu as pltpu

F32 = jnp.float32
BF16 = jnp.bfloat16

ATT_HD = 64
ATT_HEADS = 8
DILATED = ((128, 1), (512, 4), (2048, 16))
BLK = 128
ML_HEADS = 4
CHUNK = 128
CONV_K = 4
XA_HEADS = 4
REL_BUCKETS = 32
REL_MAX_DIST = 2048
DEPTH = 1
ALPHA = (2 * DEPTH) ** 0.25
LN_EPS = 1e-5
NEG = -1e30
LOG2E = math.log2(math.e)

LANES = 128
VMEM_LIMIT = 56 * 1024 * 1024
ROW_TILE = 512
ROW_SUBTILES = 1
ATT_ORDER = (2, 1, 0)
ATT_UNROLL = 8
ML_CHUNKS_PER_STEP = 4


def _const_spec(shape):
    nd = len(shape)
    return pl.BlockSpec(shape, lambda *_: (0,) * nd, pipeline_mode=pl.Buffered(1))


def _layer_norm(y, g, b):
    mu = jnp.mean(y, -1, keepdims=True)
    yc = y - mu
    var = jnp.mean(yc * yc, -1, keepdims=True)
    return yc * lax.rsqrt(var + LN_EPS) * g + b


def _silu(x):
    return x * jax.nn.sigmoid(x)


def _swiglu(xb, wg_ref, wu_ref, wd_ref):
    g = jnp.dot(xb, wg_ref[...], preferred_element_type=F32)
    u = jnp.dot(xb, wu_ref[...], preferred_element_type=F32)
    h = (_silu(g) * u).astype(BF16)
    return jnp.dot(h, wd_ref[...], preferred_element_type=F32)


def _bias_kernel(rel_ref, bkt_ref, out_ref):
    bkt = bkt_ref[0]
    prev_half = lax.broadcasted_iota(jnp.int32, bkt.shape, 1) < BLK
    for h in range(ATT_HEADS):
        acc = jnp.full(bkt.shape, NEG, F32)
        for b in range(REL_BUCKETS):
            acc = jnp.where(bkt == b, rel_ref[b, h] * LOG2E, acc)
        out_ref[0, 0, h] = acc
        out_ref[1, 0, h] = jnp.where(prev_half, NEG, acc)


def _bias_tables(rel_bias):
    qi = jnp.arange(BLK)[:, None]
    ki = jnp.arange(2 * BLK)[None, :]
    off = qi + BLK - ki
    exact = REL_BUCKETS // 2
    tabs = []
    for window, dil in DILATED:
        n_keys = window // dil
        dist = dil * jnp.clip(off, 0, n_keys)
        df = jnp.maximum(dist, 1).astype(F32)
        large = exact + (jnp.log(df / exact) / math.log(REL_MAX_DIST / exact)
                         * (REL_BUCKETS - exact)).astype(jnp.int32)
        large = jnp.minimum(large, REL_BUCKETS - 1)
        bucket = jnp.where(dist < exact, dist, large)
        band = (off >= 0) & (off <= n_keys)
        tabs.append(jnp.where(band, bucket, -1))
    bkt = jnp.stack(tabs, 0).astype(jnp.int32)
    nd = len(DILATED)
    return pl.pallas_call(
        _bias_kernel,
        grid=(nd,),
        in_specs=[pl.BlockSpec(memory_space=pltpu.SMEM),
                  pl.BlockSpec((1, BLK, 2 * BLK), lambda d: (d, 0, 0))],
        out_specs=pl.BlockSpec((2, 1, ATT_HEADS, BLK, 2 * BLK), lambda d: (0, d, 0, 0, 0)),
        out_shape=jax.ShapeDtypeStruct((2, nd, ATT_HEADS, BLK, 2 * BLK), F32),
        name="bias",
    )(rel_bias.astype(F32), bkt)


def _memkv_kernel(mem_ref, wkt_ref, wv_ref, kt_ref, v_ref):
    mb = mem_ref[...].astype(BF16)
    kt = lax.dot_general(wkt_ref[...], mb, (((1,), (1,)), ((), ())),
                         preferred_element_type=F32)
    kt_ref[...] = kt.astype(BF16)
    v_ref[...] = jnp.dot(mb, wv_ref[...], preferred_element_type=F32).astype(BF16)


def _memkv(mem, wkt, wv):
    B, L, D = mem.shape
    return pl.pallas_call(
        _memkv_kernel,
        grid=(B,),
        in_specs=[pl.BlockSpec((None, L, D), lambda b: (b, 0, 0)),
                  _const_spec((D, D)), _const_spec((D, D))],
        out_specs=[pl.BlockSpec((None, D, L), lambda b: (b, 0, 0)),
                   pl.BlockSpec((None, L, D), lambda b: (b, 0, 0))],
        out_shape=[jax.ShapeDtypeStruct((B, D, L), BF16),
                   jax.ShapeDtypeStruct((B, L, D), BF16)],
        compiler_params=pltpu.CompilerParams(vmem_limit_bytes=VMEM_LIMIT),
        name="memkv",
    )(mem, wkt, wv)


def _ffn_in_kernel(x_ref, wg_ref, wu_ref, wd_ref, g_ref, b_ref,
                   wqkv_ref, wqkm_ref, wvm_ref, wom_ref, wgtt_ref,
                   x1_ref, qkv_ref, qk_ref, vm_ref, og_ref, gatest_ref):
    x = x_ref[...]
    y = ALPHA * x + 0.5 * _swiglu(x.astype(BF16), wg_ref, wu_ref, wd_ref)
    x1 = _layer_norm(y, g_ref[...], b_ref[...])
    x1_ref[...] = x1
    xb = x1.astype(BF16)

    qkm = jnp.dot(xb, wqkm_ref[...], preferred_element_type=F32)
    for c in range(qk_ref.shape[0]):
        qk_ref[c] = qkm[:, c * LANES:(c + 1) * LANES]

    qkv_ref[...] = jnp.dot(xb, wqkv_ref[...], preferred_element_type=F32)
    vm_ref[...] = jnp.dot(xb, wvm_ref[...], preferred_element_type=F32).astype(BF16)
    og_ref[...] = jax.nn.sigmoid(jnp.dot(xb, wom_ref[...], preferred_element_type=F32))
    gatest_ref[...] = lax.dot_general(wgtt_ref[...], xb, (((1,), (1,)), ((), ())),
                                      preferred_element_type=F32)


def _ffn_in(x2d, wg, wu, wd, g, b, wqkv, wqkm, wvm, wom, wgtt):
    T, D = x2d.shape
    tm = ROW_TILE
    row = lambda w: pl.BlockSpec((tm, w), lambda i: (i, 0))
    n_att, n_qkm, n_ml, n_g = wqkv.shape[1], wqkm.shape[1], wvm.shape[1], wgtt.shape[0]
    n_slab = n_qkm // LANES
    weights = (wg, wu, wd, g, b, wqkv, wqkm, wvm, wom, wgtt)
    return pl.pallas_call(
        _ffn_in_kernel,
        grid=(T // tm,),
        in_specs=[row(D)] + [_const_spec(w.shape) for w in weights],
        out_specs=[row(D), row(n_att),
                   pl.BlockSpec((n_slab, tm, LANES), lambda i: (0, i, 0)),
                   row(n_ml), row(n_ml),
                   pl.BlockSpec((n_g, tm), lambda i: (0, i))],
        out_shape=[jax.ShapeDtypeStruct((T, D), F32),
                   jax.ShapeDtypeStruct((T, n_att), F32),
                   jax.ShapeDtypeStruct((n_slab, T, LANES), F32),
                   jax.ShapeDtypeStruct((T, n_ml), BF16),
                   jax.ShapeDtypeStruct((T, n_ml), F32),
                   jax.ShapeDtypeStruct((n_g, T), F32)],
        compiler_params=pltpu.CompilerParams(
            dimension_semantics=("parallel",), vmem_limit_bytes=VMEM_LIMIT),
        name="ffn_in",
    )(x2d, *weights)


def _attn_block(q_ref, k_ref, v_ref, bias_ref, out_ref, acc_s, m_s, l_s, j, step):
    di = ATT_ORDER[step]
    dil = DILATED[di][1]
    lane = lax.broadcasted_iota(jnp.int32, (BLK, LANES), 1)
    head0 = lane < ATT_HD
    r = j % dil
    n = j // dil
    span = BLK * dil

    def rows_at(nn):
        start = r + nn * span
        if dil == 1:
            return pl.ds(pl.multiple_of(start, BLK), BLK)
        return pl.ds(start, BLK, stride=dil)

    rows = rows_at(n)
    prev = rows_at(jnp.maximum(n - 1, 0))
    q = q_ref[rows, :]
    kk = jnp.concatenate([k_ref[prev, :].astype(BF16), k_ref[rows, :].astype(BF16)], axis=0)
    vv = jnp.concatenate([v_ref[prev, :].astype(BF16), v_ref[rows, :].astype(BF16)], axis=0)
    first = jnp.where(n == 0, 1, 0)
    ms, ls, os_ = [], [], []
    for h in range(2):
        sel = head0 if h == 0 else jnp.logical_not(head0)
        qh = jnp.where(sel, q, 0.0).astype(BF16)
        lg = lax.dot_general(qh, kk, (((1,), (1,)), ((), ())), preferred_element_type=F32)
        lg = lg + bias_ref[first, di, h]
        mh = jnp.max(lg, -1, keepdims=True)
        p = jnp.exp2(lg - mh)
        ls.append(jnp.sum(p, -1, keepdims=True))
        ms.append(mh)
        os_.append(jnp.dot(p.astype(BF16), vv, preferred_element_type=F32))
    m_b = jnp.where(head0, ms[0], ms[1])
    l_b = jnp.where(head0, ls[0], ls[1])
    o_b = jnp.where(head0, os_[0], os_[1])
    if step > 0:
        m_o = m_s[rows, :]
        m_n = jnp.maximum(m_o, m_b)
        e_o = jnp.exp2(m_o - m_n)
        e_b = jnp.exp2(m_b - m_n)
        l_b = l_s[rows, :] * e_o + l_b * e_b
        o_b = acc_s[rows, :] * e_o + o_b * e_b
        m_b = m_n
    if step < len(ATT_ORDER) - 1:
        m_s[rows, :] = m_b
        l_s[rows, :] = l_b
        acc_s[rows, :] = o_b
    else:
        out_ref[rows, :] = (o_b / l_b).astype(out_ref.dtype)


def _attn_kernel(q_ref, k_ref, v_ref, bias_ref, out_ref, acc_s, m_s, l_s):
    n_blocks = q_ref.shape[0] // BLK
    for step in range(len(ATT_ORDER)):
        def body(i, _, step=step):
            for u in range(ATT_UNROLL):
                _attn_block(q_ref, k_ref, v_ref, bias_ref, out_ref, acc_s, m_s, l_s,
                            i * ATT_UNROLL + u, step)
            return 0
        lax.fori_loop(0, n_blocks // ATT_UNROLL, body, 0)


def _attention(qkv, bias, B, S):
    n_pairs = ATT_HEADS * ATT_HD // LANES
    qkv3 = qkv.reshape(B, S, qkv.shape[-1])
    col = lambda off: pl.BlockSpec((None, S, LANES), lambda b, p: (b, 0, off + p))
    assert DILATED[ATT_ORDER[-1]][1] == 1
    return pl.pallas_call(
        _attn_kernel,
        grid=(B, n_pairs),
        in_specs=[col(0), col(n_pairs), col(2 * n_pairs),
                  pl.BlockSpec((2, len(DILATED), 2, BLK, 2 * BLK),
                               lambda b, p: (0, 0, p, 0, 0))],
        out_specs=pl.BlockSpec((None, S, LANES), lambda b, p: (b, 0, p)),
        out_shape=jax.ShapeDtypeStruct((B, S, n_pairs * LANES), BF16),
        scratch_shapes=[pltpu.VMEM((S, LANES), F32)] * 3,
        compiler_params=pltpu.CompilerParams(
            dimension_semantics=("parallel", "parallel"), vmem_limit_bytes=VMEM_LIMIT),
        name="attn",
    )(qkv3, qkv3, qkv3, bias)


def _split3(a):
    hi = a.astype(BF16)
    r1 = a - hi.astype(F32)
    mid = r1.astype(BF16)
    lo = (r1 - mid.astype(F32)).astype(BF16)
    return hi, mid, lo


def _log_sigmoid(x):
    return jnp.minimum(x, 0.0) - jnp.log1p(jnp.exp(-jnp.abs(x)))


def _prefix_max_lanes(x):
    lane = lax.broadcasted_iota(jnp.int32, x.shape, 1)
    sh = 1
    while sh < x.shape[1]:
        x = jnp.maximum(x, jnp.where(lane >= sh, pltpu.roll(x, sh, axis=1), -jnp.inf))
        sh *= 2
    return x


def _mlstm_kernel(qk_ref, v_ref, og_ref, gt_ref, cw_ref, cb_ref, gbc_ref, mlg_ref, tri_ref,
                  out_ref, xbuf, ybuf, ct_s, nb_s, m_s):
    step = pl.program_id(1)
    E = LANES
    H = ML_HEADS
    n_slab, rows = qk_ref.shape[0], qk_ref.shape[1]
    half = rows // 2
    n_chunks = rows // CHUNK
    assert n_slab == 2 * H
    assert (n_chunks * 4 * 2 * H) % LANES == 0

    @pl.when(step == 0)
    def _():
        xbuf[:, 0:8, :] = jnp.zeros((n_slab, 8, LANES), F32)
        ct_s[...] = jnp.zeros(ct_s.shape, F32)
        nb_s[...] = jnp.zeros(nb_s.shape, F32)
        m_s[...] = jnp.zeros(m_s.shape, F32)

    @pl.when(step > 0)
    def _():
        xbuf[:, 0:8, :] = xbuf[:, rows:rows + 8, :]

    xbuf[:, 8:8 + rows, :] = qk_ref[...]
    for c in range(n_slab):
        cl = slice(c * LANES, (c + 1) * LANES)
        for par in range(2):
            acc = cb_ref[:, cl]
            for j in range(CONV_K):
                off = 8 - (CONV_K - 1) + j + par
                acc = acc + xbuf[c, pl.ds(off, half, stride=2), :] * cw_ref[j:j + 1, cl]
            act = _silu(acc)
            if c >= H:
                act = act * (E ** -0.5)
            ybuf[c, pl.ds(par, half, stride=2), :] = act

    tri = tri_ref[...]
    n_rows = 2 * H * n_chunks
    head_row = (lax.broadcasted_iota(jnp.int32, (n_rows, CHUNK), 0) & (2 * H - 1)) < H
    gr = jnp.concatenate([gt_ref[:, i * CHUNK:(i + 1) * CHUNK] + gbc_ref[...]
                          for i in range(n_chunks)], axis=0)
    b_all = sum(lax.dot_general(part, tri, (((1,), (1,)), ((), ())),
                                preferred_element_type=F32)
                for part in _split3(_log_sigmoid(gr)))
    b = jnp.concatenate([pltpu.roll(b_all[i * 2 * H:(i + 1) * 2 * H], H, axis=0)
                         for i in range(n_chunks)], axis=0)
    b = jnp.where(head_row, b, 0.0)
    u = jnp.where(head_row, gr - b, 0.0)
    cm = _prefix_max_lanes(u)
    u_max = jnp.broadcast_to(cm[:, CHUNK - 1:CHUNK], cm.shape)
    g_tot = jnp.broadcast_to(b[:, CHUNK - 1:CHUNK], b.shape)
    m_prev = m_s[...]
    m_prevs = []
    for i in range(n_chunks):
        rs = slice(i * 2 * H, (i + 1) * 2 * H)
        m_prevs.append(m_prev)
        m_prev = g_tot[rs] + jnp.maximum(m_prev, u_max[rs])
    m_s[...] = m_prev
    m_prev = jnp.concatenate(m_prevs, axis=0)
    mm = jnp.maximum(m_prev, u_max)
    sp_all = jnp.exp(m_prev - mm)
    sl_all = jnp.exp(u_max - mm)
    wa_all = jnp.exp(u - u_max)
    m_in = jnp.maximum(m_prev, cm)
    cols = jnp.concatenate([m_in, jnp.exp(m_prev - m_in), jnp.exp(-(b + m_in)),
                            jnp.zeros_like(m_in)], axis=0).T

    ti = lax.broadcasted_iota(jnp.int32, (CHUNK, CHUNK), 0)
    si = lax.broadcasted_iota(jnp.int32, (CHUNK, CHUNK), 1)
    causal = si <= ti
    ones = jnp.ones((CHUNK, E), BF16)

    cts = [ct_s[h] for h in range(H)]
    nbs = [nb_s[h] for h in range(H)]
    for i in range(n_chunks):
        r0 = i * CHUNK
        for h in range(H):
            row = i * 2 * H + h
            col = lambda q_, row=row: cols[:, q_ * n_rows + row:q_ * n_rows + row + 1]
            m_in_c, e_w, e_m = col(0), col(1), col(2)
            q = ybuf[h, r0:r0 + CHUNK, :]
            k = ybuf[H + h, r0:r0 + CHUNK, :]
            v1 = jnp.concatenate([v_ref[r0:r0 + CHUNK, h * E:(h + 1) * E], ones], axis=1)
            qb = q.astype(BF16)

            d_w = jnp.where(causal, jnp.exp(u[row:row + 1, :] - m_in_c), 0.0)
            s_qk = lax.dot_general(qb, k.astype(BF16), (((1,), (1,)), ((), ())),
                                   preferred_element_type=F32) * d_w
            st = jnp.concatenate([cts[h].astype(BF16), nbs[h].astype(BF16)], axis=1)
            inter = jnp.dot(qb, st, preferred_element_type=F32)
            intra = jnp.dot(s_qk.astype(BF16), v1, preferred_element_type=F32)
            num = e_w * inter[:, :E] + intra[:, :E]
            den = e_w * inter[:, E:] + intra[:, E:]
            hh = num / jnp.maximum(jnp.abs(den), e_m)

            hg = og_ref[r0:r0 + CHUNK, h * E:(h + 1) * E] * hh
            mu = jnp.mean(hg, -1, keepdims=True)
            ex2 = jnp.mean(hg * hg, -1, keepdims=True)
            hc = hg - mu
            var = jnp.maximum(ex2 - mu * mu, 0.0)
            yn = hc * lax.rsqrt(var + LN_EPS) * mlg_ref[:, h * E:(h + 1) * E]
            out_ref[r0:r0 + CHUNK, h * E:(h + 1) * E] = yn.astype(out_ref.dtype)

            kw = (k.T * wa_all[row:row + 1, :]).astype(BF16)
            loc = jnp.dot(kw, v1, preferred_element_type=F32)
            sp_h, sl_h = sp_all[row:row + 1, :], sl_all[row:row + 1, :]
            cts[h] = sp_h * cts[h] + sl_h * loc[:, :E]
            nbs[h] = sp_h * nbs[h] + sl_h * loc[:, E:]
    for h in range(H):
        ct_s[h] = cts[h]
        nb_s[h] = nbs[h]


def _mlstm(qk, vm, og, gatest, conv_w, conv_b, gb_col, ml_g, B, S):
    rows = ML_CHUNKS_PER_STEP * CHUNK
    ns = S // rows
    W = vm.shape[1]
    G = gatest.shape[0]
    tri = jnp.tril(jnp.ones((CHUNK, CHUNK), F32)).astype(BF16)
    row = lambda w: pl.BlockSpec((rows, w), lambda b, c: (b * ns + c, 0))
    consts = (conv_w, conv_b, gb_col, ml_g, tri)
    return pl.pallas_call(
        _mlstm_kernel,
        grid=(B, ns),
        in_specs=[pl.BlockSpec((qk.shape[0], rows, LANES), lambda b, c: (0, b * ns + c, 0)),
                  row(W), row(W),
                  pl.BlockSpec((G, rows), lambda b, c: (0, b * ns + c))]
                 + [_const_spec(a.shape) for a in consts],
        out_specs=row(W),
        out_shape=jax.ShapeDtypeStruct((B * S, W), BF16),
        scratch_shapes=[pltpu.VMEM((qk.shape[0], rows + 8, LANES), F32),
                        pltpu.VMEM((qk.shape[0], rows, LANES), F32),
                        pltpu.VMEM((ML_HEADS, LANES, LANES), F32),
                        pltpu.VMEM((ML_HEADS, LANES, LANES), F32),
                        pltpu.VMEM((2 * ML_HEADS, LANES), F32)],
        compiler_params=pltpu.CompilerParams(
            dimension_semantics=("parallel", "arbitrary"), vmem_limit_bytes=VMEM_LIMIT),
        name="mlstm",
    )(qk, vm, og, gatest, *consts)


def _tail_kernel(x1_ref, att_ref, hm_ref, kt_ref, v_ref, woa_ref, wom_ref, wq_ref, wo_ref,
                 wg_ref, wu_ref, wd_ref, g_ref, b_ref, out_ref):
    sub = x1_ref.shape[0] // ROW_SUBTILES
    for t in range(ROW_SUBTILES):
        rs = slice(t * sub, (t + 1) * sub)
        mix = (jnp.dot(att_ref[rs, :], woa_ref[...], preferred_element_type=F32)
               + jnp.dot(hm_ref[rs, :], wom_ref[...], preferred_element_type=F32))
        x2 = _layer_norm(ALPHA * x1_ref[rs, :] + mix, g_ref[0:1, :], b_ref[0:1, :])

        q = jnp.dot(x2.astype(BF16), wq_ref[...], preferred_element_type=F32)
        hd = q.shape[1] // XA_HEADS
        heads = []
        for h in range(XA_HEADS):
            qh = q[:, h * hd:(h + 1) * hd].astype(BF16)
            lg = jnp.dot(qh, kt_ref[h * hd:(h + 1) * hd, :], preferred_element_type=F32)
            p = jnp.exp(lg - jnp.max(lg, -1, keepdims=True))
            s = jnp.sum(p, -1, keepdims=True)
            o = jnp.dot(p.astype(BF16), v_ref[:, h * hd:(h + 1) * hd],
                        preferred_element_type=F32)
            heads.append((o / s).astype(BF16))
        xa = jnp.dot(jnp.concatenate(heads, axis=1), wo_ref[...], preferred_element_type=F32)
        x3 = _layer_norm(ALPHA * x2 + xa, g_ref[1:2, :], b_ref[1:2, :])

        y = ALPHA * x3 + 0.5 * _swiglu(x3.astype(BF16), wg_ref, wu_ref, wd_ref)
        out_ref[rs, :] = _layer_norm(y, g_ref[2:3, :], b_ref[2:3, :])


def _tail(x1, att, hm, kt, v, woa, wom, wq, wo, wg, wu, wd, g, b, S):
    T, D = x1.shape
    tm = ROW_TILE
    per_batch = S // tm
    row = lambda w: pl.BlockSpec((tm, w), lambda i: (i, 0))
    L = v.shape[1]
    weights = (woa, wom, wq, wo, wg, wu, wd, g, b)
    return pl.pallas_call(
        _tail_kernel,
        grid=(T // tm,),
        in_specs=[row(D), row(att.shape[1]), row(hm.shape[1]),
                  pl.BlockSpec((None, D, L), lambda i: (i // per_batch, 0, 0)),
                  pl.BlockSpec((None, L, D), lambda i: (i // per_batch, 0, 0))]
                 + [_const_spec(w.shape) for w in weights],
        out_specs=row(D),
        out_shape=jax.ShapeDtypeStruct((T, D), F32),
        compiler_params=pltpu.CompilerParams(
            dimension_semantics=("parallel",), vmem_limit_bytes=VMEM_LIMIT),
        name="tail",
    )(x1, att, hm, kt, v, *weights)


def kernel(x, mem, rel_bias, ln_g, ln_b, ffn_w_gate, ffn_w_up, ffn_w_down, w_in, conv_w, conv_b,
           ig_bias, fg_bias, ml_norm_g, w_out, xq_w, xkv_w, xo_w):
    B, S, D = x.shape
    att_w = ATT_HEADS * ATT_HD
    ml_w = ML_HEADS * LANES
    bias = _bias_tables(rel_bias)
    xf = x.reshape(B * S, D)
    for l in range(DEPTH):
        bf = lambda a: a.astype(BF16)
        wi = w_in[l]
        c0 = 3 * att_w
        scale_q = jnp.concatenate([jnp.full((att_w,), ATT_HD ** -0.5 * LOG2E, F32),
                                   jnp.ones((2 * att_w,), F32)])
        wqkv = bf(wi[:, :c0] * scale_q)
        wqkm = bf(wi[:, c0:c0 + 2 * ml_w])
        wvm = bf(wi[:, c0 + 2 * ml_w:c0 + 3 * ml_w])
        wom = bf(wi[:, c0 + 3 * ml_w:c0 + 4 * ml_w])
        wgt = bf(wi[:, c0 + 4 * ml_w:])
        gb = jnp.concatenate([ig_bias[l], fg_bias[l]]).astype(F32)

        x1, qkv, qk, vm, og, gatest = _ffn_in(
            xf, bf(ffn_w_gate[l, 0]), bf(ffn_w_up[l, 0]), bf(ffn_w_down[l, 0]),
            ln_g[l, 0][None], ln_b[l, 0][None], wqkv, wqkm, wvm, wom, wgt.T)
        att = _attention(qkv, bias, B, S).reshape(B * S, att_w)
        hm = _mlstm(qk, vm, og, gatest, conv_w[l], conv_b[l][None], gb[:, None],
                    ml_norm_g[l][None], B, S)

        hd = D // XA_HEADS
        kt, v = _memkv(mem, bf(xkv_w[l][:, :D].T), bf(xkv_w[l][:, D:]))
        xf = _tail(x1, att, hm, kt, v, bf(w_out[l][:att_w]), bf(w_out[l][att_w:]),
                   bf(xq_w[l] * hd ** -0.5), bf(xo_w[l]),
                   bf(ffn_w_gate[l, 1]), bf(ffn_w_up[l, 1]), bf(ffn_w_down[l, 1]),
                   ln_g[l, 1:4], ln_b[l, 1:4], S)
    return xf.reshape(B, S, D)
```

```python
import functools
import math

import jax
import jax.numpy as jnp
from jax import lax
from jax.experimental import pallas as pl
from jax.experimental.pallas import tpu as pltpu

F32 = jnp.float32
BF16 = jnp.bfloat16

ATT_HD = 64
ATT_HEADS = 8
DILATED = ((128, 1), (512, 4), (2048, 16))
BLK = 128
ML_HEADS = 4
CHUNK = 128
CONV_K = 4
XA_HEADS = 4
REL_BUCKETS = 32
REL_MAX_DIST = 2048
DEPTH = 1
ALPHA = (2 * DEPTH) ** 0.25
LN_EPS = 1e-5
NEG = -1e30
LOG2E = math.log2(math.e)

LANES = 128
VMEM_LIMIT = 56 * 1024 * 1024
ROW_TILE = 512
ROW_SUBTILES = 1
ATT_ORDER = (2, 1, 0)
ATT_UNROLL = 8
ML_CHUNKS_PER_STEP = 4


def _const_spec(shape):
    nd = len(shape)
    return pl.BlockSpec(shape, lambda *_: (0,) * nd, pipeline_mode=pl.Buffered(1))


def _layer_norm(y, g, b):
    mu = jnp.mean(y, -1, keepdims=True)
    yc = y - mu
    var = jnp.mean(yc * yc, -1, keepdims=True)
    return yc * lax.rsqrt(var + LN_EPS) * g + b


def _silu(x):
    return x * jax.nn.sigmoid(x)


def _swiglu(xb, wg_ref, wu_ref, wd_ref):
    g = jnp.dot(xb, wg_ref[...], preferred_element_type=F32)
    u = jnp.dot(xb, wu_ref[...], preferred_element_type=F32)
    h = (_silu(g) * u).astype(BF16)
    return jnp.dot(h, wd_ref[...], preferred_element_type=F32)


def _bias_kernel(rel_ref, bkt_ref, out_ref):
    bkt = bkt_ref[0]
    prev_half = lax.broadcasted_iota(jnp.int32, bkt.shape, 1) < BLK
    accs = [jnp.full(bkt.shape, NEG, F32) for _ in range(ATT_HEADS)]
    for b in range(REL_BUCKETS):
        hit = bkt == b
        for h in range(ATT_HEADS):
            accs[h] = jnp.where(hit, rel_ref[b, h] * LOG2E, accs[h])
    for h in range(ATT_HEADS):
        out_ref[0, 0, h] = accs[h]
        out_ref[1, 0, h] = jnp.where(prev_half, NEG, accs[h])


def _bias_tables(rel_bias):
    qi = jnp.arange(BLK)[:, None]
    ki = jnp.arange(2 * BLK)[None, :]
    off = qi + BLK - ki
    exact = REL_BUCKETS // 2
    n_log = REL_BUCKETS - exact
    starts = [math.ceil(exact * (REL_MAX_DIST / exact) ** (k / n_log)) for k in range(1, n_log)]
    tabs = []
    for window, dil in DILATED:
        n_keys = window // dil
        dist = dil * jnp.clip(off, 0, n_keys)
        large = exact + sum((dist >= s).astype(jnp.int32) for s in starts)
        bucket = jnp.where(dist < exact, dist, large)
        band = (off >= 0) & (off <= n_keys)
        tabs.append(jnp.where(band, bucket, -1))
    bkt = jnp.stack(tabs, 0).astype(jnp.int32)
    nd = len(DILATED)
    return pl.pallas_call(
        _bias_kernel,
        grid=(nd,),
        in_specs=[pl.BlockSpec(memory_space=pltpu.SMEM),
                  pl.BlockSpec((1, BLK, 2 * BLK), lambda d: (d, 0, 0))],
        out_specs=pl.BlockSpec((2, 1, ATT_HEADS, BLK, 2 * BLK), lambda d: (0, d, 0, 0, 0)),
        out_shape=jax.ShapeDtypeStruct((2, nd, ATT_HEADS, BLK, 2 * BLK), F32),
        name="bias",
    )(rel_bias.astype(F32), bkt)


def _memkv_kernel(mem_ref, wkv_ref, kt_ref, v_ref):
    D = mem_ref.shape[1]
    mb = mem_ref[...].astype(BF16)
    k = jnp.dot(mb, wkv_ref[:, 0:D], preferred_element_type=F32)
    kt_ref[...] = k.T.astype(BF16)
    v_ref[...] = jnp.dot(mb, wkv_ref[:, D:], preferred_element_type=F32).astype(BF16)


def _memkv(mem, wkv):
    B, L, D = mem.shape
    return pl.pallas_call(
        _memkv_kernel,
        grid=(B,),
        in_specs=[pl.BlockSpec((None, L, D), lambda b: (b, 0, 0)),
                  _const_spec((D, 2 * D))],
        out_specs=[pl.BlockSpec((None, D, L), lambda b: (b, 0, 0)),
                   pl.BlockSpec((None, L, D), lambda b: (b, 0, 0))],
        out_shape=[jax.ShapeDtypeStruct((B, D, L), BF16),
                   jax.ShapeDtypeStruct((B, L, D), BF16)],
        compiler_params=pltpu.CompilerParams(vmem_limit_bytes=VMEM_LIMIT),
        name="memkv",
    )(mem, wkv)


def _ffn_in_kernel(x_ref, wg_ref, wu_ref, wd_ref, g_ref, b_ref, win_ref, wgtt_ref,
                   x1_ref, qkv_ref, qk_ref, vm_ref, og_ref, gatest_ref):
    n_q = qkv_ref.shape[1] // 3
    c1 = qkv_ref.shape[1]
    c2 = c1 + qk_ref.shape[0] * LANES
    c3 = c2 + vm_ref.shape[1]
    c4 = c3 + og_ref.shape[1]
    x = x_ref[...]
    y = ALPHA * x + 0.5 * _swiglu(x.astype(BF16), wg_ref, wu_ref, wd_ref)
    x1 = _layer_norm(y, g_ref[...], b_ref[...])
    x1_ref[...] = x1
    xb = x1.astype(BF16)

    qkm = jnp.dot(xb, win_ref[:, c1:c2], preferred_element_type=F32)
    for c in range(qk_ref.shape[0]):
        qk_ref[c] = qkm[:, c * LANES:(c + 1) * LANES]

    qkv = jnp.dot(xb, win_ref[:, 0:c1], preferred_element_type=F32)
    qkv_ref[:, 0:n_q] = qkv[:, 0:n_q] * (ATT_HD ** -0.5 * LOG2E)
    qkv_ref[:, n_q:c1] = qkv[:, n_q:c1]
    vm_ref[...] = jnp.dot(xb, win_ref[:, c2:c3], preferred_element_type=F32).astype(BF16)
    og_ref[...] = jax.nn.sigmoid(jnp.dot(xb, win_ref[:, c3:c4], preferred_element_type=F32))
    gatest_ref[...] = lax.dot_general(wgtt_ref[...], xb, (((1,), (1,)), ((), ())),
                                      preferred_element_type=F32)


def _ffn_in(x2d, wg, wu, wd, g, b, win, wgtt, n_att, n_qkm, n_ml):
    T, D = x2d.shape
    tm = ROW_TILE
    row = lambda w: pl.BlockSpec((tm, w), lambda i: (i, 0))
    n_g = wgtt.shape[0]
    n_slab = n_qkm // LANES
    weights = (wg, wu, wd, g, b, win, wgtt)
    return pl.pallas_call(
        _ffn_in_kernel,
        grid=(T // tm,),
        in_specs=[row(D)] + [_const_spec(w.shape) for w in weights],
        out_specs=[row(D), row(n_att),
                   pl.BlockSpec((n_slab, tm, LANES), lambda i: (0, i, 0)),
                   row(n_ml), row(n_ml),
                   pl.BlockSpec((n_g, tm), lambda i: (0, i))],
        out_shape=[jax.ShapeDtypeStruct((T, D), F32),
                   jax.ShapeDtypeStruct((T, n_att), F32),
                   jax.ShapeDtypeStruct((n_slab, T, LANES), F32),
                   jax.ShapeDtypeStruct((T, n_ml), BF16),
                   jax.ShapeDtypeStruct((T, n_ml), F32),
                   jax.ShapeDtypeStruct((n_g, T), F32)],
        compiler_params=pltpu.CompilerParams(
            dimension_semantics=("parallel",), vmem_limit_bytes=VMEM_LIMIT),
        name="ffn_in",
    )(x2d, *weights)


def _attn_block(q_ref, k_ref, v_ref, bias_ref, out_ref, acc_s, m_s, l_s, j, step):
    di = ATT_ORDER[step]
    dil = DILATED[di][1]
    lane = lax.broadcasted_iota(jnp.int32, (BLK, LANES), 1)
    head0 = lane < ATT_HD
    r = j % dil
    n = j // dil
    span = BLK * dil

    def rows_at(nn):
        start = r + nn * span
        if dil == 1:
            return pl.ds(pl.multiple_of(start, BLK), BLK)
        return pl.ds(start, BLK, stride=dil)

    rows = rows_at(n)
    prev = rows_at(jnp.maximum(n - 1, 0))
    q = q_ref[rows, :]
    kk = jnp.concatenate([k_ref[prev, :].astype(BF16), k_ref[rows, :].astype(BF16)], axis=0)
    vv = jnp.concatenate([v_ref[prev, :].astype(BF16), v_ref[rows, :].astype(BF16)], axis=0)
    first = jnp.where(n == 0, 1, 0)
    ms, ls, os_ = [], [], []
    for h in range(2):
        sel = head0 if h == 0 else jnp.logical_not(head0)
        qh = jnp.where(sel, q, 0.0).astype(BF16)
        lg = lax.dot_general(qh, kk, (((1,), (1,)), ((), ())), preferred_element_type=F32)
        lg = lg + bias_ref[first, di, h]
        mh = jnp.max(lg, -1, keepdims=True)
        p = jnp.exp2(lg - mh)
        ls.append(jnp.sum(p, -1, keepdims=True))
        ms.append(mh)
        os_.append(jnp.dot(p.astype(BF16), vv, preferred_element_type=F32))
    m_b = jnp.where(head0, ms[0], ms[1])
    l_b = jnp.where(head0, ls[0], ls[1])
    o_b = jnp.where(head0, os_[0], os_[1])
    if step > 0:
        m_o = m_s[rows, :]
        m_n = jnp.maximum(m_o, m_b)
        e_o = jnp.exp2(m_o - m_n)
        e_b = jnp.exp2(m_b - m_n)
        l_b = l_s[rows, :] * e_o + l_b * e_b
        o_b = acc_s[rows, :] * e_o + o_b * e_b
        m_b = m_n
    if step < len(ATT_ORDER) - 1:
        m_s[rows, :] = m_b
        l_s[rows, :] = l_b
        acc_s[rows, :] = o_b
    else:
        out_ref[rows, :] = (o_b / l_b).astype(out_ref.dtype)


def _attn_kernel(q_ref, k_ref, v_ref, bias_ref, out_ref, acc_s, m_s, l_s):
    n_blocks = q_ref.shape[0] // BLK
    for step in range(len(ATT_ORDER)):
        def body(i, _, step=step):
            for u in range(ATT_UNROLL):
                _attn_block(q_ref, k_ref, v_ref, bias_ref, out_ref, acc_s, m_s, l_s,
                            i * ATT_UNROLL + u, step)
            return 0
        lax.fori_loop(0, n_blocks // ATT_UNROLL, body, 0)


def _attention(qkv, bias, B, S):
    n_pairs = ATT_HEADS * ATT_HD // LANES
    qkv3 = qkv.reshape(B, S, qkv.shape[-1])
    col = lambda off: pl.BlockSpec((None, S, LANES), lambda b, p: (b, 0, off + p))
    assert DILATED[ATT_ORDER[-1]][1] == 1
    return pl.pallas_call(
        _attn_kernel,
        grid=(B, n_pairs),
        in_specs=[col(0), col(n_pairs), col(2 * n_pairs),
                  pl.BlockSpec((2, len(DILATED), 2, BLK, 2 * BLK),
                               lambda b, p: (0, 0, p, 0, 0))],
        out_specs=pl.BlockSpec((None, S, LANES), lambda b, p: (b, 0, p)),
        out_shape=jax.ShapeDtypeStruct((B, S, n_pairs * LANES), BF16),
        scratch_shapes=[pltpu.VMEM((S, LANES), F32)] * 3,
        compiler_params=pltpu.CompilerParams(
            dimension_semantics=("parallel", "parallel"), vmem_limit_bytes=VMEM_LIMIT),
        name="attn",
    )(qkv3, qkv3, qkv3, bias)


def _split3(a):
    hi = a.astype(BF16)
    r1 = a - hi.astype(F32)
    mid = r1.astype(BF16)
    lo = (r1 - mid.astype(F32)).astype(BF16)
    return hi, mid, lo


def _log_sigmoid(x):
    return jnp.minimum(x, 0.0) - jnp.log1p(jnp.exp(-jnp.abs(x)))


def _prefix_max_lanes(x):
    lane = lax.broadcasted_iota(jnp.int32, x.shape, 1)
    sh = 1
    while sh < x.shape[1]:
        x = jnp.maximum(x, jnp.where(lane >= sh, pltpu.roll(x, sh, axis=1), -jnp.inf))
        sh *= 2
    return x


def _mlstm_kernel(qk_ref, v_ref, og_ref, gt_ref, cw_ref, cb_ref, gbc_ref, mlg_ref, tri_ref,
                  out_ref, xbuf, ybuf, ct_s, nb_s, m_s):
    step = pl.program_id(1)
    E = LANES
    H = ML_HEADS
    n_slab, rows = qk_ref.shape[0], qk_ref.shape[1]
    half = rows // 2
    n_chunks = rows // CHUNK
    assert n_slab == 2 * H
    assert (n_chunks * 4 * 2 * H) % LANES == 0

    @pl.when(step == 0)
    def _():
        xbuf[:, 0:8, :] = jnp.zeros((n_slab, 8, LANES), F32)
        ct_s[...] = jnp.zeros(ct_s.shape, F32)
        nb_s[...] = jnp.zeros(nb_s.shape, F32)
        m_s[...] = jnp.zeros(m_s.shape, F32)

    @pl.when(step > 0)
    def _():
        xbuf[:, 0:8, :] = xbuf[:, rows:rows + 8, :]

    xbuf[:, 8:8 + rows, :] = qk_ref[...]
    for c in range(n_slab):
        cl = slice(c * LANES, (c + 1) * LANES)
        for par in range(2):
            acc = cb_ref[:, cl]
            for j in range(CONV_K):
                off = 8 - (CONV_K - 1) + j + par
                acc = acc + xbuf[c, pl.ds(off, half, stride=2), :] * cw_ref[j:j + 1, cl]
            act = _silu(acc)
            if c >= H:
                act = act * (E ** -0.5)
            ybuf[c, pl.ds(par, half, stride=2), :] = act

    tri = tri_ref[...]
    n_rows = 2 * H * n_chunks
    head_row = (lax.broadcasted_iota(jnp.int32, (n_rows, CHUNK), 0) & (2 * H - 1)) < H
    gr = jnp.concatenate([gt_ref[:, i * CHUNK:(i + 1) * CHUNK] + gbc_ref[...]
                          for i in range(n_chunks)], axis=0)
    b_all = sum(lax.dot_general(part, tri, (((1,), (1,)), ((), ())),
                                preferred_element_type=F32)
                for part in _split3(_log_sigmoid(gr)))
    b = jnp.concatenate([pltpu.roll(b_all[i * 2 * H:(i + 1) * 2 * H], H, axis=0)
                         for i in range(n_chunks)], axis=0)
    b = jnp.where(head_row, b, 0.0)
    u = jnp.where(head_row, gr - b, 0.0)
    cm = _prefix_max_lanes(u)
    u_max = jnp.broadcast_to(cm[:, CHUNK - 1:CHUNK], cm.shape)
    g_tot = jnp.broadcast_to(b[:, CHUNK - 1:CHUNK], b.shape)
    m_prev = m_s[...]
    m_prevs = []
    for i in range(n_chunks):
        rs = slice(i * 2 * H, (i + 1) * 2 * H)
        m_prevs.append(m_prev)
        m_prev = g_tot[rs] + jnp.maximum(m_prev, u_max[rs])
    m_s[...] = m_prev
    m_prev = jnp.concatenate(m_prevs, axis=0)
    mm = jnp.maximum(m_prev, u_max)
    sp_all = jnp.exp(m_prev - mm)
    sl_all = jnp.exp(u_max - mm)
    wa_all = jnp.exp(u - u_max)
    m_in = jnp.maximum(m_prev, cm)
    cols = jnp.concatenate([m_in, jnp.exp(m_prev - m_in), jnp.exp(-(b + m_in)),
                            jnp.zeros_like(m_in)], axis=0).T

    ti = lax.broadcasted_iota(jnp.int32, (CHUNK, CHUNK), 0)
    si = lax.broadcasted_iota(jnp.int32, (CHUNK, CHUNK), 1)
    causal = si <= ti
    ones = jnp.ones((CHUNK, E), BF16)

    cts = [ct_s[h] for h in range(H)]
    nbs = [nb_s[h] for h in range(H)]
    for i in range(n_chunks):
        r0 = i * CHUNK
        for h in range(H):
            row = i * 2 * H + h
            col = lambda q_, row=row: cols[:, q_ * n_rows + row:q_ * n_rows + row + 1]
            m_in_c, e_w, e_m = col(0), col(1), col(2)
            q = ybuf[h, r0:r0 + CHUNK, :]
            k = ybuf[H + h, r0:r0 + CHUNK, :]
            v1 = jnp.concatenate([v_ref[r0:r0 + CHUNK, h * E:(h + 1) * E], ones], axis=1)
            qb = q.astype(BF16)

            d_w = jnp.where(causal, jnp.exp(u[row:row + 1, :] - m_in_c), 0.0)
            s_qk = lax.dot_general(qb, k.astype(BF16), (((1,), (1,)), ((), ())),
                                   preferred_element_type=F32) * d_w
            st = jnp.concatenate([cts[h].astype(BF16), nbs[h].astype(BF16)], axis=1)
            inter = jnp.dot(qb, st, preferred_element_type=F32)
            intra = jnp.dot(s_qk.astype(BF16), v1, preferred_element_type=F32)
            num = e_w * inter[:, :E] + intra[:, :E]
            den = e_w * inter[:, E:] + intra[:, E:]
            hh = num / jnp.maximum(jnp.abs(den), e_m)

            hg = og_ref[r0:r0 + CHUNK, h * E:(h + 1) * E] * hh
            mu = jnp.mean(hg, -1, keepdims=True)
            ex2 = jnp.mean(hg * hg, -1, keepdims=True)
            hc = hg - mu
            var = jnp.maximum(ex2 - mu * mu, 0.0)
            yn = hc * lax.rsqrt(var + LN_EPS) * mlg_ref[:, h * E:(h + 1) * E]
            out_ref[r0:r0 + CHUNK, h * E:(h + 1) * E] = yn.astype(out_ref.dtype)

            kw = (k.T * wa_all[row:row + 1, :]).astype(BF16)
            loc = jnp.dot(kw, v1, preferred_element_type=F32)
            sp_h, sl_h = sp_all[row:row + 1, :], sl_all[row:row + 1, :]
            cts[h] = sp_h * cts[h] + sl_h * loc[:, :E]
            nbs[h] = sp_h * nbs[h] + sl_h * loc[:, E:]
    for h in range(H):
        ct_s[h] = cts[h]
        nb_s[h] = nbs[h]


def _mlstm(qk, vm, og, gatest, conv_w, conv_b, gb_col, ml_g, B, S):
    rows = ML_CHUNKS_PER_STEP * CHUNK
    ns = S // rows
    W = vm.shape[1]
    G = gatest.shape[0]
    tri = jnp.tril(jnp.ones((CHUNK, CHUNK), F32)).astype(BF16)
    row = lambda w: pl.BlockSpec((rows, w), lambda b, c: (b * ns + c, 0))
    consts = (conv_w, conv_b, gb_col, ml_g, tri)
    return pl.pallas_call(
        _mlstm_kernel,
        grid=(B, ns),
        in_specs=[pl.BlockSpec((qk.shape[0], rows, LANES), lambda b, c: (0, b * ns + c, 0)),
                  row(W), row(W),
                  pl.BlockSpec((G, rows), lambda b, c: (0, b * ns + c))]
                 + [_const_spec(a.shape) for a in consts],
        out_specs=row(W),
        out_shape=jax.ShapeDtypeStruct((B * S, W), BF16),
        scratch_shapes=[pltpu.VMEM((qk.shape[0], rows + 8, LANES), F32),
                        pltpu.VMEM((qk.shape[0], rows, LANES), F32),
                        pltpu.VMEM((ML_HEADS, LANES, LANES), F32),
                        pltpu.VMEM((ML_HEADS, LANES, LANES), F32),
                        pltpu.VMEM((2 * ML_HEADS, LANES), F32)],
        compiler_params=pltpu.CompilerParams(
            dimension_semantics=("parallel", "arbitrary"), vmem_limit_bytes=VMEM_LIMIT),
        name="mlstm",
    )(qk, vm, og, gatest, *consts)


def _tail_kernel(x1_ref, att_ref, hm_ref, kt_ref, v_ref, wout_ref, wq_ref, wo_ref,
                 wg_ref, wu_ref, wd_ref, g_ref, b_ref, out_ref):
    sub = x1_ref.shape[0] // ROW_SUBTILES
    for t in range(ROW_SUBTILES):
        rs = slice(t * sub, (t + 1) * sub)
        n_a = att_ref.shape[1]
        mix = (jnp.dot(att_ref[rs, :], wout_ref[0:n_a, :], preferred_element_type=F32)
               + jnp.dot(hm_ref[rs, :], wout_ref[n_a:, :], preferred_element_type=F32))
        x2 = _layer_norm(ALPHA * x1_ref[rs, :] + mix, g_ref[0:1, :], b_ref[0:1, :])

        q = jnp.dot(x2.astype(BF16), wq_ref[...], preferred_element_type=F32)
        hd = q.shape[1] // XA_HEADS
        heads = []
        for h in range(XA_HEADS):
            qh = q[:, h * hd:(h + 1) * hd].astype(BF16)
            lg = jnp.dot(qh, kt_ref[h * hd:(h + 1) * hd, :], preferred_element_type=F32)
            p = jnp.exp(lg - jnp.max(lg, -1, keepdims=True))
            s = jnp.sum(p, -1, keepdims=True)
            o = jnp.dot(p.astype(BF16), v_ref[:, h * hd:(h + 1) * hd],
                        preferred_element_type=F32)
            heads.append((o / s).astype(BF16))
        xa = jnp.dot(jnp.concatenate(heads, axis=1), wo_ref[...], preferred_element_type=F32)
        x3 = _layer_norm(ALPHA * x2 + xa, g_ref[1:2, :], b_ref[1:2, :])

        y = ALPHA * x3 + 0.5 * _swiglu(x3.astype(BF16), wg_ref, wu_ref, wd_ref)
        out_ref[rs, :] = _layer_norm(y, g_ref[2:3, :], b_ref[2:3, :])


def _tail(x1, att, hm, kt, v, wout, wq, wo, wg, wu, wd, g, b, S):
    T, D = x1.shape
    tm = ROW_TILE
    per_batch = S // tm
    row = lambda w: pl.BlockSpec((tm, w), lambda i: (i, 0))
    L = v.shape[1]
    weights = (wout, wq, wo, wg, wu, wd, g, b)
    return pl.pallas_call(
        _tail_kernel,
        grid=(T // tm,),
        in_specs=[row(D), row(att.shape[1]), row(hm.shape[1]),
                  pl.BlockSpec((None, D, L), lambda i: (i // per_batch, 0, 0)),
                  pl.BlockSpec((None, L, D), lambda i: (i // per_batch, 0, 0))]
                 + [_const_spec(w.shape) for w in weights],
        out_specs=row(D),
        out_shape=jax.ShapeDtypeStruct((T, D), F32),
        compiler_params=pltpu.CompilerParams(
            dimension_semantics=("parallel",), vmem_limit_bytes=VMEM_LIMIT),
        name="tail",
    )(x1, att, hm, kt, v, *weights)


def kernel(x, mem, rel_bias, ln_g, ln_b, ffn_w_gate, ffn_w_up, ffn_w_down, w_in, conv_w, conv_b,
           ig_bias, fg_bias, ml_norm_g, w_out, xq_w, xkv_w, xo_w):
    B, S, D = x.shape
    att_w = ATT_HEADS * ATT_HD
    ml_w = ML_HEADS * LANES
    bias = _bias_tables(rel_bias)
    xf = x.reshape(B * S, D)
    for l in range(DEPTH):
        bf = lambda a: a.astype(BF16)
        win = bf(w_in[l])
        wgtt = win[:, 3 * att_w + 4 * ml_w:].T
        gb = jnp.concatenate([ig_bias[l], fg_bias[l]]).astype(F32)

        x1, qkv, qk, vm, og, gatest = _ffn_in(
            xf, bf(ffn_w_gate[l, 0]), bf(ffn_w_up[l, 0]), bf(ffn_w_down[l, 0]),
            ln_g[l, 0][None], ln_b[l, 0][None], win, wgtt, 3 * att_w, 2 * ml_w, ml_w)
        att = _attention(qkv, bias, B, S).reshape(B * S, att_w)
        hm = _mlstm(qk, vm, og, gatest, conv_w[l], conv_b[l][None], gb[:, None],
                    ml_norm_g[l][None], B, S)

        hd = D // XA_HEADS
        kt, v = _memkv(mem, bf(xkv_w[l]))
        xf = _tail(x1, att, hm, kt, v, bf(w_out[l]), bf(xq_w[l] * hd ** -0.5), bf(xo_w[l]),
                   bf(ffn_w_gate[l, 1]), bf(ffn_w_up[l, 1]), bf(ffn_w_down[l, 1]),
                   ln_g[l, 1:4], ln_b[l, 1:4], S)
    return xf.reshape(B, S, D)
```

```python
import functools
import math

import jax
import jax.numpy as jnp
from jax import lax
from jax.experimental import pallas as pl
from jax.experimental.pallas import tpu as pltpu

F32 = jnp.float32
BF16 = jnp.bfloat16

ATT_HD = 64
ATT_HEADS = 8
DILATED = ((128, 1), (512, 4), (2048, 16))
BLK = 128
ML_HEADS = 4
CHUNK = 128
CONV_K = 4
XA_HEADS = 4
REL_BUCKETS = 32
REL_MAX_DIST = 2048
DEPTH = 1
ALPHA = (2 * DEPTH) ** 0.25
LN_EPS = 1e-5
NEG = -1e30
LOG2E = math.log2(math.e)

LANES = 128
VMEM_LIMIT = 56 * 1024 * 1024
ROW_TILE = 512
ROW_SUBTILES = 1
ATT_ORDER = (2, 1, 0)
ATT_UNROLL = 8
ML_CHUNKS_PER_STEP = 4


def _const_spec(shape, lead=()):
    block = (None,) * len(lead) + tuple(shape[len(lead):])
    index = tuple(lead) + (0,) * (len(shape) - len(lead))
    return pl.BlockSpec(block, lambda *_: index, pipeline_mode=pl.Buffered(1))


def _weight_specs(weights, ffn_index):
    return [_const_spec(w.shape, ffn_index if w.ndim == 4 else ()) for w in weights]


def _layer_norm(y, g, b):
    mu = jnp.mean(y, -1, keepdims=True)
    yc = y - mu
    var = jnp.mean(yc * yc, -1, keepdims=True)
    return yc * lax.rsqrt(var + LN_EPS) * g + b


def _silu(x):
    return x * jax.nn.sigmoid(x)


def _swiglu(xb, wg_ref, wu_ref, wd_ref):
    g = jnp.dot(xb, wg_ref[...], preferred_element_type=F32)
    u = jnp.dot(xb, wu_ref[...], preferred_element_type=F32)
    h = (_silu(g) * u).astype(BF16)
    return jnp.dot(h, wd_ref[...], preferred_element_type=F32)


def _bias_kernel(rel_ref, bkt_ref, out_ref):
    bkt = bkt_ref[0]
    prev_half = lax.broadcasted_iota(jnp.int32, bkt.shape, 1) < BLK
    accs = [jnp.full(bkt.shape, NEG, F32) for _ in range(ATT_HEADS)]
    for b in range(REL_BUCKETS):
        hit = bkt == b
        for h in range(ATT_HEADS):
            accs[h] = jnp.where(hit, rel_ref[b, h] * LOG2E, accs[h])
    for h in range(ATT_HEADS):
        out_ref[0, 0, h] = accs[h]
        out_ref[1, 0, h] = jnp.where(prev_half, NEG, accs[h])


def _bias_tables(rel_bias):
    qi = jnp.arange(BLK)[:, None]
    ki = jnp.arange(2 * BLK)[None, :]
    off = qi + BLK - ki
    exact = REL_BUCKETS // 2
    n_log = REL_BUCKETS - exact
    starts = [math.ceil(exact * (REL_MAX_DIST / exact) ** (k / n_log)) for k in range(1, n_log)]
    tabs = []
    for window, dil in DILATED:
        n_keys = window // dil
        dist = dil * jnp.clip(off, 0, n_keys)
        large = exact + sum((dist >= s).astype(jnp.int32) for s in starts)
        bucket = jnp.where(dist < exact, dist, large)
        band = (off >= 0) & (off <= n_keys)
        tabs.append(jnp.where(band, bucket, -1))
    bkt = jnp.stack(tabs, 0).astype(jnp.int32)
    nd = len(DILATED)
    return pl.pallas_call(
        _bias_kernel,
        grid=(nd,),
        in_specs=[pl.BlockSpec(memory_space=pltpu.SMEM),
                  pl.BlockSpec((1, BLK, 2 * BLK), lambda d: (d, 0, 0))],
        out_specs=pl.BlockSpec((2, 1, ATT_HEADS, BLK, 2 * BLK), lambda d: (0, d, 0, 0, 0)),
        out_shape=jax.ShapeDtypeStruct((2, nd, ATT_HEADS, BLK, 2 * BLK), F32),
        name="bias",
    )(rel_bias.astype(F32), bkt)


def _memkv_kernel(mem_ref, wkv_ref, kt_ref, v_ref):
    D = mem_ref.shape[1]
    mb = mem_ref[...].astype(BF16)
    k = jnp.dot(mb, wkv_ref[:, 0:D], preferred_element_type=F32)
    kt_ref[...] = k.T.astype(BF16)
    v_ref[...] = jnp.dot(mb, wkv_ref[:, D:], preferred_element_type=F32).astype(BF16)


def _memkv(mem, wkv):
    B, L, D = mem.shape
    return pl.pallas_call(
        _memkv_kernel,
        grid=(B,),
        in_specs=[pl.BlockSpec((None, L, D), lambda b: (b, 0, 0)),
                  _const_spec((D, 2 * D))],
        out_specs=[pl.BlockSpec((None, D, L), lambda b: (b, 0, 0)),
                   pl.BlockSpec((None, L, D), lambda b: (b, 0, 0))],
        out_shape=[jax.ShapeDtypeStruct((B, D, L), BF16),
                   jax.ShapeDtypeStruct((B, L, D), BF16)],
        compiler_params=pltpu.CompilerParams(vmem_limit_bytes=VMEM_LIMIT),
        name="memkv",
    )(mem, wkv)


def _ffn_in_kernel(x_ref, wg_ref, wu_ref, wd_ref, g_ref, b_ref, win_ref, wgtt_ref,
                   x1_ref, qkv_ref, qk_ref, vm_ref, og_ref, gatest_ref):
    n_q = qkv_ref.shape[1] // 3
    c1 = qkv_ref.shape[1]
    c2 = c1 + qk_ref.shape[0] * LANES
    c3 = c2 + vm_ref.shape[1]
    c4 = c3 + og_ref.shape[1]
    x = x_ref[...]
    y = ALPHA * x + 0.5 * _swiglu(x.astype(BF16), wg_ref, wu_ref, wd_ref)
    x1 = _layer_norm(y, g_ref[...], b_ref[...])
    x1_ref[...] = x1
    xb = x1.astype(BF16)

    qkm = jnp.dot(xb, win_ref[:, c1:c2], preferred_element_type=F32)
    for c in range(qk_ref.shape[0]):
        qk_ref[c] = qkm[:, c * LANES:(c + 1) * LANES]

    qkv = jnp.dot(xb, win_ref[:, 0:c1], preferred_element_type=F32)
    qkv_ref[:, 0:n_q] = qkv[:, 0:n_q] * (ATT_HD ** -0.5 * LOG2E)
    qkv_ref[:, n_q:c1] = qkv[:, n_q:c1]
    vm_ref[...] = jnp.dot(xb, win_ref[:, c2:c3], preferred_element_type=F32).astype(BF16)
    og_ref[...] = jax.nn.sigmoid(jnp.dot(xb, win_ref[:, c3:c4], preferred_element_type=F32))
    gatest_ref[...] = lax.dot_general(wgtt_ref[...], xb, (((1,), (1,)), ((), ())),
                                      preferred_element_type=F32)


def _ffn_in(x2d, wg, wu, wd, ffn_index, g, b, win, wgtt, n_att, n_qkm, n_ml):
    T, D = x2d.shape
    tm = ROW_TILE
    row = lambda w: pl.BlockSpec((tm, w), lambda i: (i, 0))
    n_g = wgtt.shape[0]
    n_slab = n_qkm // LANES
    weights = (wg, wu, wd, g, b, win, wgtt)
    return pl.pallas_call(
        _ffn_in_kernel,
        grid=(T // tm,),
        in_specs=[row(D)] + _weight_specs(weights, ffn_index),
        out_specs=[row(D), row(n_att),
                   pl.BlockSpec((n_slab, tm, LANES), lambda i: (0, i, 0)),
                   row(n_ml), row(n_ml),
                   pl.BlockSpec((n_g, tm), lambda i: (0, i))],
        out_shape=[jax.ShapeDtypeStruct((T, D), F32),
                   jax.ShapeDtypeStruct((T, n_att), F32),
                   jax.ShapeDtypeStruct((n_slab, T, LANES), F32),
                   jax.ShapeDtypeStruct((T, n_ml), BF16),
                   jax.ShapeDtypeStruct((T, n_ml), F32),
                   jax.ShapeDtypeStruct((n_g, T), F32)],
        compiler_params=pltpu.CompilerParams(
            dimension_semantics=("parallel",), vmem_limit_bytes=VMEM_LIMIT),
        name="ffn_in",
    )(x2d, *weights)


def _attn_block(q_ref, k_ref, v_ref, bias_ref, out_ref, acc_s, m_s, l_s, j, step):
    di = ATT_ORDER[step]
    dil = DILATED[di][1]
    lane = lax.broadcasted_iota(jnp.int32, (BLK, LANES), 1)
    head0 = lane < ATT_HD
    r = j % dil
    n = j // dil
    span = BLK * dil

    def rows_at(nn):
        start = r + nn * span
        if dil == 1:
            return pl.ds(pl.multiple_of(start, BLK), BLK)
        return pl.ds(start, BLK, stride=dil)

    rows = rows_at(n)
    prev = rows_at(jnp.maximum(n - 1, 0))
    q = q_ref[rows, :]
    kk = jnp.concatenate([k_ref[prev, :].astype(BF16), k_ref[rows, :].astype(BF16)], axis=0)
    vv = jnp.concatenate([v_ref[prev, :].astype(BF16), v_ref[rows, :].astype(BF16)], axis=0)
    first = jnp.where(n == 0, 1, 0)
    ms, ls, os_ = [], [], []
    for h in range(2):
        sel = head0 if h == 0 else jnp.logical_not(head0)
        qh = jnp.where(sel, q, 0.0).astype(BF16)
        lg = lax.dot_general(qh, kk, (((1,), (1,)), ((), ())), preferred_element_type=F32)
        lg = lg + bias_ref[first, di, h]
        mh = jnp.max(lg, -1, keepdims=True)
        p = jnp.exp2(lg - mh)
        ls.append(jnp.sum(p, -1, keepdims=True))
        ms.append(mh)
        os_.append(jnp.dot(p.astype(BF16), vv, preferred_element_type=F32))
    m_b = jnp.where(head0, ms[0], ms[1])
    l_b = jnp.where(head0, ls[0], ls[1])
    o_b = jnp.where(head0, os_[0], os_[1])
    if step > 0:
        m_o = m_s[rows, :]
        m_n = jnp.maximum(m_o, m_b)
        e_o = jnp.exp2(m_o - m_n)
        e_b = jnp.exp2(m_b - m_n)
        l_b = l_s[rows, :] * e_o + l_b * e_b
        o_b = acc_s[rows, :] * e_o + o_b * e_b
        m_b = m_n
    if step < len(ATT_ORDER) - 1:
        m_s[rows, :] = m_b
        l_s[rows, :] = l_b
        acc_s[rows, :] = o_b
    else:
        out_ref[rows, :] = (o_b / l_b).astype(out_ref.dtype)


def _attn_kernel(q_ref, k_ref, v_ref, bias_ref, out_ref, acc_s, m_s, l_s):
    n_blocks = q_ref.shape[0] // BLK
    for step in range(len(ATT_ORDER)):
        def body(i, _, step=step):
            for u in range(ATT_UNROLL):
                _attn_block(q_ref, k_ref, v_ref, bias_ref, out_ref, acc_s, m_s, l_s,
                            i * ATT_UNROLL + u, step)
            return 0
        lax.fori_loop(0, n_blocks // ATT_UNROLL, body, 0)


def _attention(qkv, bias, B, S):
    n_pairs = ATT_HEADS * ATT_HD // LANES
    qkv3 = qkv.reshape(B, S, qkv.shape[-1])
    col = lambda off: pl.BlockSpec((None, S, LANES), lambda b, p: (b, 0, off + p))
    assert DILATED[ATT_ORDER[-1]][1] == 1
    return pl.pallas_call(
        _attn_kernel,
        grid=(B, n_pairs),
        in_specs=[col(0), col(n_pairs), col(2 * n_pairs),
                  pl.BlockSpec((2, len(DILATED), 2, BLK, 2 * BLK),
                               lambda b, p: (0, 0, p, 0, 0))],
        out_specs=pl.BlockSpec((None, S, LANES), lambda b, p: (b, 0, p)),
        out_shape=jax.ShapeDtypeStruct((B, S, n_pairs * LANES), BF16),
        scratch_shapes=[pltpu.VMEM((S, LANES), F32)] * 3,
        compiler_params=pltpu.CompilerParams(
            dimension_semantics=("parallel", "parallel"), vmem_limit_bytes=VMEM_LIMIT),
        name="attn",
    )(qkv3, qkv3, qkv3, bias)


def _split3(a):
    hi = a.astype(BF16)
    r1 = a - hi.astype(F32)
    mid = r1.astype(BF16)
    lo = (r1 - mid.astype(F32)).astype(BF16)
    return hi, mid, lo


def _log_sigmoid(x):
    return jnp.minimum(x, 0.0) - jnp.log1p(jnp.exp(-jnp.abs(x)))


def _prefix_max_lanes(x):
    lane = lax.broadcasted_iota(jnp.int32, x.shape, 1)
    sh = 1
    while sh < x.shape[1]:
        x = jnp.maximum(x, jnp.where(lane >= sh, pltpu.roll(x, sh, axis=1), -jnp.inf))
        sh *= 2
    return x


def _mlstm_kernel(qk_ref, v_ref, og_ref, gt_ref, cw_ref, cb_ref, gbc_ref, mlg_ref, tri_ref,
                  out_ref, xbuf, ybuf, ct_s, nb_s, m_s):
    step = pl.program_id(1)
    E = LANES
    H = ML_HEADS
    n_slab, rows = qk_ref.shape[0], qk_ref.shape[1]
    half = rows // 2
    n_chunks = rows // CHUNK
    assert n_slab == 2 * H
    assert (n_chunks * 4 * 2 * H) % LANES == 0

    @pl.when(step == 0)
    def _():
        xbuf[:, 0:8, :] = jnp.zeros((n_slab, 8, LANES), F32)
        ct_s[...] = jnp.zeros(ct_s.shape, F32)
        nb_s[...] = jnp.zeros(nb_s.shape, F32)
        m_s[...] = jnp.zeros(m_s.shape, F32)

    @pl.when(step > 0)
    def _():
        xbuf[:, 0:8, :] = xbuf[:, rows:rows + 8, :]

    xbuf[:, 8:8 + rows, :] = qk_ref[...]
    for c in range(n_slab):
        cl = slice(c * LANES, (c + 1) * LANES)
        for par in range(2):
            acc = cb_ref[:, cl]
            for j in range(CONV_K):
                off = 8 - (CONV_K - 1) + j + par
                acc = acc + xbuf[c, pl.ds(off, half, stride=2), :] * cw_ref[j:j + 1, cl]
            act = _silu(acc)
            if c >= H:
                act = act * (E ** -0.5)
            ybuf[c, pl.ds(par, half, stride=2), :] = act

    tri = tri_ref[...]
    n_rows = 2 * H * n_chunks
    head_row = (lax.broadcasted_iota(jnp.int32, (n_rows, CHUNK), 0) & (2 * H - 1)) < H
    gr = jnp.concatenate([gt_ref[:, i * CHUNK:(i + 1) * CHUNK] + gbc_ref[...]
                          for i in range(n_chunks)], axis=0)
    b_all = sum(lax.dot_general(part, tri, (((1,), (1,)), ((), ())),
                                preferred_element_type=F32)
                for part in _split3(_log_sigmoid(gr)))
    b = jnp.concatenate([pltpu.roll(b_all[i * 2 * H:(i + 1) * 2 * H], H, axis=0)
                         for i in range(n_chunks)], axis=0)
    b = jnp.where(head_row, b, 0.0)
    u = jnp.where(head_row, gr - b, 0.0)
    cm = _prefix_max_lanes(u)
    u_max = jnp.broadcast_to(cm[:, CHUNK - 1:CHUNK], cm.shape)
    g_tot = jnp.broadcast_to(b[:, CHUNK - 1:CHUNK], b.shape)
    m_prev = m_s[...]
    m_prevs = []
    for i in range(n_chunks):
        rs = slice(i * 2 * H, (i + 1) * 2 * H)
        m_prevs.append(m_prev)
        m_prev = g_tot[rs] + jnp.maximum(m_prev, u_max[rs])
    m_s[...] = m_prev
    m_prev = jnp.concatenate(m_prevs, axis=0)
    mm = jnp.maximum(m_prev, u_max)
    sp_all = jnp.exp(m_prev - mm)
    sl_all = jnp.exp(u_max - mm)
    wa_all = jnp.exp(u - u_max)
    m_in = jnp.maximum(m_prev, cm)
    cols = jnp.concatenate([m_in, jnp.exp(m_prev - m_in), jnp.exp(-(b + m_in)),
                            jnp.zeros_like(m_in)], axis=0).T

    ti = lax.broadcasted_iota(jnp.int32, (CHUNK, CHUNK), 0)
    si = lax.broadcasted_iota(jnp.int32, (CHUNK, CHUNK), 1)
    causal = si <= ti
    ones = jnp.ones((CHUNK, E), BF16)

    cts = [ct_s[h] for h in range(H)]
    nbs = [nb_s[h] for h in range(H)]
    for i in range(n_chunks):
        r0 = i * CHUNK
        for h in range(H):
            row = i * 2 * H + h
            col = lambda q_, row=row: cols[:, q_ * n_rows + row:q_ * n_rows + row + 1]
            m_in_c, e_w, e_m = col(0), col(1), col(2)
            q = ybuf[h, r0:r0 + CHUNK, :]
            k = ybuf[H + h, r0:r0 + CHUNK, :]
            v1 = jnp.concatenate([v_ref[r0:r0 + CHUNK, h * E:(h + 1) * E], ones], axis=1)
            qb = q.astype(BF16)

            d_w = jnp.where(causal, jnp.exp(u[row:row + 1, :] - m_in_c), 0.0)
            s_qk = lax.dot_general(qb, k.astype(BF16), (((1,), (1,)), ((), ())),
                                   preferred_element_type=F32) * d_w
            st = jnp.concatenate([cts[h].astype(BF16), nbs[h].astype(BF16)], axis=1)
            inter = jnp.dot(qb, st, preferred_element_type=F32)
            intra = jnp.dot(s_qk.astype(BF16), v1, preferred_element_type=F32)
            num = e_w * inter[:, :E] + intra[:, :E]
            den = e_w * inter[:, E:] + intra[:, E:]
            hh = num / jnp.maximum(jnp.abs(den), e_m)

            hg = og_ref[r0:r0 + CHUNK, h * E:(h + 1) * E] * hh
            mu = jnp.mean(hg, -1, keepdims=True)
            ex2 = jnp.mean(hg * hg, -1, keepdims=True)
            hc = hg - mu
            var = jnp.maximum(ex2 - mu * mu, 0.0)
            yn = hc * lax.rsqrt(var + LN_EPS) * mlg_ref[:, h * E:(h + 1) * E]
            out_ref[r0:r0 + CHUNK, h * E:(h + 1) * E] = yn.astype(out_ref.dtype)

            kw = (k.T * wa_all[row:row + 1, :]).astype(BF16)
            loc = jnp.dot(kw, v1, preferred_element_type=F32)
            sp_h, sl_h = sp_all[row:row + 1, :], sl_all[row:row + 1, :]
            cts[h] = sp_h * cts[h] + sl_h * loc[:, :E]
            nbs[h] = sp_h * nbs[h] + sl_h * loc[:, E:]
    for h in range(H):
        ct_s[h] = cts[h]
        nb_s[h] = nbs[h]


def _mlstm(qk, vm, og, gatest, conv_w, conv_b, gb_col, ml_g, B, S):
    rows = ML_CHUNKS_PER_STEP * CHUNK
    ns = S // rows
    W = vm.shape[1]
    G = gatest.shape[0]
    tri = jnp.tril(jnp.ones((CHUNK, CHUNK), F32)).astype(BF16)
    row = lambda w: pl.BlockSpec((rows, w), lambda b, c: (b * ns + c, 0))
    consts = (conv_w, conv_b, gb_col, ml_g, tri)
    return pl.pallas_call(
        _mlstm_kernel,
        grid=(B, ns),
        in_specs=[pl.BlockSpec((qk.shape[0], rows, LANES), lambda b, c: (0, b * ns + c, 0)),
                  row(W), row(W),
                  pl.BlockSpec((G, rows), lambda b, c: (0, b * ns + c))]
                 + [_const_spec(a.shape) for a in consts],
        out_specs=row(W),
        out_shape=jax.ShapeDtypeStruct((B * S, W), BF16),
        scratch_shapes=[pltpu.VMEM((qk.shape[0], rows + 8, LANES), F32),
                        pltpu.VMEM((qk.shape[0], rows, LANES), F32),
                        pltpu.VMEM((ML_HEADS, LANES, LANES), F32),
                        pltpu.VMEM((ML_HEADS, LANES, LANES), F32),
                        pltpu.VMEM((2 * ML_HEADS, LANES), F32)],
        compiler_params=pltpu.CompilerParams(
            dimension_semantics=("parallel", "arbitrary"), vmem_limit_bytes=VMEM_LIMIT),
        name="mlstm",
    )(qk, vm, og, gatest, *consts)


def _tail_kernel(x1_ref, att_ref, hm_ref, kt_ref, v_ref, wout_ref, wq_ref, wo_ref,
                 wg_ref, wu_ref, wd_ref, g_ref, b_ref, out_ref):
    sub = x1_ref.shape[0] // ROW_SUBTILES
    for t in range(ROW_SUBTILES):
        rs = slice(t * sub, (t + 1) * sub)
        n_a = att_ref.shape[1]
        mix = (jnp.dot(att_ref[rs, :], wout_ref[0:n_a, :], preferred_element_type=F32)
               + jnp.dot(hm_ref[rs, :], wout_ref[n_a:, :], preferred_element_type=F32))
        x2 = _layer_norm(ALPHA * x1_ref[rs, :] + mix, g_ref[0:1, :], b_ref[0:1, :])

        q = jnp.dot(x2.astype(BF16), wq_ref[...], preferred_element_type=F32)
        hd = q.shape[1] // XA_HEADS
        heads = []
        for h in range(XA_HEADS):
            qh = q[:, h * hd:(h + 1) * hd].astype(BF16)
            lg = jnp.dot(qh, kt_ref[h * hd:(h + 1) * hd, :], preferred_element_type=F32)
            p = jnp.exp(lg - jnp.max(lg, -1, keepdims=True))
            s = jnp.sum(p, -1, keepdims=True)
            o = jnp.dot(p.astype(BF16), v_ref[:, h * hd:(h + 1) * hd],
                        preferred_element_type=F32)
            heads.append((o / s).astype(BF16))
        xa = jnp.dot(jnp.concatenate(heads, axis=1), wo_ref[...], preferred_element_type=F32)
        x3 = _layer_norm(ALPHA * x2 + xa, g_ref[1:2, :], b_ref[1:2, :])

        y = ALPHA * x3 + 0.5 * _swiglu(x3.astype(BF16), wg_ref, wu_ref, wd_ref)
        out_ref[rs, :] = _layer_norm(y, g_ref[2:3, :], b_ref[2:3, :])


def _tail(x1, att, hm, kt, v, wout, wq, wo, wg, wu, wd, ffn_index, g, b, S):
    T, D = x1.shape
    tm = ROW_TILE
    per_batch = S // tm
    row = lambda w: pl.BlockSpec((tm, w), lambda i: (i, 0))
    L = v.shape[1]
    weights = (wout, wq, wo, wg, wu, wd, g, b)
    return pl.pallas_call(
        _tail_kernel,
        grid=(T // tm,),
        in_specs=[row(D), row(att.shape[1]), row(hm.shape[1]),
                  pl.BlockSpec((None, D, L), lambda i: (i // per_batch, 0, 0)),
                  pl.BlockSpec((None, L, D), lambda i: (i // per_batch, 0, 0))]
                 + _weight_specs(weights, ffn_index),
        out_specs=row(D),
        out_shape=jax.ShapeDtypeStruct((T, D), F32),
        compiler_params=pltpu.CompilerParams(
            dimension_semantics=("parallel",), vmem_limit_bytes=VMEM_LIMIT),
        name="tail",
    )(x1, att, hm, kt, v, *weights)


def kernel(x, mem, rel_bias, ln_g, ln_b, ffn_w_gate, ffn_w_up, ffn_w_down, w_in, conv_w, conv_b,
           ig_bias, fg_bias, ml_norm_g, w_out, xq_w, xkv_w, xo_w):
    B, S, D = x.shape
    att_w = ATT_HEADS * ATT_HD
    ml_w = ML_HEADS * LANES
    bias = _bias_tables(rel_bias)
    xf = x.reshape(B * S, D)
    bf = lambda a: a.astype(BF16)
    wg_all, wu_all, wd_all = bf(ffn_w_gate), bf(ffn_w_up), bf(ffn_w_down)
    for l in range(DEPTH):
        n_main = 3 * att_w + 4 * ml_w
        win = bf(w_in[l][:, :n_main])
        wgtt = bf(lax.optimization_barrier(w_in[l][:, n_main:]).T)
        gb = jnp.concatenate([ig_bias[l], fg_bias[l]]).astype(F32)

        x1, qkv, qk, vm, og, gatest = _ffn_in(
            xf, wg_all, wu_all, wd_all, (l, 0),
            ln_g[l, 0][None], ln_b[l, 0][None], win, wgtt, 3 * att_w, 2 * ml_w, ml_w)
        att = _attention(qkv, bias, B, S).reshape(B * S, att_w)
        hm = _mlstm(qk, vm, og, gatest, conv_w[l], conv_b[l][None], gb[:, None],
                    ml_norm_g[l][None], B, S)

        hd = D // XA_HEADS
        kt, v = _memkv(mem, bf(xkv_w[l]))
        xf = _tail(x1, att, hm, kt, v, bf(w_out[l]), bf(xq_w[l] * hd ** -0.5), bf(xo_w[l]),
                   wg_all, wu_all, wd_all, (l, 1), ln_g[l, 1:4], ln_b[l, 1:4], S)
    return xf.reshape(B, S, D)
```

```python
import functools
import math

import jax
import jax.numpy as jnp
from jax import lax
from jax.experimental import pallas as pl
from jax.experimental.pallas import tpu as pltpu

F32 = jnp.float32
BF16 = jnp.bfloat16

ATT_HD = 64
ATT_HEADS = 8
DILATED = ((128, 1), (512, 4), (2048, 16))
BLK = 128
ML_HEADS = 4
CHUNK = 128
CONV_K = 4
XA_HEADS = 4
REL_BUCKETS = 32
REL_MAX_DIST = 2048
DEPTH = 1
ALPHA = (2 * DEPTH) ** 0.25
LN_EPS = 1e-5
NEG = -1e30
LOG2E = math.log2(math.e)

LANES = 128
VMEM_LIMIT = 56 * 1024 * 1024
ROW_TILE = 512
ROW_SUBTILES = 1
ATT_ORDER = (2, 1, 0)
ATT_UNROLL = 8
ML_CHUNKS_PER_STEP = 4


def _const_spec(shape, lead=()):
    block = (None,) * len(lead) + tuple(shape[len(lead):])
    index = tuple(lead) + (0,) * (len(shape) - len(lead))
    return pl.BlockSpec(block, lambda *_: index, pipeline_mode=pl.Buffered(1))


def _weight_specs(weights, ffn_index):
    return [_const_spec(w.shape, ffn_index if w.ndim == 4 else ()) for w in weights]


def _layer_norm(y, g, b):
    mu = jnp.mean(y, -1, keepdims=True)
    yc = y - mu
    var = jnp.mean(yc * yc, -1, keepdims=True)
    return yc * lax.rsqrt(var + LN_EPS) * g + b


def _silu(x):
    return x * jax.nn.sigmoid(x)


def _swiglu(xb, wg_ref, wu_ref, wd_ref):
    g = jnp.dot(xb, wg_ref[...], preferred_element_type=F32)
    u = jnp.dot(xb, wu_ref[...], preferred_element_type=F32)
    h = (_silu(g) * u).astype(BF16)
    return jnp.dot(h, wd_ref[...], preferred_element_type=F32)


def _bias_kernel(rel_ref, bkt_ref, out_ref):
    bkt = bkt_ref[0]
    prev_half = lax.broadcasted_iota(jnp.int32, bkt.shape, 1) < BLK
    accs = [jnp.full(bkt.shape, NEG, F32) for _ in range(ATT_HEADS)]
    for b in range(REL_BUCKETS):
        hit = bkt == b
        for h in range(ATT_HEADS):
            accs[h] = jnp.where(hit, rel_ref[b, h] * LOG2E, accs[h])
    for h in range(ATT_HEADS):
        out_ref[0, 0, h] = accs[h]
        out_ref[1, 0, h] = jnp.where(prev_half, NEG, accs[h])


def _bias_tables(rel_bias):
    qi = jnp.arange(BLK)[:, None]
    ki = jnp.arange(2 * BLK)[None, :]
    off = qi + BLK - ki
    exact = REL_BUCKETS // 2
    n_log = REL_BUCKETS - exact
    starts = [math.ceil(exact * (REL_MAX_DIST / exact) ** (k / n_log)) for k in range(1, n_log)]
    tabs = []
    for window, dil in DILATED:
        n_keys = window // dil
        dist = dil * jnp.clip(off, 0, n_keys)
        large = exact + sum((dist >= s).astype(jnp.int32) for s in starts)
        bucket = jnp.where(dist < exact, dist, large)
        band = (off >= 0) & (off <= n_keys)
        tabs.append(jnp.where(band, bucket, -1))
    bkt = jnp.stack(tabs, 0).astype(jnp.int32)
    nd = len(DILATED)
    return pl.pallas_call(
        _bias_kernel,
        grid=(nd,),
        in_specs=[pl.BlockSpec(memory_space=pltpu.SMEM),
                  pl.BlockSpec((1, BLK, 2 * BLK), lambda d: (d, 0, 0))],
        out_specs=pl.BlockSpec((2, 1, ATT_HEADS, BLK, 2 * BLK), lambda d: (0, d, 0, 0, 0)),
        out_shape=jax.ShapeDtypeStruct((2, nd, ATT_HEADS, BLK, 2 * BLK), F32),
        name="bias",
    )(rel_bias.astype(F32), bkt)


def _memkv_kernel(mem_ref, wkv_ref, kt_ref, v_ref):
    D = mem_ref.shape[1]
    mb = mem_ref[...].astype(BF16)
    k = jnp.dot(mb, wkv_ref[:, 0:D], preferred_element_type=F32)
    kt_ref[...] = k.T.astype(BF16)
    v_ref[...] = jnp.dot(mb, wkv_ref[:, D:], preferred_element_type=F32).astype(BF16)


def _memkv(mem, wkv):
    B, L, D = mem.shape
    return pl.pallas_call(
        _memkv_kernel,
        grid=(B,),
        in_specs=[pl.BlockSpec((None, L, D), lambda b: (b, 0, 0)),
                  _const_spec((D, 2 * D))],
        out_specs=[pl.BlockSpec((None, D, L), lambda b: (b, 0, 0)),
                   pl.BlockSpec((None, L, D), lambda b: (b, 0, 0))],
        out_shape=[jax.ShapeDtypeStruct((B, D, L), BF16),
                   jax.ShapeDtypeStruct((B, L, D), BF16)],
        compiler_params=pltpu.CompilerParams(vmem_limit_bytes=VMEM_LIMIT),
        name="memkv",
    )(mem, wkv)


def _ffn_in_kernel(x_ref, wg_ref, wu_ref, wd_ref, g_ref, b_ref, wint_ref,
                   x1_ref, qkv_ref, qk_ref, vm_ref, og_ref, gatest_ref):
    n_q = qkv_ref.shape[1] // 3
    c1 = qkv_ref.shape[1]
    c2 = c1 + qk_ref.shape[0] * LANES
    c3 = c2 + vm_ref.shape[1]
    c4 = c3 + og_ref.shape[1]
    x = x_ref[...]
    y = ALPHA * x + 0.5 * _swiglu(x.astype(BF16), wg_ref, wu_ref, wd_ref)
    x1 = _layer_norm(y, g_ref[...], b_ref[...])
    x1_ref[...] = x1
    xb = x1.astype(BF16)
    nt = (((1,), (1,)), ((), ()))
    proj = lambda lo, hi: lax.dot_general(xb, wint_ref[lo:hi, :], nt, preferred_element_type=F32)

    qkm = proj(c1, c2)
    for c in range(qk_ref.shape[0]):
        qk_ref[c] = qkm[:, c * LANES:(c + 1) * LANES]

    qkv = proj(0, c1)
    qkv_ref[:, 0:n_q] = qkv[:, 0:n_q] * (ATT_HD ** -0.5 * LOG2E)
    qkv_ref[:, n_q:c1] = qkv[:, n_q:c1]
    vm_ref[...] = proj(c2, c3).astype(BF16)
    og_ref[...] = jax.nn.sigmoid(proj(c3, c4))
    gatest_ref[...] = lax.dot_general(wint_ref[c4:c4 + gatest_ref.shape[0], :], xb, nt,
                                      preferred_element_type=F32)


def _ffn_in(x2d, wg, wu, wd, ffn_index, g, b, wint, n_att, n_qkm, n_ml):
    T, D = x2d.shape
    tm = ROW_TILE
    row = lambda w: pl.BlockSpec((tm, w), lambda i: (i, 0))
    n_g = wint.shape[0] - n_att - n_qkm - 2 * n_ml
    n_slab = n_qkm // LANES
    weights = (wg, wu, wd, g, b, wint)
    return pl.pallas_call(
        _ffn_in_kernel,
        grid=(T // tm,),
        in_specs=[row(D)] + _weight_specs(weights, ffn_index),
        out_specs=[row(D), row(n_att),
                   pl.BlockSpec((n_slab, tm, LANES), lambda i: (0, i, 0)),
                   row(n_ml), row(n_ml),
                   pl.BlockSpec((n_g, tm), lambda i: (0, i))],
        out_shape=[jax.ShapeDtypeStruct((T, D), F32),
                   jax.ShapeDtypeStruct((T, n_att), F32),
                   jax.ShapeDtypeStruct((n_slab, T, LANES), F32),
                   jax.ShapeDtypeStruct((T, n_ml), BF16),
                   jax.ShapeDtypeStruct((T, n_ml), F32),
                   jax.ShapeDtypeStruct((n_g, T), F32)],
        compiler_params=pltpu.CompilerParams(
            dimension_semantics=("parallel",), vmem_limit_bytes=VMEM_LIMIT),
        name="ffn_in",
    )(x2d, *weights)


def _attn_block(q_ref, k_ref, v_ref, bias_ref, out_ref, acc_s, m_s, l_s, j, step):
    di = ATT_ORDER[step]
    dil = DILATED[di][1]
    lane = lax.broadcasted_iota(jnp.int32, (BLK, LANES), 1)
    head0 = lane < ATT_HD
    r = j % dil
    n = j // dil
    span = BLK * dil

    def rows_at(nn):
        start = r + nn * span
        if dil == 1:
            return pl.ds(pl.multiple_of(start, BLK), BLK)
        return pl.ds(start, BLK, stride=dil)

    rows = rows_at(n)
    prev = rows_at(jnp.maximum(n - 1, 0))
    q = q_ref[rows, :]
    kk = jnp.concatenate([k_ref[prev, :].astype(BF16), k_ref[rows, :].astype(BF16)], axis=0)
    vv = jnp.concatenate([v_ref[prev, :].astype(BF16), v_ref[rows, :].astype(BF16)], axis=0)
    first = jnp.where(n == 0, 1, 0)
    ms, ls, os_ = [], [], []
    for h in range(2):
        sel = head0 if h == 0 else jnp.logical_not(head0)
        qh = jnp.where(sel, q, 0.0).astype(BF16)
        lg = lax.dot_general(qh, kk, (((1,), (1,)), ((), ())), preferred_element_type=F32)
        lg = lg + bias_ref[first, di, h]
        mh = jnp.max(lg, -1, keepdims=True)
        p = jnp.exp2(lg - mh)
        ls.append(jnp.sum(p, -1, keepdims=True))
        ms.append(mh)
        os_.append(jnp.dot(p.astype(BF16), vv, preferred_element_type=F32))
    m_b = jnp.where(head0, ms[0], ms[1])
    l_b = jnp.where(head0, ls[0], ls[1])
    o_b = jnp.where(head0, os_[0], os_[1])
    if step > 0:
        m_o = m_s[rows, :]
        m_n = jnp.maximum(m_o, m_b)
        e_o = jnp.exp2(m_o - m_n)
        e_b = jnp.exp2(m_b - m_n)
        l_b = l_s[rows, :] * e_o + l_b * e_b
        o_b = acc_s[rows, :] * e_o + o_b * e_b
        m_b = m_n
    if step < len(ATT_ORDER) - 1:
        m_s[rows, :] = m_b
        l_s[rows, :] = l_b
        acc_s[rows, :] = o_b
    else:
        out_ref[rows, :] = (o_b / l_b).astype(out_ref.dtype)


def _attn_kernel(q_ref, k_ref, v_ref, bias_ref, out_ref, acc_s, m_s, l_s):
    n_blocks = q_ref.shape[0] // BLK
    for step in range(len(ATT_ORDER)):
        def body(i, _, step=step):
            for u in range(ATT_UNROLL):
                _attn_block(q_ref, k_ref, v_ref, bias_ref, out_ref, acc_s, m_s, l_s,
                            i * ATT_UNROLL + u, step)
            return 0
        lax.fori_loop(0, n_blocks // ATT_UNROLL, body, 0)


def _attention(qkv, bias, B, S):
    n_pairs = ATT_HEADS * ATT_HD // LANES
    qkv3 = qkv.reshape(B, S, qkv.shape[-1])
    col = lambda off: pl.BlockSpec((None, S, LANES), lambda b, p: (b, 0, off + p))
    assert DILATED[ATT_ORDER[-1]][1] == 1
    return pl.pallas_call(
        _attn_kernel,
        grid=(B, n_pairs),
        in_specs=[col(0), col(n_pairs), col(2 * n_pairs),
                  pl.BlockSpec((2, len(DILATED), 2, BLK, 2 * BLK),
                               lambda b, p: (0, 0, p, 0, 0))],
        out_specs=pl.BlockSpec((None, S, LANES), lambda b, p: (b, 0, p)),
        out_shape=jax.ShapeDtypeStruct((B, S, n_pairs * LANES), BF16),
        scratch_shapes=[pltpu.VMEM((S, LANES), F32)] * 3,
        compiler_params=pltpu.CompilerParams(
            dimension_semantics=("parallel", "parallel"), vmem_limit_bytes=VMEM_LIMIT),
        name="attn",
    )(qkv3, qkv3, qkv3, bias)


def _split3(a):
    hi = a.astype(BF16)
    r1 = a - hi.astype(F32)
    mid = r1.astype(BF16)
    lo = (r1 - mid.astype(F32)).astype(BF16)
    return hi, mid, lo


def _log_sigmoid(x):
    return jnp.minimum(x, 0.0) - jnp.log1p(jnp.exp(-jnp.abs(x)))


def _prefix_max_lanes(x):
    lane = lax.broadcasted_iota(jnp.int32, x.shape, 1)
    sh = 1
    while sh < x.shape[1]:
        x = jnp.maximum(x, jnp.where(lane >= sh, pltpu.roll(x, sh, axis=1), -jnp.inf))
        sh *= 2
    return x


def _mlstm_kernel(qk_ref, v_ref, og_ref, gt_ref, cw_ref, cb_ref, gbc_ref, mlg_ref, tri_ref,
                  out_ref, xbuf, ybuf, ct_s, nb_s, m_s):
    step = pl.program_id(1)
    E = LANES
    H = ML_HEADS
    n_slab, rows = qk_ref.shape[0], qk_ref.shape[1]
    half = rows // 2
    n_chunks = rows // CHUNK
    assert n_slab == 2 * H
    assert (n_chunks * 4 * 2 * H) % LANES == 0

    @pl.when(step == 0)
    def _():
        xbuf[:, 0:8, :] = jnp.zeros((n_slab, 8, LANES), F32)
        ct_s[...] = jnp.zeros(ct_s.shape, F32)
        nb_s[...] = jnp.zeros(nb_s.shape, F32)
        m_s[...] = jnp.zeros(m_s.shape, F32)

    @pl.when(step > 0)
    def _():
        xbuf[:, 0:8, :] = xbuf[:, rows:rows + 8, :]

    xbuf[:, 8:8 + rows, :] = qk_ref[...]
    for c in range(n_slab):
        cl = slice(c * LANES, (c + 1) * LANES)
        for par in range(2):
            acc = cb_ref[:, cl]
            for j in range(CONV_K):
                off = 8 - (CONV_K - 1) + j + par
                acc = acc + xbuf[c, pl.ds(off, half, stride=2), :] * cw_ref[j:j + 1, cl]
            act = _silu(acc)
            if c >= H:
                act = act * (E ** -0.5)
            ybuf[c, pl.ds(par, half, stride=2), :] = act

    tri = tri_ref[...]
    n_rows = 2 * H * n_chunks
    head_row = (lax.broadcasted_iota(jnp.int32, (n_rows, CHUNK), 0) & (2 * H - 1)) < H
    gr = jnp.concatenate([gt_ref[:, i * CHUNK:(i + 1) * CHUNK] + gbc_ref[...]
                          for i in range(n_chunks)], axis=0)
    b_all = sum(lax.dot_general(part, tri, (((1,), (1,)), ((), ())),
                                preferred_element_type=F32)
                for part in _split3(_log_sigmoid(gr)))
    b = jnp.concatenate([pltpu.roll(b_all[i * 2 * H:(i + 1) * 2 * H], H, axis=0)
                         for i in range(n_chunks)], axis=0)
    b = jnp.where(head_row, b, 0.0)
    u = jnp.where(head_row, gr - b, 0.0)
    cm = _prefix_max_lanes(u)
    u_max = jnp.broadcast_to(cm[:, CHUNK - 1:CHUNK], cm.shape)
    g_tot = jnp.broadcast_to(b[:, CHUNK - 1:CHUNK], b.shape)
    m_prev = m_s[...]
    m_prevs = []
    for i in range(n_chunks):
        rs = slice(i * 2 * H, (i + 1) * 2 * H)
        m_prevs.append(m_prev)
        m_prev = g_tot[rs] + jnp.maximum(m_prev, u_max[rs])
    m_s[...] = m_prev
    m_prev = jnp.concatenate(m_prevs, axis=0)
    mm = jnp.maximum(m_prev, u_max)
    sp_all = jnp.exp(m_prev - mm)
    sl_all = jnp.exp(u_max - mm)
    wa_all = jnp.exp(u - u_max)
    m_in = jnp.maximum(m_prev, cm)
    cols = jnp.concatenate([m_in, jnp.exp(m_prev - m_in), jnp.exp(-(b + m_in)),
                            jnp.zeros_like(m_in)], axis=0).T

    ti = lax.broadcasted_iota(jnp.int32, (CHUNK, CHUNK), 0)
    si = lax.broadcasted_iota(jnp.int32, (CHUNK, CHUNK), 1)
    causal = si <= ti
    ones = jnp.ones((CHUNK, E), BF16)

    cts = [ct_s[h] for h in range(H)]
    nbs = [nb_s[h] for h in range(H)]
    for i in range(n_chunks):
        r0 = i * CHUNK
        for h in range(H):
            row = i * 2 * H + h
            col = lambda q_, row=row: cols[:, q_ * n_rows + row:q_ * n_rows + row + 1]
            m_in_c, e_w, e_m = col(0), col(1), col(2)
            q = ybuf[h, r0:r0 + CHUNK, :]
            k = ybuf[H + h, r0:r0 + CHUNK, :]
            v1 = jnp.concatenate([v_ref[r0:r0 + CHUNK, h * E:(h + 1) * E], ones], axis=1)
            qb = q.astype(BF16)

            d_w = jnp.where(causal, jnp.exp(u[row:row + 1, :] - m_in_c), 0.0)
            s_qk = lax.dot_general(qb, k.astype(BF16), (((1,), (1,)), ((), ())),
                                   preferred_element_type=F32) * d_w
            st = jnp.concatenate([cts[h].astype(BF16), nbs[h].astype(BF16)], axis=1)
            inter = jnp.dot(qb, st, preferred_element_type=F32)
            intra = jnp.dot(s_qk.astype(BF16), v1, preferred_element_type=F32)
            num = e_w * inter[:, :E] + intra[:, :E]
            den = e_w * inter[:, E:] + intra[:, E:]
            hh = num / jnp.maximum(jnp.abs(den), e_m)

            hg = og_ref[r0:r0 + CHUNK, h * E:(h + 1) * E] * hh
            mu = jnp.mean(hg, -1, keepdims=True)
            ex2 = jnp.mean(hg * hg, -1, keepdims=True)
            hc = hg - mu
            var = jnp.maximum(ex2 - mu * mu, 0.0)
            yn = hc * lax.rsqrt(var + LN_EPS) * mlg_ref[:, h * E:(h + 1) * E]
            out_ref[r0:r0 + CHUNK, h * E:(h + 1) * E] = yn.astype(out_ref.dtype)

            kw = (k.T * wa_all[row:row + 1, :]).astype(BF16)
            loc = jnp.dot(kw, v1, preferred_element_type=F32)
            sp_h, sl_h = sp_all[row:row + 1, :], sl_all[row:row + 1, :]
            cts[h] = sp_h * cts[h] + sl_h * loc[:, :E]
            nbs[h] = sp_h * nbs[h] + sl_h * loc[:, E:]
    for h in range(H):
        ct_s[h] = cts[h]
        nb_s[h] = nbs[h]


def _mlstm(qk, vm, og, gatest, conv_w, conv_b, gb_col, ml_g, B, S):
    rows = ML_CHUNKS_PER_STEP * CHUNK
    ns = S // rows
    W = vm.shape[1]
    G = gatest.shape[0]
    tri = jnp.tril(jnp.ones((CHUNK, CHUNK), F32)).astype(BF16)
    row = lambda w: pl.BlockSpec((rows, w), lambda b, c: (b * ns + c, 0))
    consts = (conv_w, conv_b, gb_col, ml_g, tri)
    return pl.pallas_call(
        _mlstm_kernel,
        grid=(B, ns),
        in_specs=[pl.BlockSpec((qk.shape[0], rows, LANES), lambda b, c: (0, b * ns + c, 0)),
                  row(W), row(W),
                  pl.BlockSpec((G, rows), lambda b, c: (0, b * ns + c))]
                 + [_const_spec(a.shape) for a in consts],
        out_specs=row(W),
        out_shape=jax.ShapeDtypeStruct((B * S, W), BF16),
        scratch_shapes=[pltpu.VMEM((qk.shape[0], rows + 8, LANES), F32),
                        pltpu.VMEM((qk.shape[0], rows, LANES), F32),
                        pltpu.VMEM((ML_HEADS, LANES, LANES), F32),
                        pltpu.VMEM((ML_HEADS, LANES, LANES), F32),
                        pltpu.VMEM((2 * ML_HEADS, LANES), F32)],
        compiler_params=pltpu.CompilerParams(
            dimension_semantics=("parallel", "arbitrary"), vmem_limit_bytes=VMEM_LIMIT),
        name="mlstm",
    )(qk, vm, og, gatest, *consts)


def _tail_kernel(x1_ref, att_ref, hm_ref, kt_ref, v_ref, wout_ref, wq_ref, wo_ref,
                 wg_ref, wu_ref, wd_ref, g_ref, b_ref, out_ref):
    sub = x1_ref.shape[0] // ROW_SUBTILES
    for t in range(ROW_SUBTILES):
        rs = slice(t * sub, (t + 1) * sub)
        n_a = att_ref.shape[1]
        mix = (jnp.dot(att_ref[rs, :], wout_ref[0:n_a, :], preferred_element_type=F32)
               + jnp.dot(hm_ref[rs, :], wout_ref[n_a:, :], preferred_element_type=F32))
        x2 = _layer_norm(ALPHA * x1_ref[rs, :] + mix, g_ref[0:1, :], b_ref[0:1, :])

        q = jnp.dot(x2.astype(BF16), wq_ref[...], preferred_element_type=F32)
        hd = q.shape[1] // XA_HEADS
        heads = []
        for h in range(XA_HEADS):
            qh = q[:, h * hd:(h + 1) * hd].astype(BF16)
            lg = jnp.dot(qh, kt_ref[h * hd:(h + 1) * hd, :], preferred_element_type=F32)
            p = jnp.exp(lg - jnp.max(lg, -1, keepdims=True))
            s = jnp.sum(p, -1, keepdims=True)
            o = jnp.dot(p.astype(BF16), v_ref[:, h * hd:(h + 1) * hd],
                        preferred_element_type=F32)
            heads.append((o / s).astype(BF16))
        xa = jnp.dot(jnp.concatenate(heads, axis=1), wo_ref[...], preferred_element_type=F32)
        x3 = _layer_norm(ALPHA * x2 + xa, g_ref[1:2, :], b_ref[1:2, :])

        y = ALPHA * x3 + 0.5 * _swiglu(x3.astype(BF16), wg_ref, wu_ref, wd_ref)
        out_ref[rs, :] = _layer_norm(y, g_ref[2:3, :], b_ref[2:3, :])


def _tail(x1, att, hm, kt, v, wout, wq, wo, wg, wu, wd, ffn_index, g, b, S):
    T, D = x1.shape
    tm = ROW_TILE
    per_batch = S // tm
    row = lambda w: pl.BlockSpec((tm, w), lambda i: (i, 0))
    L = v.shape[1]
    weights = (wout, wq, wo, wg, wu, wd, g, b)
    return pl.pallas_call(
        _tail_kernel,
        grid=(T // tm,),
        in_specs=[row(D), row(att.shape[1]), row(hm.shape[1]),
                  pl.BlockSpec((None, D, L), lambda i: (i // per_batch, 0, 0)),
                  pl.BlockSpec((None, L, D), lambda i: (i // per_batch, 0, 0))]
                 + _weight_specs(weights, ffn_index),
        out_specs=row(D),
        out_shape=jax.ShapeDtypeStruct((T, D), F32),
        compiler_params=pltpu.CompilerParams(
            dimension_semantics=("parallel",), vmem_limit_bytes=VMEM_LIMIT),
        name="tail",
    )(x1, att, hm, kt, v, *weights)


def kernel(x, mem, rel_bias, ln_g, ln_b, ffn_w_gate, ffn_w_up, ffn_w_down, w_in, conv_w, conv_b,
           ig_bias, fg_bias, ml_norm_g, w_out, xq_w, xkv_w, xo_w):
    B, S, D = x.shape
    att_w = ATT_HEADS * ATT_HD
    ml_w = ML_HEADS * LANES
    bias = _bias_tables(rel_bias)
    xf = x.reshape(B * S, D)
    bf = lambda a: a.astype(BF16)
    wg_all, wu_all, wd_all = bf(ffn_w_gate), bf(ffn_w_up), bf(ffn_w_down)
    for l in range(DEPTH):
        wint = bf(w_in[l].T)
        gb = jnp.concatenate([ig_bias[l], fg_bias[l]]).astype(F32)

        x1, qkv, qk, vm, og, gatest = _ffn_in(
            xf, wg_all, wu_all, wd_all, (l, 0),
            ln_g[l, 0][None], ln_b[l, 0][None], wint, 3 * att_w, 2 * ml_w, ml_w)
        att = _attention(qkv, bias, B, S).reshape(B * S, att_w)
        hm = _mlstm(qk, vm, og, gatest, conv_w[l], conv_b[l][None], gb[:, None],
                    ml_norm_g[l][None], B, S)

        hd = D // XA_HEADS
        kt, v = _memkv(mem, bf(xkv_w[l]))
        xf = _tail(x1, att, hm, kt, v, bf(w_out[l]), bf(xq_w[l] * hd ** -0.5), bf(xo_w[l]),
                   wg_all, wu_all, wd_all, (l, 1), ln_g[l, 1:4], ln_b[l, 1:4], S)
    return xf.reshape(B, S, D)
```

```python
import functools
import math

import jax
import jax.numpy as jnp
from jax import lax
from jax.experimental import pallas as pl
from jax.experimental.pallas import tpu as pltpu

F32 = jnp.float32
BF16 = jnp.bfloat16

ATT_HD = 64
ATT_HEADS = 8
DILATED = ((128, 1), (512, 4), (2048, 16))
BLK = 128
ML_HEADS = 4
CHUNK = 128
CONV_K = 4
XA_HEADS = 4
REL_BUCKETS = 32
REL_MAX_DIST = 2048
DEPTH = 1
ALPHA = (2 * DEPTH) ** 0.25
LN_EPS = 1e-5
NEG = -1e30
LOG2E = math.log2(math.e)

LANES = 128
VMEM_LIMIT = 56 * 1024 * 1024
ROW_TILE = 512
ROW_SUBTILES = 2
ATT_ORDER = (2, 1, 0)
ATT_UNROLL = 8
ML_CHUNKS_PER_STEP = 4


def _const_spec(shape, lead=()):
    block = (None,) * len(lead) + tuple(shape[len(lead):])
    index = tuple(lead) + (0,) * (len(shape) - len(lead))
    return pl.BlockSpec(block, lambda *_: index, pipeline_mode=pl.Buffered(1))


def _weight_specs(weights, ffn_index):
    return [_const_spec(w.shape, ffn_index if w.ndim == 4 else ()) for w in weights]


def _layer_norm(y, g, b):
    mu = jnp.mean(y, -1, keepdims=True)
    yc = y - mu
    var = jnp.mean(yc * yc, -1, keepdims=True)
    return yc * lax.rsqrt(var + LN_EPS) * g + b


def _silu(x):
    return x * jax.nn.sigmoid(x)


def _swiglu(xb, wg_ref, wu_ref, wd_ref):
    g = jnp.dot(xb, wg_ref[...], preferred_element_type=F32)
    u = jnp.dot(xb, wu_ref[...], preferred_element_type=F32)
    h = (_silu(g) * u).astype(BF16)
    return jnp.dot(h, wd_ref[...], preferred_element_type=F32)


def _bias_kernel(rel_ref, bkt_ref, out_ref):
    bkt = bkt_ref[0]
    prev_half = lax.broadcasted_iota(jnp.int32, bkt.shape, 1) < BLK
    accs = [jnp.full(bkt.shape, NEG, F32) for _ in range(ATT_HEADS)]
    for b in range(REL_BUCKETS):
        hit = bkt == b
        for h in range(ATT_HEADS):
            accs[h] = jnp.where(hit, rel_ref[b, h] * LOG2E, accs[h])
    for h in range(ATT_HEADS):
        out_ref[0, 0, h] = accs[h]
        out_ref[1, 0, h] = jnp.where(prev_half, NEG, accs[h])


def _bias_tables(rel_bias):
    qi = jnp.arange(BLK)[:, None]
    ki = jnp.arange(2 * BLK)[None, :]
    off = qi + BLK - ki
    exact = REL_BUCKETS // 2
    n_log = REL_BUCKETS - exact
    starts = [math.ceil(exact * (REL_MAX_DIST / exact) ** (k / n_log)) for k in range(1, n_log)]
    tabs = []
    for window, dil in DILATED:
        n_keys = window // dil
        dist = dil * jnp.clip(off, 0, n_keys)
        large = exact + sum((dist >= s).astype(jnp.int32) for s in starts)
        bucket = jnp.where(dist < exact, dist, large)
        band = (off >= 0) & (off <= n_keys)
        tabs.append(jnp.where(band, bucket, -1))
    bkt = jnp.stack(tabs, 0).astype(jnp.int32)
    nd = len(DILATED)
    return pl.pallas_call(
        _bias_kernel,
        grid=(nd,),
        in_specs=[pl.BlockSpec(memory_space=pltpu.SMEM),
                  pl.BlockSpec((1, BLK, 2 * BLK), lambda d: (d, 0, 0))],
        out_specs=pl.BlockSpec((2, 1, ATT_HEADS, BLK, 2 * BLK), lambda d: (0, d, 0, 0, 0)),
        out_shape=jax.ShapeDtypeStruct((2, nd, ATT_HEADS, BLK, 2 * BLK), F32),
        name="bias",
    )(rel_bias.astype(F32), bkt)


def _memkv_kernel(mem_ref, wkv_ref, kt_ref, v_ref):
    D = mem_ref.shape[1]
    mb = mem_ref[...].astype(BF16)
    k = jnp.dot(mb, wkv_ref[:, 0:D], preferred_element_type=F32)
    kt_ref[...] = k.T.astype(BF16)
    v_ref[...] = jnp.dot(mb, wkv_ref[:, D:], preferred_element_type=F32).astype(BF16)


def _memkv(mem, wkv):
    B, L, D = mem.shape
    return pl.pallas_call(
        _memkv_kernel,
        grid=(B,),
        in_specs=[pl.BlockSpec((None, L, D), lambda b: (b, 0, 0)),
                  _const_spec((D, 2 * D))],
        out_specs=[pl.BlockSpec((None, D, L), lambda b: (b, 0, 0)),
                   pl.BlockSpec((None, L, D), lambda b: (b, 0, 0))],
        out_shape=[jax.ShapeDtypeStruct((B, D, L), BF16),
                   jax.ShapeDtypeStruct((B, L, D), BF16)],
        compiler_params=pltpu.CompilerParams(vmem_limit_bytes=VMEM_LIMIT),
        name="memkv",
    )(mem, wkv)


def _ffn_in_kernel(x_ref, wg_ref, wu_ref, wd_ref, g_ref, b_ref, wint_ref,
                   x1_ref, qkv_ref, qk_ref, vm_ref, og_ref, gatest_ref):
    n_q = qkv_ref.shape[1] // 3
    c1 = qkv_ref.shape[1]
    c2 = c1 + qk_ref.shape[0] * LANES
    c3 = c2 + vm_ref.shape[1]
    c4 = c3 + og_ref.shape[1]
    nt = (((1,), (1,)), ((), ()))
    sub = x_ref.shape[0] // ROW_SUBTILES
    tiles = [slice(t * sub, (t + 1) * sub) for t in range(ROW_SUBTILES)]
    ffs = [_swiglu(x_ref[rs, :].astype(BF16), wg_ref, wu_ref, wd_ref) for rs in tiles]
    for rs, ff in zip(tiles, ffs):
        x1 = _layer_norm(ALPHA * x_ref[rs, :] + 0.5 * ff, g_ref[...], b_ref[...])
        x1_ref[rs, :] = x1
        xb = x1.astype(BF16)
        proj = lambda lo, hi, xb=xb: lax.dot_general(xb, wint_ref[lo:hi, :], nt,
                                                     preferred_element_type=F32)

        qkm = proj(c1, c2)
        for c in range(qk_ref.shape[0]):
            qk_ref[c, rs, :] = qkm[:, c * LANES:(c + 1) * LANES]

        qkv = proj(0, c1)
        qkv_ref[rs, 0:n_q] = qkv[:, 0:n_q] * (ATT_HD ** -0.5 * LOG2E)
        qkv_ref[rs, n_q:c1] = qkv[:, n_q:c1]
        vm_ref[rs, :] = proj(c2, c3).astype(BF16)
        og_ref[rs, :] = jax.nn.sigmoid(proj(c3, c4))
        gatest_ref[:, rs] = lax.dot_general(wint_ref[c4:c4 + gatest_ref.shape[0], :], xb, nt,
                                            preferred_element_type=F32)


def _ffn_in(x2d, wg, wu, wd, ffn_index, g, b, wint, n_att, n_qkm, n_ml):
    T, D = x2d.shape
    tm = ROW_TILE
    row = lambda w: pl.BlockSpec((tm, w), lambda i: (i, 0))
    n_g = wint.shape[0] - n_att - n_qkm - 2 * n_ml
    n_slab = n_qkm // LANES
    weights = (wg, wu, wd, g, b, wint)
    return pl.pallas_call(
        _ffn_in_kernel,
        grid=(T // tm,),
        in_specs=[row(D)] + _weight_specs(weights, ffn_index),
        out_specs=[row(D), row(n_att),
                   pl.BlockSpec((n_slab, tm, LANES), lambda i: (0, i, 0)),
                   row(n_ml), row(n_ml),
                   pl.BlockSpec((n_g, tm), lambda i: (0, i))],
        out_shape=[jax.ShapeDtypeStruct((T, D), F32),
                   jax.ShapeDtypeStruct((T, n_att), F32),
                   jax.ShapeDtypeStruct((n_slab, T, LANES), F32),
                   jax.ShapeDtypeStruct((T, n_ml), BF16),
                   jax.ShapeDtypeStruct((T, n_ml), F32),
                   jax.ShapeDtypeStruct((n_g, T), F32)],
        compiler_params=pltpu.CompilerParams(
            dimension_semantics=("parallel",), vmem_limit_bytes=VMEM_LIMIT),
        name="ffn_in",
    )(x2d, *weights)


def _attn_block(q_ref, k_ref, v_ref, bias_ref, out_ref, acc_s, m_s, l_s, j, step):
    di = ATT_ORDER[step]
    dil = DILATED[di][1]
    lane = lax.broadcasted_iota(jnp.int32, (BLK, LANES), 1)
    head0 = lane < ATT_HD
    r = j % dil
    n = j // dil
    span = BLK * dil

    def rows_at(nn):
        start = r + nn * span
        if dil == 1:
            return pl.ds(pl.multiple_of(start, BLK), BLK)
        return pl.ds(start, BLK, stride=dil)

    rows = rows_at(n)
    prev = rows_at(jnp.maximum(n - 1, 0))
    q = q_ref[rows, :]
    kk = jnp.concatenate([k_ref[prev, :].astype(BF16), k_ref[rows, :].astype(BF16)], axis=0)
    vv = jnp.concatenate([v_ref[prev, :].astype(BF16), v_ref[rows, :].astype(BF16)], axis=0)
    first = jnp.where(n == 0, 1, 0)
    ms, ls, os_ = [], [], []
    for h in range(2):
        sel = head0 if h == 0 else jnp.logical_not(head0)
        qh = jnp.where(sel, q, 0.0).astype(BF16)
        lg = lax.dot_general(qh, kk, (((1,), (1,)), ((), ())), preferred_element_type=F32)
        lg = lg + bias_ref[first, di, h]
        mh = jnp.max(lg, -1, keepdims=True)
        p = jnp.exp2(lg - mh)
        ls.append(jnp.sum(p, -1, keepdims=True))
        ms.append(mh)
        os_.append(jnp.dot(p.astype(BF16), vv, preferred_element_type=F32))
    m_b = jnp.where(head0, ms[0], ms[1])
    l_b = jnp.where(head0, ls[0], ls[1])
    o_b = jnp.where(head0, os_[0], os_[1])
    if step > 0:
        m_o = m_s[rows, :]
        m_n = jnp.maximum(m_o, m_b)
        e_o = jnp.exp2(m_o - m_n)
        e_b = jnp.exp2(m_b - m_n)
        l_b = l_s[rows, :] * e_o + l_b * e_b
        o_b = acc_s[rows, :] * e_o + o_b * e_b
        m_b = m_n
    if step < len(ATT_ORDER) - 1:
        m_s[rows, :] = m_b
        l_s[rows, :] = l_b
        acc_s[rows, :] = o_b
    else:
        out_ref[rows, :] = (o_b / l_b).astype(out_ref.dtype)


def _attn_kernel(q_ref, k_ref, v_ref, bias_ref, out_ref, acc_s, m_s, l_s):
    n_blocks = q_ref.shape[0] // BLK
    for step in range(len(ATT_ORDER)):
        def body(i, _, step=step):
            for u in range(ATT_UNROLL):
                _attn_block(q_ref, k_ref, v_ref, bias_ref, out_ref, acc_s, m_s, l_s,
                            i * ATT_UNROLL + u, step)
            return 0
        lax.fori_loop(0, n_blocks // ATT_UNROLL, body, 0)


def _attention(qkv, bias, B, S):
    n_pairs = ATT_HEADS * ATT_HD // LANES
    qkv3 = qkv.reshape(B, S, qkv.shape[-1])
    col = lambda off: pl.BlockSpec((None, S, LANES), lambda b, p: (b, 0, off + p))
    assert DILATED[ATT_ORDER[-1]][1] == 1
    return pl.pallas_call(
        _attn_kernel,
        grid=(B, n_pairs),
        in_specs=[col(0), col(n_pairs), col(2 * n_pairs),
                  pl.BlockSpec((2, len(DILATED), 2, BLK, 2 * BLK),
                               lambda b, p: (0, 0, p, 0, 0))],
        out_specs=pl.BlockSpec((None, S, LANES), lambda b, p: (b, 0, p)),
        out_shape=jax.ShapeDtypeStruct((B, S, n_pairs * LANES), BF16),
        scratch_shapes=[pltpu.VMEM((S, LANES), F32)] * 3,
        compiler_params=pltpu.CompilerParams(
            dimension_semantics=("parallel", "parallel"), vmem_limit_bytes=VMEM_LIMIT),
        name="attn",
    )(qkv3, qkv3, qkv3, bias)


def _split3(a):
    hi = a.astype(BF16)
    r1 = a - hi.astype(F32)
    mid = r1.astype(BF16)
    lo = (r1 - mid.astype(F32)).astype(BF16)
    return hi, mid, lo


def _log_sigmoid(x):
    return jnp.minimum(x, 0.0) - jnp.log1p(jnp.exp(-jnp.abs(x)))


def _prefix_max_lanes(x):
    lane = lax.broadcasted_iota(jnp.int32, x.shape, 1)
    sh = 1
    while sh < x.shape[1]:
        x = jnp.maximum(x, jnp.where(lane >= sh, pltpu.roll(x, sh, axis=1), -jnp.inf))
        sh *= 2
    return x


def _mlstm_kernel(qk_ref, v_ref, og_ref, gt_ref, cw_ref, cb_ref, gbc_ref, mlg_ref, tri_ref,
                  out_ref, xbuf, ybuf, ct_s, nb_s, m_s):
    step = pl.program_id(1)
    E = LANES
    H = ML_HEADS
    n_slab, rows = qk_ref.shape[0], qk_ref.shape[1]
    half = rows // 2
    n_chunks = rows // CHUNK
    assert n_slab == 2 * H
    assert (n_chunks * 4 * 2 * H) % LANES == 0

    @pl.when(step == 0)
    def _():
        xbuf[:, 0:8, :] = jnp.zeros((n_slab, 8, LANES), F32)
        ct_s[...] = jnp.zeros(ct_s.shape, F32)
        nb_s[...] = jnp.zeros(nb_s.shape, F32)
        m_s[...] = jnp.zeros(m_s.shape, F32)

    @pl.when(step > 0)
    def _():
        xbuf[:, 0:8, :] = xbuf[:, rows:rows + 8, :]

    xbuf[:, 8:8 + rows, :] = qk_ref[...]
    for c in range(n_slab):
        cl = slice(c * LANES, (c + 1) * LANES)
        for par in range(2):
            acc = cb_ref[:, cl]
            for j in range(CONV_K):
                off = 8 - (CONV_K - 1) + j + par
                acc = acc + xbuf[c, pl.ds(off, half, stride=2), :] * cw_ref[j:j + 1, cl]
            act = _silu(acc)
            if c >= H:
                act = act * (E ** -0.5)
            ybuf[c, pl.ds(par, half, stride=2), :] = act

    tri = tri_ref[...]
    n_rows = 2 * H * n_chunks
    head_row = (lax.broadcasted_iota(jnp.int32, (n_rows, CHUNK), 0) & (2 * H - 1)) < H
    gr = jnp.concatenate([gt_ref[:, i * CHUNK:(i + 1) * CHUNK] + gbc_ref[...]
                          for i in range(n_chunks)], axis=0)
    b_all = sum(lax.dot_general(part, tri, (((1,), (1,)), ((), ())),
                                preferred_element_type=F32)
                for part in _split3(_log_sigmoid(gr)))
    b = jnp.concatenate([pltpu.roll(b_all[i * 2 * H:(i + 1) * 2 * H], H, axis=0)
                         for i in range(n_chunks)], axis=0)
    b = jnp.where(head_row, b, 0.0)
    u = jnp.where(head_row, gr - b, 0.0)
    cm = _prefix_max_lanes(u)
    u_max = jnp.broadcast_to(cm[:, CHUNK - 1:CHUNK], cm.shape)
    g_tot = jnp.broadcast_to(b[:, CHUNK - 1:CHUNK], b.shape)
    m_prev = m_s[...]
    m_prevs = []
    for i in range(n_chunks):
        rs = slice(i * 2 * H, (i + 1) * 2 * H)
        m_prevs.append(m_prev)
        m_prev = g_tot[rs] + jnp.maximum(m_prev, u_max[rs])
    m_s[...] = m_prev
    m_prev = jnp.concatenate(m_prevs, axis=0)
    mm = jnp.maximum(m_prev, u_max)
    sp_all = jnp.exp(m_prev - mm)
    sl_all = jnp.exp(u_max - mm)
    wa_all = jnp.exp(u - u_max)
    m_in = jnp.maximum(m_prev, cm)
    cols = jnp.concatenate([m_in, jnp.exp(m_prev - m_in), jnp.exp(-(b + m_in)),
                            jnp.zeros_like(m_in)], axis=0).T

    ti = lax.broadcasted_iota(jnp.int32, (CHUNK, CHUNK), 0)
    si = lax.broadcasted_iota(jnp.int32, (CHUNK, CHUNK), 1)
    causal = si <= ti
    ones = jnp.ones((CHUNK, E), BF16)

    cts = [ct_s[h] for h in range(H)]
    nbs = [nb_s[h] for h in range(H)]
    for i in range(n_chunks):
        r0 = i * CHUNK
        for h in range(H):
            row = i * 2 * H + h
            col = lambda q_, row=row: cols[:, q_ * n_rows + row:q_ * n_rows + row + 1]
            m_in_c, e_w, e_m = col(0), col(1), col(2)
            q = ybuf[h, r0:r0 + CHUNK, :]
            k = ybuf[H + h, r0:r0 + CHUNK, :]
            v1 = jnp.concatenate([v_ref[r0:r0 + CHUNK, h * E:(h + 1) * E], ones], axis=1)
            qb = q.astype(BF16)

            d_w = jnp.where(causal, jnp.exp(u[row:row + 1, :] - m_in_c), 0.0)
            s_qk = lax.dot_general(qb, k.astype(BF16), (((1,), (1,)), ((), ())),
                                   preferred_element_type=F32) * d_w
            st = jnp.concatenate([cts[h].astype(BF16), nbs[h].astype(BF16)], axis=1)
            inter = jnp.dot(qb, st, preferred_element_type=F32)
            intra = jnp.dot(s_qk.astype(BF16), v1, preferred_element_type=F32)
            num = e_w * inter[:, :E] + intra[:, :E]
            den = e_w * inter[:, E:] + intra[:, E:]
            hh = num / jnp.maximum(jnp.abs(den), e_m)

            hg = og_ref[r0:r0 + CHUNK, h * E:(h + 1) * E] * hh
            mu = jnp.mean(hg, -1, keepdims=True)
            ex2 = jnp.mean(hg * hg, -1, keepdims=True)
            hc = hg - mu
            var = jnp.maximum(ex2 - mu * mu, 0.0)
            yn = hc * lax.rsqrt(var + LN_EPS) * mlg_ref[:, h * E:(h + 1) * E]
            out_ref[r0:r0 + CHUNK, h * E:(h + 1) * E] = yn.astype(out_ref.dtype)

            kw = (k.T * wa_all[row:row + 1, :]).astype(BF16)
            loc = jnp.dot(kw, v1, preferred_element_type=F32)
            sp_h, sl_h = sp_all[row:row + 1, :], sl_all[row:row + 1, :]
            cts[h] = sp_h * cts[h] + sl_h * loc[:, :E]
            nbs[h] = sp_h * nbs[h] + sl_h * loc[:, E:]
    for h in range(H):
        ct_s[h] = cts[h]
        nb_s[h] = nbs[h]


def _mlstm(qk, vm, og, gatest, conv_w, conv_b, gb_col, ml_g, B, S):
    rows = ML_CHUNKS_PER_STEP * CHUNK
    ns = S // rows
    W = vm.shape[1]
    G = gatest.shape[0]
    tri = jnp.tril(jnp.ones((CHUNK, CHUNK), F32)).astype(BF16)
    row = lambda w: pl.BlockSpec((rows, w), lambda b, c: (b * ns + c, 0))
    consts = (conv_w, conv_b, gb_col, ml_g, tri)
    return pl.pallas_call(
        _mlstm_kernel,
        grid=(B, ns),
        in_specs=[pl.BlockSpec((qk.shape[0], rows, LANES), lambda b, c: (0, b * ns + c, 0)),
                  row(W), row(W),
                  pl.BlockSpec((G, rows), lambda b, c: (0, b * ns + c))]
                 + [_const_spec(a.shape) for a in consts],
        out_specs=row(W),
        out_shape=jax.ShapeDtypeStruct((B * S, W), BF16),
        scratch_shapes=[pltpu.VMEM((qk.shape[0], rows + 8, LANES), F32),
                        pltpu.VMEM((qk.shape[0], rows, LANES), F32),
                        pltpu.VMEM((ML_HEADS, LANES, LANES), F32),
                        pltpu.VMEM((ML_HEADS, LANES, LANES), F32),
                        pltpu.VMEM((2 * ML_HEADS, LANES), F32)],
        compiler_params=pltpu.CompilerParams(
            dimension_semantics=("parallel", "arbitrary"), vmem_limit_bytes=VMEM_LIMIT),
        name="mlstm",
    )(qk, vm, og, gatest, *consts)


def _tail_kernel(x1_ref, att_ref, hm_ref, kt_ref, v_ref, wout_ref, wq_ref, wo_ref,
                 wg_ref, wu_ref, wd_ref, g_ref, b_ref, out_ref):
    sub = x1_ref.shape[0] // ROW_SUBTILES
    tiles = [slice(t * sub, (t + 1) * sub) for t in range(ROW_SUBTILES)]
    n_a = att_ref.shape[1]
    hd = wq_ref.shape[1] // XA_HEADS

    def mix(rs):
        return (jnp.dot(att_ref[rs, :], wout_ref[0:n_a, :], preferred_element_type=F32)
                + jnp.dot(hm_ref[rs, :], wout_ref[n_a:, :], preferred_element_type=F32))

    def cross(x2):
        q = jnp.dot(x2.astype(BF16), wq_ref[...], preferred_element_type=F32)
        heads = []
        for h in range(XA_HEADS):
            qh = q[:, h * hd:(h + 1) * hd].astype(BF16)
            lg = jnp.dot(qh, kt_ref[h * hd:(h + 1) * hd, :], preferred_element_type=F32)
            p = jnp.exp(lg - jnp.max(lg, -1, keepdims=True))
            s = jnp.sum(p, -1, keepdims=True)
            o = jnp.dot(p.astype(BF16), v_ref[:, h * hd:(h + 1) * hd],
                        preferred_element_type=F32)
            heads.append((o / s).astype(BF16))
        return jnp.dot(jnp.concatenate(heads, axis=1), wo_ref[...], preferred_element_type=F32)

    mixes = [mix(rs) for rs in tiles]
    x2s = [_layer_norm(ALPHA * x1_ref[rs, :] + m, g_ref[0:1, :], b_ref[0:1, :])
           for rs, m in zip(tiles, mixes)]
    xas = [cross(x2) for x2 in x2s]
    x3s = [_layer_norm(ALPHA * x2 + xa, g_ref[1:2, :], b_ref[1:2, :])
           for x2, xa in zip(x2s, xas)]
    ffs = [_swiglu(x3.astype(BF16), wg_ref, wu_ref, wd_ref) for x3 in x3s]
    for rs, x3, ff in zip(tiles, x3s, ffs):
        out_ref[rs, :] = _layer_norm(ALPHA * x3 + 0.5 * ff, g_ref[2:3, :], b_ref[2:3, :])


def _tail(x1, att, hm, kt, v, wout, wq, wo, wg, wu, wd, ffn_index, g, b, S):
    T, D = x1.shape
    tm = ROW_TILE
    per_batch = S // tm
    row = lambda w: pl.BlockSpec((tm, w), lambda i: (i, 0))
    L = v.shape[1]
    weights = (wout, wq, wo, wg, wu, wd, g, b)
    return pl.pallas_call(
        _tail_kernel,
        grid=(T // tm,),
        in_specs=[row(D), row(att.shape[1]), row(hm.shape[1]),
                  pl.BlockSpec((None, D, L), lambda i: (i // per_batch, 0, 0)),
                  pl.BlockSpec((None, L, D), lambda i: (i // per_batch, 0, 0))]
                 + _weight_specs(weights, ffn_index),
        out_specs=row(D),
        out_shape=jax.ShapeDtypeStruct((T, D), F32),
        compiler_params=pltpu.CompilerParams(
            dimension_semantics=("parallel",), vmem_limit_bytes=VMEM_LIMIT),
        name="tail",
    )(x1, att, hm, kt, v, *weights)


def kernel(x, mem, rel_bias, ln_g, ln_b, ffn_w_gate, ffn_w_up, ffn_w_down, w_in, conv_w, conv_b,
           ig_bias, fg_bias, ml_norm_g, w_out, xq_w, xkv_w, xo_w):
    B, S, D = x.shape
    att_w = ATT_HEADS * ATT_HD
    ml_w = ML_HEADS * LANES
    bias = _bias_tables(rel_bias)
    xf = x.reshape(B * S, D)
    bf = lambda a: a.astype(BF16)
    wg_all, wu_all, wd_all = bf(ffn_w_gate), bf(ffn_w_up), bf(ffn_w_down)
    for l in range(DEPTH):
        wint = bf(w_in[l].T)
        gb = jnp.concatenate([ig_bias[l], fg_bias[l]]).astype(F32)

        x1, qkv, qk, vm, og, gatest = _ffn_in(
            xf, wg_all, wu_all, wd_all, (l, 0),
            ln_g[l, 0][None], ln_b[l, 0][None], wint, 3 * att_w, 2 * ml_w, ml_w)
        att = _attention(qkv, bias, B, S).reshape(B * S, att_w)
        hm = _mlstm(qk, vm, og, gatest, conv_w[l], conv_b[l][None], gb[:, None],
                    ml_norm_g[l][None], B, S)

        hd = D // XA_HEADS
        kt, v = _memkv(mem, bf(xkv_w[l]))
        xf = _tail(x1, att, hm, kt, v, bf(w_out[l]), bf(xq_w[l] * hd ** -0.5), bf(xo_w[l]),
                   wg_all, wu_all, wd_all, (l, 1), ln_g[l, 1:4], ln_b[l, 1:4], S)
    return xf.reshape(B, S, D)
```

```python
import functools
import math

import jax
import jax.numpy as jnp
from jax import lax
from jax.experimental import pallas as pl
from jax.experimental.pallas import tpu as pltpu

F32 = jnp.float32
BF16 = jnp.bfloat16

ATT_HD = 64
ATT_HEADS = 8
DILATED = ((128, 1), (512, 4), (2048, 16))
BLK = 128
ML_HEADS = 4
CHUNK = 128
CONV_K = 4
XA_HEADS = 4
REL_BUCKETS = 32
REL_MAX_DIST = 2048
DEPTH = 1
ALPHA = (2 * DEPTH) ** 0.25
LN_EPS = 1e-5
NEG = -1e30
LOG2E = math.log2(math.e)

LANES = 128
VMEM_LIMIT = 60 * 1024 * 1024
ROW_TILE = 512
ROW_SUBTILES = 2
ATT_ORDER = (2, 1, 0)
ATT_UNROLL = 8
ML_CHUNKS_PER_STEP = 4


def _const_spec(shape, lead=()):
    block = (None,) * len(lead) + tuple(shape[len(lead):])
    index = tuple(lead) + (0,) * (len(shape) - len(lead))
    return pl.BlockSpec(block, lambda *_: index, pipeline_mode=pl.Buffered(1))


def _weight_specs(weights, ffn_index):
    return [_const_spec(w.shape, ffn_index if w.ndim == 4 else ()) for w in weights]


def _layer_norm(y, g, b):
    mu = jnp.mean(y, -1, keepdims=True)
    yc = y - mu
    var = jnp.mean(yc * yc, -1, keepdims=True)
    return yc * lax.rsqrt(var + LN_EPS) * g + b


def _silu(x):
    return x * jax.nn.sigmoid(x)


def _swiglu(xb, wg_ref, wu_ref, wd_ref):
    g = jnp.dot(xb, wg_ref[...], preferred_element_type=F32)
    u = jnp.dot(xb, wu_ref[...], preferred_element_type=F32)
    h = (_silu(g) * u).astype(BF16)
    return jnp.dot(h, wd_ref[...], preferred_element_type=F32)


def _bias_kernel(rel_ref, bkt_ref, out_ref):
    bkt = bkt_ref[0]
    prev_half = lax.broadcasted_iota(jnp.int32, bkt.shape, 1) < BLK
    accs = [jnp.full(bkt.shape, NEG, F32) for _ in range(ATT_HEADS)]
    for b in range(REL_BUCKETS):
        hit = bkt == b
        for h in range(ATT_HEADS):
            accs[h] = jnp.where(hit, rel_ref[b, h] * LOG2E, accs[h])
    for h in range(ATT_HEADS):
        out_ref[0, 0, h] = accs[h]
        out_ref[1, 0, h] = jnp.where(prev_half, NEG, accs[h])


def _bias_tables(rel_bias):
    qi = jnp.arange(BLK)[:, None]
    ki = jnp.arange(2 * BLK)[None, :]
    off = qi + BLK - ki
    exact = REL_BUCKETS // 2
    n_log = REL_BUCKETS - exact
    starts = [math.ceil(exact * (REL_MAX_DIST / exact) ** (k / n_log)) for k in range(1, n_log)]
    tabs = []
    for window, dil in DILATED:
        n_keys = window // dil
        dist = dil * jnp.clip(off, 0, n_keys)
        large = exact + sum((dist >= s).astype(jnp.int32) for s in starts)
        bucket = jnp.where(dist < exact, dist, large)
        band = (off >= 0) & (off <= n_keys)
        tabs.append(jnp.where(band, bucket, -1))
    bkt = jnp.stack(tabs, 0).astype(jnp.int32)
    nd = len(DILATED)
    return pl.pallas_call(
        _bias_kernel,
        grid=(nd,),
        in_specs=[pl.BlockSpec(memory_space=pltpu.SMEM),
                  pl.BlockSpec((1, BLK, 2 * BLK), lambda d: (d, 0, 0))],
        out_specs=pl.BlockSpec((2, 1, ATT_HEADS, BLK, 2 * BLK), lambda d: (0, d, 0, 0, 0)),
        out_shape=jax.ShapeDtypeStruct((2, nd, ATT_HEADS, BLK, 2 * BLK), F32),
        name="bias",
    )(rel_bias.astype(F32), bkt)


def _memkv_kernel(mem_ref, wkv_ref, kt_ref, v_ref):
    D = mem_ref.shape[1]
    mb = mem_ref[...].astype(BF16)
    k = jnp.dot(mb, wkv_ref[:, 0:D], preferred_element_type=F32)
    kt_ref[...] = k.T.astype(BF16)
    v_ref[...] = jnp.dot(mb, wkv_ref[:, D:], preferred_element_type=F32).astype(BF16)


def _memkv(mem, wkv):
    B, L, D = mem.shape
    return pl.pallas_call(
        _memkv_kernel,
        grid=(B,),
        in_specs=[pl.BlockSpec((None, L, D), lambda b: (b, 0, 0)),
                  _const_spec((D, 2 * D))],
        out_specs=[pl.BlockSpec((None, D, L), lambda b: (b, 0, 0)),
                   pl.BlockSpec((None, L, D), lambda b: (b, 0, 0))],
        out_shape=[jax.ShapeDtypeStruct((B, D, L), BF16),
                   jax.ShapeDtypeStruct((B, L, D), BF16)],
        compiler_params=pltpu.CompilerParams(vmem_limit_bytes=VMEM_LIMIT),
        name="memkv",
    )(mem, wkv)


def _fold_rows(kv, t, s1, s2, fold_refs):
    (d1, ref1), (d2, ref2) = fold_refs
    ratio = d2 // d1
    sub = kv.shape[0]
    n1, n2 = sub // d1, sub // d2
    for c in range(kv.shape[1] // LANES):
        cl = slice(c * LANES, (c + 1) * LANES)
        s1[c] = kv[:, cl]
        for r1 in range(d1):
            p1 = s1[c, pl.ds(r1, n1, stride=d1), :]
            ref1[r1, t * n1:(t + 1) * n1, cl] = p1.astype(BF16)
            s2[c, r1 * n1:(r1 + 1) * n1, :] = p1
        for r1 in range(d1):
            for rp in range(ratio):
                p2 = s2[c, pl.ds(r1 * n1 + rp, n2, stride=ratio), :]
                ref2[d1 * rp + r1, t * n2:(t + 1) * n2, cl] = p2.astype(BF16)


def _ffn_in_kernel(x_ref, wg_ref, wu_ref, wd_ref, g_ref, b_ref, wint_ref,
                   x1_ref, qa_ref, kv1_ref, kvf1_ref, kvf2_ref, qk_ref, vm_ref, og_ref,
                   gatest_ref, s1, s2):
    n_q = qa_ref.shape[1]
    c1 = n_q + kv1_ref.shape[1]
    c2 = c1 + qk_ref.shape[0] * LANES
    c3 = c2 + vm_ref.shape[1]
    c4 = c3 + og_ref.shape[1]
    nt = (((1,), (1,)), ((), ()))
    sub = x_ref.shape[0] // ROW_SUBTILES
    tiles = [slice(t * sub, (t + 1) * sub) for t in range(ROW_SUBTILES)]
    ffs = [_swiglu(x_ref[rs, :].astype(BF16), wg_ref, wu_ref, wd_ref) for rs in tiles]
    fold_dils = [d for _, d in DILATED if d > 1]
    for t, (rs, ff) in enumerate(zip(tiles, ffs)):
        x1 = _layer_norm(ALPHA * x_ref[rs, :] + 0.5 * ff, g_ref[...], b_ref[...])
        x1_ref[rs, :] = x1
        xb = x1.astype(BF16)
        proj = lambda lo, hi, xb=xb: lax.dot_general(xb, wint_ref[lo:hi, :], nt,
                                                     preferred_element_type=F32)

        qkm = proj(c1, c2)
        for c in range(qk_ref.shape[0]):
            qk_ref[c, rs, :] = qkm[:, c * LANES:(c + 1) * LANES]

        qa_ref[rs, :] = proj(0, n_q) * (ATT_HD ** -0.5 * LOG2E)
        kv = proj(n_q, c1)
        kv1_ref[rs, :] = kv.astype(BF16)
        _fold_rows(kv, t, s1, s2, list(zip(fold_dils, (kvf1_ref, kvf2_ref))))
        vm_ref[rs, :] = proj(c2, c3).astype(BF16)
        og_ref[rs, :] = jax.nn.sigmoid(proj(c3, c4))
        gatest_ref[:, rs] = lax.dot_general(wint_ref[c4:c4 + gatest_ref.shape[0], :], xb, nt,
                                            preferred_element_type=F32)


def _ffn_in(x2d, wg, wu, wd, ffn_index, g, b, wint, n_att, n_qkm, n_ml, S):
    T, D = x2d.shape
    tm = ROW_TILE
    tps = S // tm
    row = lambda w: pl.BlockSpec((tm, w), lambda i: (i, 0))
    n_g = wint.shape[0] - n_att - n_qkm - 2 * n_ml
    n_slab = n_qkm // LANES
    n_q, n_kv = n_att // 3, 2 * n_att // 3
    fold = lambda d: pl.BlockSpec((None, d, tm // d, n_kv), lambda i: (i // tps, 0, i % tps, 0))
    fold_shape = lambda d: jax.ShapeDtypeStruct((T // S, d, S // d, n_kv), BF16)
    d1, d2 = [d for _, d in DILATED if d > 1]
    assert d2 % d1 == 0 and (tm // ROW_SUBTILES) % (16 * d2) == 0
    weights = (wg, wu, wd, g, b, wint)
    sub = tm // ROW_SUBTILES
    return pl.pallas_call(
        _ffn_in_kernel,
        grid=(T // tm,),
        in_specs=[row(D)] + _weight_specs(weights, ffn_index),
        out_specs=[row(D), row(n_q), row(n_kv), fold(d1), fold(d2),
                   pl.BlockSpec((n_slab, tm, LANES), lambda i: (0, i, 0)),
                   row(n_ml), row(n_ml),
                   pl.BlockSpec((n_g, tm), lambda i: (0, i))],
        out_shape=[jax.ShapeDtypeStruct((T, D), F32),
                   jax.ShapeDtypeStruct((T, n_q), F32),
                   jax.ShapeDtypeStruct((T, n_kv), BF16),
                   fold_shape(d1), fold_shape(d2),
                   jax.ShapeDtypeStruct((n_slab, T, LANES), F32),
                   jax.ShapeDtypeStruct((T, n_ml), BF16),
                   jax.ShapeDtypeStruct((T, n_ml), F32),
                   jax.ShapeDtypeStruct((n_g, T), F32)],
        scratch_shapes=[pltpu.VMEM((n_kv // LANES, sub, LANES), F32)] * 2,
        compiler_params=pltpu.CompilerParams(
            dimension_semantics=("parallel",), vmem_limit_bytes=VMEM_LIMIT),
        name="ffn_in",
    )(x2d, *weights)


def _attn_block(q_ref, kv_refs, bias_ref, out_ref, acc_s, m_s, l_s, j, step):
    di = ATT_ORDER[step]
    dil = DILATED[di][1]
    k_ref, v_ref = kv_refs[di]
    lane = lax.broadcasted_iota(jnp.int32, (BLK, LANES), 1)
    head0 = lane < ATT_HD
    r = j % dil
    n = j // dil

    def rows_of(start, stride):
        if stride == 1:
            return pl.ds(pl.multiple_of(start, BLK), BLK)
        return pl.ds(start, BLK, stride=stride)

    cur = rows_of(n * BLK, 1)
    prev = rows_of(jnp.maximum(n - 1, 0) * BLK, 1)
    if dil == 1:
        take = lambda ref, rows: ref[rows, :]
    else:
        take = lambda ref, rows: ref[r, rows, :]
    rows = rows_of(r + n * (BLK * dil), dil)
    q = q_ref[rows, :]
    kk = jnp.concatenate([take(k_ref, prev), take(k_ref, cur)], axis=0)
    vv = jnp.concatenate([take(v_ref, prev), take(v_ref, cur)], axis=0)
    first = jnp.where(n == 0, 1, 0)
    ms, ls, os_ = [], [], []
    for h in range(2):
        sel = head0 if h == 0 else jnp.logical_not(head0)
        qh = jnp.where(sel, q, 0.0).astype(BF16)
        lg = lax.dot_general(qh, kk, (((1,), (1,)), ((), ())), preferred_element_type=F32)
        lg = lg + bias_ref[first, di, h]
        mh = jnp.max(lg, -1, keepdims=True)
        p = jnp.exp2(lg - mh)
        ls.append(jnp.sum(p, -1, keepdims=True))
        ms.append(mh)
        os_.append(jnp.dot(p.astype(BF16), vv, preferred_element_type=F32))
    m_b = jnp.where(head0, ms[0], ms[1])
    l_b = jnp.where(head0, ls[0], ls[1])
    o_b = jnp.where(head0, os_[0], os_[1])
    if step > 0:
        m_o = m_s[rows, :]
        m_n = jnp.maximum(m_o, m_b)
        e_o = jnp.exp2(m_o - m_n)
        e_b = jnp.exp2(m_b - m_n)
        l_b = l_s[rows, :] * e_o + l_b * e_b
        o_b = acc_s[rows, :] * e_o + o_b * e_b
        m_b = m_n
    if step < len(ATT_ORDER) - 1:
        m_s[rows, :] = m_b
        l_s[rows, :] = l_b
        acc_s[rows, :] = o_b
    else:
        out_ref[rows, :] = (o_b / l_b).astype(out_ref.dtype)


def _attn_kernel(q_ref, *refs):
    n_br = len(DILATED)
    kv_refs = [(refs[2 * i], refs[2 * i + 1]) for i in range(n_br)]
    bias_ref, out_ref, acc_s, m_s, l_s = refs[2 * n_br:]
    n_blocks = q_ref.shape[0] // BLK
    for step in range(len(ATT_ORDER)):
        def body(i, _, step=step):
            for u in range(ATT_UNROLL):
                _attn_block(q_ref, kv_refs, bias_ref, out_ref, acc_s, m_s, l_s,
                            i * ATT_UNROLL + u, step)
            return 0
        lax.fori_loop(0, n_blocks // ATT_UNROLL, body, 0)


def _attention(q, kvs, bias, B, S):
    n_pairs = ATT_HEADS * ATT_HD // LANES
    dils = [d for _, d in DILATED]
    assert dils[ATT_ORDER[-1]] == 1
    in_specs = [pl.BlockSpec((None, S, LANES), lambda b, p: (b, 0, p))]
    operands = [q.reshape(B, S, q.shape[-1])]
    for d, kv in zip(dils, kvs):
        for off in (0, n_pairs):
            if d == 1:
                in_specs.append(pl.BlockSpec((None, S, LANES),
                                             lambda b, p, off=off: (b, 0, off + p)))
            else:
                in_specs.append(pl.BlockSpec((None, d, S // d, LANES),
                                             lambda b, p, off=off: (b, 0, 0, off + p)))
            operands.append(kv)
    in_specs.append(pl.BlockSpec((2, len(DILATED), 2, BLK, 2 * BLK),
                                 lambda b, p: (0, 0, p, 0, 0)))
    return pl.pallas_call(
        _attn_kernel,
        grid=(B, n_pairs),
        in_specs=in_specs,
        out_specs=pl.BlockSpec((None, S, LANES), lambda b, p: (b, 0, p)),
        out_shape=jax.ShapeDtypeStruct((B, S, n_pairs * LANES), BF16),
        scratch_shapes=[pltpu.VMEM((S, LANES), F32)] * 3,
        compiler_params=pltpu.CompilerParams(
            dimension_semantics=("parallel", "parallel"), vmem_limit_bytes=VMEM_LIMIT),
        name="attn",
    )(*operands, bias)


def _split3(a):
    hi = a.astype(BF16)
    r1 = a - hi.astype(F32)
    mid = r1.astype(BF16)
    lo = (r1 - mid.astype(F32)).astype(BF16)
    return hi, mid, lo


def _log_sigmoid(x):
    return jnp.minimum(x, 0.0) - jnp.log1p(jnp.exp(-jnp.abs(x)))


def _prefix_max_lanes(x):
    lane = lax.broadcasted_iota(jnp.int32, x.shape, 1)
    sh = 1
    while sh < x.shape[1]:
        x = jnp.maximum(x, jnp.where(lane >= sh, pltpu.roll(x, sh, axis=1), -jnp.inf))
        sh *= 2
    return x


def _mlstm_kernel(qk_ref, v_ref, og_ref, gt_ref, cw_ref, cb_ref, gbc_ref, mlg_ref, tri_ref,
                  out_ref, xbuf, ybuf, ct_s, nb_s, m_s):
    step = pl.program_id(1)
    E = LANES
    H = ML_HEADS
    n_slab, rows = qk_ref.shape[0], qk_ref.shape[1]
    half = rows // 2
    n_chunks = rows // CHUNK
    assert n_slab == 2 * H
    assert (n_chunks * 4 * 2 * H) % LANES == 0

    @pl.when(step == 0)
    def _():
        xbuf[:, 0:8, :] = jnp.zeros((n_slab, 8, LANES), F32)
        ct_s[...] = jnp.zeros(ct_s.shape, F32)
        nb_s[...] = jnp.zeros(nb_s.shape, F32)
        m_s[...] = jnp.zeros(m_s.shape, F32)

    @pl.when(step > 0)
    def _():
        xbuf[:, 0:8, :] = xbuf[:, rows:rows + 8, :]

    xbuf[:, 8:8 + rows, :] = qk_ref[...]
    for c in range(n_slab):
        cl = slice(c * LANES, (c + 1) * LANES)
        for par in range(2):
            acc = cb_ref[:, cl]
            for j in range(CONV_K):
                off = 8 - (CONV_K - 1) + j + par
                acc = acc + xbuf[c, pl.ds(off, half, stride=2), :] * cw_ref[j:j + 1, cl]
            act = _silu(acc)
            if c >= H:
                act = act * (E ** -0.5)
            ybuf[c, pl.ds(par, half, stride=2), :] = act

    tri = tri_ref[...]
    n_rows = 2 * H * n_chunks
    head_row = (lax.broadcasted_iota(jnp.int32, (n_rows, CHUNK), 0) & (2 * H - 1)) < H
    gr = jnp.concatenate([gt_ref[:, i * CHUNK:(i + 1) * CHUNK] + gbc_ref[...]
                          for i in range(n_chunks)], axis=0)
    b_all = sum(lax.dot_general(part, tri, (((1,), (1,)), ((), ())),
                                preferred_element_type=F32)
                for part in _split3(_log_sigmoid(gr)))
    b = jnp.concatenate([pltpu.roll(b_all[i * 2 * H:(i + 1) * 2 * H], H, axis=0)
                         for i in range(n_chunks)], axis=0)
    b = jnp.where(head_row, b, 0.0)
    u = jnp.where(head_row, gr - b, 0.0)
    cm = _prefix_max_lanes(u)
    u_max = jnp.broadcast_to(cm[:, CHUNK - 1:CHUNK], cm.shape)
    g_tot = jnp.broadcast_to(b[:, CHUNK - 1:CHUNK], b.shape)
    m_prev = m_s[...]
    m_prevs = []
    for i in range(n_chunks):
        rs = slice(i * 2 * H, (i + 1) * 2 * H)
        m_prevs.append(m_prev)
        m_prev = g_tot[rs] + jnp.maximum(m_prev, u_max[rs])
    m_s[...] = m_prev
    m_prev = jnp.concatenate(m_prevs, axis=0)
    mm = jnp.maximum(m_prev, u_max)
    sp_all = jnp.exp(m_prev - mm)
    sl_all = jnp.exp(u_max - mm)
    wa_all = jnp.exp(u - u_max)
    m_in = jnp.maximum(m_prev, cm)
    cols = jnp.concatenate([m_in, jnp.exp(m_prev - m_in), jnp.exp(-(b + m_in)),
                            jnp.zeros_like(m_in)], axis=0).T

    ti = lax.broadcasted_iota(jnp.int32, (CHUNK, CHUNK), 0)
    si = lax.broadcasted_iota(jnp.int32, (CHUNK, CHUNK), 1)
    causal = si <= ti
    ones = jnp.ones((CHUNK, E), BF16)

    cts = [ct_s[h] for h in range(H)]
    nbs = [nb_s[h] for h in range(H)]
    for i in range(n_chunks):
        r0 = i * CHUNK
        for h in range(H):
            row = i * 2 * H + h
            col = lambda q_, row=row: cols[:, q_ * n_rows + row:q_ * n_rows + row + 1]
            m_in_c, e_w, e_m = col(0), col(1), col(2)
            q = ybuf[h, r0:r0 + CHUNK, :]
            k = ybuf[H + h, r0:r0 + CHUNK, :]
            v1 = jnp.concatenate([v_ref[r0:r0 + CHUNK, h * E:(h + 1) * E], ones], axis=1)
            qb = q.astype(BF16)

            d_w = jnp.where(causal, jnp.exp(u[row:row + 1, :] - m_in_c), 0.0)
            s_qk = lax.dot_general(qb, k.astype(BF16), (((1,), (1,)), ((), ())),
                                   preferred_element_type=F32) * d_w
            st = jnp.concatenate([cts[h].astype(BF16), nbs[h].astype(BF16)], axis=1)
            inter = jnp.dot(qb, st, preferred_element_type=F32)
            intra = jnp.dot(s_qk.astype(BF16), v1, preferred_element_type=F32)
            num = e_w * inter[:, :E] + intra[:, :E]
            den = e_w * inter[:, E:] + intra[:, E:]
            hh = num / jnp.maximum(jnp.abs(den), e_m)

            hg = og_ref[r0:r0 + CHUNK, h * E:(h + 1) * E] * hh
            mu = jnp.mean(hg, -1, keepdims=True)
            ex2 = jnp.mean(hg * hg, -1, keepdims=True)
            hc = hg - mu
            var = jnp.maximum(ex2 - mu * mu, 0.0)
            yn = hc * lax.rsqrt(var + LN_EPS) * mlg_ref[:, h * E:(h + 1) * E]
            out_ref[r0:r0 + CHUNK, h * E:(h + 1) * E] = yn.astype(out_ref.dtype)

            kw = (k.T * wa_all[row:row + 1, :]).astype(BF16)
            loc = jnp.dot(kw, v1, preferred_element_type=F32)
            sp_h, sl_h = sp_all[row:row + 1, :], sl_all[row:row + 1, :]
            cts[h] = sp_h * cts[h] + sl_h * loc[:, :E]
            nbs[h] = sp_h * nbs[h] + sl_h * loc[:, E:]
    for h in range(H):
        ct_s[h] = cts[h]
        nb_s[h] = nbs[h]


def _mlstm(qk, vm, og, gatest, conv_w, conv_b, gb_col, ml_g, B, S):
    rows = ML_CHUNKS_PER_STEP * CHUNK
    ns = S // rows
    W = vm.shape[1]
    G = gatest.shape[0]
    tri = jnp.tril(jnp.ones((CHUNK, CHUNK), F32)).astype(BF16)
    row = lambda w: pl.BlockSpec((rows, w), lambda b, c: (b * ns + c, 0))
    consts = (conv_w, conv_b, gb_col, ml_g, tri)
    return pl.pallas_call(
        _mlstm_kernel,
        grid=(B, ns),
        in_specs=[pl.BlockSpec((qk.shape[0], rows, LANES), lambda b, c: (0, b * ns + c, 0)),
                  row(W), row(W),
                  pl.BlockSpec((G, rows), lambda b, c: (0, b * ns + c))]
                 + [_const_spec(a.shape) for a in consts],
        out_specs=row(W),
        out_shape=jax.ShapeDtypeStruct((B * S, W), BF16),
        scratch_shapes=[pltpu.VMEM((qk.shape[0], rows + 8, LANES), F32),
                        pltpu.VMEM((qk.shape[0], rows, LANES), F32),
                        pltpu.VMEM((ML_HEADS, LANES, LANES), F32),
                        pltpu.VMEM((ML_HEADS, LANES, LANES), F32),
                        pltpu.VMEM((2 * ML_HEADS, LANES), F32)],
        compiler_params=pltpu.CompilerParams(
            dimension_semantics=("parallel", "arbitrary"), vmem_limit_bytes=VMEM_LIMIT),
        name="mlstm",
    )(qk, vm, og, gatest, *consts)


def _tail_kernel(x1_ref, att_ref, hm_ref, kt_ref, v_ref, wout_ref, wq_ref, wo_ref,
                 wg_ref, wu_ref, wd_ref, g_ref, b_ref, out_ref):
    sub = x1_ref.shape[0] // ROW_SUBTILES
    tiles = [slice(t * sub, (t + 1) * sub) for t in range(ROW_SUBTILES)]
    n_a = att_ref.shape[1]
    hd = wq_ref.shape[1] // XA_HEADS

    def mix(rs):
        return (jnp.dot(att_ref[rs, :], wout_ref[0:n_a, :], preferred_element_type=F32)
                + jnp.dot(hm_ref[rs, :], wout_ref[n_a:, :], preferred_element_type=F32))

    def cross(x2):
        q = jnp.dot(x2.astype(BF16), wq_ref[...], preferred_element_type=F32)
        heads = []
        for h in range(XA_HEADS):
            qh = q[:, h * hd:(h + 1) * hd].astype(BF16)
            lg = jnp.dot(qh, kt_ref[h * hd:(h + 1) * hd, :], preferred_element_type=F32)
            p = jnp.exp(lg - jnp.max(lg, -1, keepdims=True))
            s = jnp.sum(p, -1, keepdims=True)
            o = jnp.dot(p.astype(BF16), v_ref[:, h * hd:(h + 1) * hd],
                        preferred_element_type=F32)
            heads.append((o / s).astype(BF16))
        return jnp.dot(jnp.concatenate(heads, axis=1), wo_ref[...], preferred_element_type=F32)

    mixes = [mix(rs) for rs in tiles]
    x2s = [_layer_norm(ALPHA * x1_ref[rs, :] + m, g_ref[0:1, :], b_ref[0:1, :])
           for rs, m in zip(tiles, mixes)]
    xas = [cross(x2) for x2 in x2s]
    x3s = [_layer_norm(ALPHA * x2 + xa, g_ref[1:2, :], b_ref[1:2, :])
           for x2, xa in zip(x2s, xas)]
    ffs = [_swiglu(x3.astype(BF16), wg_ref, wu_ref, wd_ref) for x3 in x3s]
    for rs, x3, ff in zip(tiles, x3s, ffs):
        out_ref[rs, :] = _layer_norm(ALPHA * x3 + 0.5 * ff, g_ref[2:3, :], b_ref[2:3, :])


def _tail(x1, att, hm, kt, v, wout, wq, wo, wg, wu, wd, ffn_index, g, b, S):
    T, D = x1.shape
    tm = ROW_TILE
    per_batch = S // tm
    row = lambda w: pl.BlockSpec((tm, w), lambda i: (i, 0))
    L = v.shape[1]
    weights = (wout, wq, wo, wg, wu, wd, g, b)
    return pl.pallas_call(
        _tail_kernel,
        grid=(T // tm,),
        in_specs=[row(D), row(att.shape[1]), row(hm.shape[1]),
                  pl.BlockSpec((None, D, L), lambda i: (i // per_batch, 0, 0)),
                  pl.BlockSpec((None, L, D), lambda i: (i // per_batch, 0, 0))]
                 + _weight_specs(weights, ffn_index),
        out_specs=row(D),
        out_shape=jax.ShapeDtypeStruct((T, D), F32),
        compiler_params=pltpu.CompilerParams(
            dimension_semantics=("parallel",), vmem_limit_bytes=VMEM_LIMIT),
        name="tail",
    )(x1, att, hm, kt, v, *weights)


def kernel(x, mem, rel_bias, ln_g, ln_b, ffn_w_gate, ffn_w_up, ffn_w_down, w_in, conv_w, conv_b,
           ig_bias, fg_bias, ml_norm_g, w_out, xq_w, xkv_w, xo_w):
    B, S, D = x.shape
    att_w = ATT_HEADS * ATT_HD
    ml_w = ML_HEADS * LANES
    bias = _bias_tables(rel_bias)
    xf = x.reshape(B * S, D)
    bf = lambda a: a.astype(BF16)
    wg_all, wu_all, wd_all = bf(ffn_w_gate), bf(ffn_w_up), bf(ffn_w_down)
    for l in range(DEPTH):
        wint = bf(w_in[l].T)
        gb = jnp.concatenate([ig_bias[l], fg_bias[l]]).astype(F32)

        x1, qa, kv1, kvf1, kvf2, qk, vm, og, gatest = _ffn_in(
            xf, wg_all, wu_all, wd_all, (l, 0),
            ln_g[l, 0][None], ln_b[l, 0][None], wint, 3 * att_w, 2 * ml_w, ml_w, S)
        att = _attention(qa, [kv1.reshape(B, S, -1), kvf1, kvf2], bias, B, S)
        att = att.reshape(B * S, att_w)
        hm = _mlstm(qk, vm, og, gatest, conv_w[l], conv_b[l][None], gb[:, None],
                    ml_norm_g[l][None], B, S)

        hd = D // XA_HEADS
        kt, v = _memkv(mem, bf(xkv_w[l]))
        xf = _tail(x1, att, hm, kt, v, bf(w_out[l]), bf(xq_w[l] * hd ** -0.5), bf(xo_w[l]),
                   wg_all, wu_all, wd_all, (l, 1), ln_g[l, 1:4], ln_b[l, 1:4], S)
    return xf.reshape(B, S, D)
```

```python
import functools
import math

import jax
import jax.numpy as jnp
from jax import lax
from jax.experimental import pallas as pl
from jax.experimental.pallas import tpu as pltpu

F32 = jnp.float32
BF16 = jnp.bfloat16

ATT_HD = 64
ATT_HEADS = 8
DILATED = ((128, 1), (512, 4), (2048, 16))
BLK = 128
ML_HEADS = 4
CHUNK = 128
CONV_K = 4
XA_HEADS = 4
REL_BUCKETS = 32
REL_MAX_DIST = 2048
DEPTH = 1
ALPHA = (2 * DEPTH) ** 0.25
LN_EPS = 1e-5
NEG = -1e30
LOG2E = math.log2(math.e)

LANES = 128
VMEM_LIMIT = 60 * 1024 * 1024
ROW_TILE = 512
ROW_SUBTILES = 2
ATT_ORDER = (2, 1, 0)
ATT_UNROLL = 8
ML_CHUNKS_PER_STEP = 4


def _const_spec(shape, lead=()):
    block = (None,) * len(lead) + tuple(shape[len(lead):])
    index = tuple(lead) + (0,) * (len(shape) - len(lead))
    return pl.BlockSpec(block, lambda *_: index, pipeline_mode=pl.Buffered(1))


def _weight_specs(weights, ffn_index):
    return [_const_spec(w.shape, ffn_index if w.ndim == 4 else ()) for w in weights]


def _layer_norm(y, g, b):
    mu = jnp.mean(y, -1, keepdims=True)
    yc = y - mu
    var = jnp.mean(yc * yc, -1, keepdims=True)
    return yc * lax.rsqrt(var + LN_EPS) * g + b


def _silu(x):
    return x * jax.nn.sigmoid(x)


def _swiglu(xb, wg_ref, wu_ref, wd_ref):
    g = jnp.dot(xb, wg_ref[...], preferred_element_type=F32)
    u = jnp.dot(xb, wu_ref[...], preferred_element_type=F32)
    h = (_silu(g) * u).astype(BF16)
    return jnp.dot(h, wd_ref[...], preferred_element_type=F32)


def _bias_kernel(rel_ref, bkt_ref, out_ref):
    bkt = bkt_ref[0]
    prev_half = lax.broadcasted_iota(jnp.int32, bkt.shape, 1) < BLK
    accs = [jnp.full(bkt.shape, NEG, F32) for _ in range(ATT_HEADS)]
    for b in range(REL_BUCKETS):
        hit = bkt == b
        for h in range(ATT_HEADS):
            accs[h] = jnp.where(hit, rel_ref[b, h] * LOG2E, accs[h])
    for h in range(ATT_HEADS):
        out_ref[0, 0, h] = accs[h]
        out_ref[1, 0, h] = jnp.where(prev_half, NEG, accs[h])


def _bias_tables(rel_bias):
    qi = jnp.arange(BLK)[:, None]
    ki = jnp.arange(2 * BLK)[None, :]
    off = qi + BLK - ki
    exact = REL_BUCKETS // 2
    n_log = REL_BUCKETS - exact
    starts = [math.ceil(exact * (REL_MAX_DIST / exact) ** (k / n_log)) for k in range(1, n_log)]
    tabs = []
    for window, dil in DILATED:
        n_keys = window // dil
        dist = dil * jnp.clip(off, 0, n_keys)
        large = exact + sum((dist >= s).astype(jnp.int32) for s in starts)
        bucket = jnp.where(dist < exact, dist, large)
        band = (off >= 0) & (off <= n_keys)
        tabs.append(jnp.where(band, bucket, -1))
    bkt = jnp.stack(tabs, 0).astype(jnp.int32)
    nd = len(DILATED)
    return pl.pallas_call(
        _bias_kernel,
        grid=(nd,),
        in_specs=[pl.BlockSpec(memory_space=pltpu.SMEM),
                  pl.BlockSpec((1, BLK, 2 * BLK), lambda d: (d, 0, 0))],
        out_specs=pl.BlockSpec((2, 1, ATT_HEADS, BLK, 2 * BLK), lambda d: (0, d, 0, 0, 0)),
        out_shape=jax.ShapeDtypeStruct((2, nd, ATT_HEADS, BLK, 2 * BLK), F32),
        name="bias",
    )(rel_bias.astype(F32), bkt)


def _memkv_kernel(mem_ref, wkv_ref, kt_ref, v_ref):
    D = mem_ref.shape[1]
    mb = mem_ref[...].astype(BF16)
    k = jnp.dot(mb, wkv_ref[:, 0:D], preferred_element_type=F32)
    kt_ref[...] = k.T.astype(BF16)
    v_ref[...] = jnp.dot(mb, wkv_ref[:, D:], preferred_element_type=F32).astype(BF16)


def _memkv(mem, wkv):
    B, L, D = mem.shape
    return pl.pallas_call(
        _memkv_kernel,
        grid=(B,),
        in_specs=[pl.BlockSpec((None, L, D), lambda b: (b, 0, 0)),
                  _const_spec((D, 2 * D))],
        out_specs=[pl.BlockSpec((None, D, L), lambda b: (b, 0, 0)),
                   pl.BlockSpec((None, L, D), lambda b: (b, 0, 0))],
        out_shape=[jax.ShapeDtypeStruct((B, D, L), BF16),
                   jax.ShapeDtypeStruct((B, L, D), BF16)],
        compiler_params=pltpu.CompilerParams(vmem_limit_bytes=VMEM_LIMIT),
        name="memkv",
    )(mem, wkv)


def _fold_rows(kv, t, s1, s2, fold_refs):
    (d1, ref1), (d2, ref2) = fold_refs
    ratio = d2 // d1
    sub = kv.shape[0]
    n1, n2 = sub // d1, sub // d2
    for c in range(kv.shape[1] // LANES):
        cl = slice(c * LANES, (c + 1) * LANES)
        s1[c] = kv[:, cl]
        for r1 in range(d1):
            p1 = s1[c, pl.ds(r1, n1, stride=d1), :]
            ref1[r1, t * n1:(t + 1) * n1, cl] = p1.astype(BF16)
            s2[c, r1 * n1:(r1 + 1) * n1, :] = p1
        for r1 in range(d1):
            for rp in range(ratio):
                p2 = s2[c, pl.ds(r1 * n1 + rp, n2, stride=ratio), :]
                ref2[d1 * rp + r1, t * n2:(t + 1) * n2, cl] = p2.astype(BF16)


def _ffn_in_kernel(x_ref, wg_ref, wu_ref, wd_ref, g_ref, b_ref, wint_ref,
                   x1_ref, qa_ref, kv1_ref, kvf1_ref, kvf2_ref, qk_ref, vm_ref, og_ref,
                   gatest_ref, s1, s2):
    n_q = qa_ref.shape[1]
    c1 = n_q + kv1_ref.shape[1]
    c2 = c1 + qk_ref.shape[0] * LANES
    c3 = c2 + vm_ref.shape[1]
    c4 = c3 + og_ref.shape[1]
    nt = (((1,), (1,)), ((), ()))
    sub = x_ref.shape[0] // ROW_SUBTILES
    tiles = [slice(t * sub, (t + 1) * sub) for t in range(ROW_SUBTILES)]
    ffs = [_swiglu(x_ref[rs, :].astype(BF16), wg_ref, wu_ref, wd_ref) for rs in tiles]
    fold_dils = [d for _, d in DILATED if d > 1]
    for t, (rs, ff) in enumerate(zip(tiles, ffs)):
        x1 = _layer_norm(ALPHA * x_ref[rs, :] + 0.5 * ff, g_ref[...], b_ref[...])
        x1_ref[rs, :] = x1
        xb = x1.astype(BF16)
        proj = lambda lo, hi, xb=xb: lax.dot_general(xb, wint_ref[lo:hi, :], nt,
                                                     preferred_element_type=F32)

        qkm = proj(c1, c2)
        for c in range(qk_ref.shape[0]):
            qk_ref[c, rs, :] = qkm[:, c * LANES:(c + 1) * LANES]

        qa_ref[rs, :] = proj(0, n_q) * (ATT_HD ** -0.5 * LOG2E)
        kv = proj(n_q, c1)
        kv1_ref[rs, :] = kv.astype(BF16)
        _fold_rows(kv, t, s1, s2, list(zip(fold_dils, (kvf1_ref, kvf2_ref))))
        vm_ref[rs, :] = proj(c2, c3).astype(BF16)
        og_ref[rs, :] = jax.nn.sigmoid(proj(c3, c4))
        gatest_ref[:, rs] = lax.dot_general(wint_ref[c4:c4 + gatest_ref.shape[0], :], xb, nt,
                                            preferred_element_type=F32)


def _ffn_in(x2d, wg, wu, wd, ffn_index, g, b, wint, n_att, n_qkm, n_ml, S):
    T, D = x2d.shape
    tm = ROW_TILE
    tps = S // tm
    row = lambda w: pl.BlockSpec((tm, w), lambda i: (i, 0))
    n_g = wint.shape[0] - n_att - n_qkm - 2 * n_ml
    n_slab = n_qkm // LANES
    n_q, n_kv = n_att // 3, 2 * n_att // 3
    fold = lambda d: pl.BlockSpec((None, d, tm // d, n_kv), lambda i: (i // tps, 0, i % tps, 0))
    fold_shape = lambda d: jax.ShapeDtypeStruct((T // S, d, S // d, n_kv), BF16)
    d1, d2 = [d for _, d in DILATED if d > 1]
    assert d2 % d1 == 0 and (tm // ROW_SUBTILES) % (16 * d2) == 0
    weights = (wg, wu, wd, g, b, wint)
    sub = tm // ROW_SUBTILES
    return pl.pallas_call(
        _ffn_in_kernel,
        grid=(T // tm,),
        in_specs=[row(D)] + _weight_specs(weights, ffn_index),
        out_specs=[row(D), row(n_q), row(n_kv), fold(d1), fold(d2),
                   pl.BlockSpec((n_slab, tm, LANES), lambda i: (0, i, 0)),
                   row(n_ml), row(n_ml),
                   pl.BlockSpec((n_g, tm), lambda i: (0, i))],
        out_shape=[jax.ShapeDtypeStruct((T, D), F32),
                   jax.ShapeDtypeStruct((T, n_q), F32),
                   jax.ShapeDtypeStruct((T, n_kv), BF16),
                   fold_shape(d1), fold_shape(d2),
                   jax.ShapeDtypeStruct((n_slab, T, LANES), F32),
                   jax.ShapeDtypeStruct((T, n_ml), BF16),
                   jax.ShapeDtypeStruct((T, n_ml), F32),
                   jax.ShapeDtypeStruct((n_g, T), F32)],
        scratch_shapes=[pltpu.VMEM((n_kv // LANES, sub, LANES), F32)] * 2,
        compiler_params=pltpu.CompilerParams(
            dimension_semantics=("parallel",), vmem_limit_bytes=VMEM_LIMIT),
        name="ffn_in",
    )(x2d, *weights)


def _attn_block(q_ref, kv_refs, bias_ref, out_ref, acc_s, m_s, l_s, j, step):
    di = ATT_ORDER[step]
    dil = DILATED[di][1]
    k_ref, v_ref = kv_refs[di]
    lane = lax.broadcasted_iota(jnp.int32, (BLK, LANES), 1)
    head0 = lane < ATT_HD
    r = j % dil
    n = j // dil

    def rows_of(start, stride):
        if stride == 1:
            return pl.ds(pl.multiple_of(start, BLK), BLK)
        return pl.ds(start, BLK, stride=stride)

    cur = rows_of(n * BLK, 1)
    prev = rows_of(jnp.maximum(n - 1, 0) * BLK, 1)
    if dil == 1:
        take = lambda ref, rows: ref[rows, :]
    else:
        take = lambda ref, rows: ref[r, rows, :]
    rows = rows_of(r + n * (BLK * dil), dil)
    q = q_ref[rows, :]
    kk = jnp.concatenate([take(k_ref, prev), take(k_ref, cur)], axis=0)
    vv = jnp.concatenate([take(v_ref, prev), take(v_ref, cur)], axis=0)
    first = jnp.where(n == 0, 1, 0)
    ms, ls, os_ = [], [], []
    for h in range(2):
        sel = head0 if h == 0 else jnp.logical_not(head0)
        qh = jnp.where(sel, q, 0.0).astype(BF16)
        lg = lax.dot_general(qh, kk, (((1,), (1,)), ((), ())), preferred_element_type=F32)
        lg = lg + bias_ref[first, di, h]
        mh = jnp.max(lg, -1, keepdims=True)
        p = jnp.exp2(lg - mh)
        ls.append(jnp.sum(p, -1, keepdims=True))
        ms.append(mh)
        os_.append(jnp.dot(p.astype(BF16), vv, preferred_element_type=F32))
    m_b = jnp.where(head0, ms[0], ms[1])
    l_b = jnp.where(head0, ls[0], ls[1])
    o_b = jnp.where(head0, os_[0], os_[1])
    if step > 0:
        m_o = m_s[rows, :]
        m_n = jnp.maximum(m_o, m_b)
        e_o = jnp.exp2(m_o - m_n)
        e_b = jnp.exp2(m_b - m_n)
        l_b = l_s[rows, :] * e_o + l_b * e_b
        o_b = acc_s[rows, :] * e_o + o_b * e_b
        m_b = m_n
    if step < len(ATT_ORDER) - 1:
        m_s[rows, :] = m_b
        l_s[rows, :] = l_b
        acc_s[rows, :] = o_b
    else:
        out_ref[rows, :] = (o_b / l_b).astype(out_ref.dtype)


def _attn_kernel(q_ref, *refs):
    n_br = len(DILATED)
    kv_refs = [(refs[2 * i], refs[2 * i + 1]) for i in range(n_br)]
    bias_ref, out_ref, acc_s, m_s, l_s = refs[2 * n_br:]
    n_blocks = q_ref.shape[0] // BLK
    for step in range(len(ATT_ORDER)):
        def body(i, _, step=step):
            for u in range(ATT_UNROLL):
                _attn_block(q_ref, kv_refs, bias_ref, out_ref, acc_s, m_s, l_s,
                            i * ATT_UNROLL + u, step)
            return 0
        lax.fori_loop(0, n_blocks // ATT_UNROLL, body, 0)


def _attention(q, kvs, bias, B, S):
    n_pairs = ATT_HEADS * ATT_HD // LANES
    dils = [d for _, d in DILATED]
    assert dils[ATT_ORDER[-1]] == 1
    in_specs = [pl.BlockSpec((None, S, LANES), lambda b, p: (b, 0, p))]
    operands = [q.reshape(B, S, q.shape[-1])]
    for d, kv in zip(dils, kvs):
        for off in (0, n_pairs):
            if d == 1:
                in_specs.append(pl.BlockSpec((None, S, LANES),
                                             lambda b, p, off=off: (b, 0, off + p)))
            else:
                in_specs.append(pl.BlockSpec((None, d, S // d, LANES),
                                             lambda b, p, off=off: (b, 0, 0, off + p)))
            operands.append(kv)
    in_specs.append(pl.BlockSpec((2, len(DILATED), 2, BLK, 2 * BLK),
                                 lambda b, p: (0, 0, p, 0, 0)))
    return pl.pallas_call(
        _attn_kernel,
        grid=(B, n_pairs),
        in_specs=in_specs,
        out_specs=pl.BlockSpec((None, S, LANES), lambda b, p: (b, 0, p)),
        out_shape=jax.ShapeDtypeStruct((B, S, n_pairs * LANES), BF16),
        scratch_shapes=[pltpu.VMEM((S, LANES), F32)] * 3,
        compiler_params=pltpu.CompilerParams(
            dimension_semantics=("parallel", "parallel"), vmem_limit_bytes=VMEM_LIMIT),
        name="attn",
    )(*operands, bias)


def _split3(a):
    hi = a.astype(BF16)
    r1 = a - hi.astype(F32)
    mid = r1.astype(BF16)
    lo = (r1 - mid.astype(F32)).astype(BF16)
    return hi, mid, lo


def _log_sigmoid(x):
    return jnp.minimum(x, 0.0) - jnp.log1p(jnp.exp(-jnp.abs(x)))


def _prefix_max_lanes(x):
    lane = lax.broadcasted_iota(jnp.int32, x.shape, 1)
    sh = 1
    while sh < x.shape[1]:
        x = jnp.maximum(x, jnp.where(lane >= sh, pltpu.roll(x, sh, axis=1), -jnp.inf))
        sh *= 2
    return x


def _gates_kernel(gt_ref, gbc_ref, tri_ref, cols_ref, rows_ref):
    H = ML_HEADS
    n_chunks = gt_ref.shape[1] // CHUNK
    n_rows = 2 * H * n_chunks
    tri = tri_ref[...]
    head_row = (lax.broadcasted_iota(jnp.int32, (n_rows, CHUNK), 0) & (2 * H - 1)) < H
    gr = jnp.concatenate([gt_ref[:, i * CHUNK:(i + 1) * CHUNK] + gbc_ref[...]
                          for i in range(n_chunks)], axis=0)
    b_all = sum(lax.dot_general(part, tri, (((1,), (1,)), ((), ())),
                                preferred_element_type=F32)
                for part in _split3(_log_sigmoid(gr)))
    b = jnp.concatenate([pltpu.roll(b_all[i * 2 * H:(i + 1) * 2 * H], H, axis=0)
                         for i in range(n_chunks)], axis=0)
    b = jnp.where(head_row, b, 0.0)
    u = jnp.where(head_row, gr - b, 0.0)
    cm = _prefix_max_lanes(u)
    u_max = jnp.broadcast_to(cm[:, CHUNK - 1:CHUNK], cm.shape)
    g_tot = jnp.broadcast_to(b[:, CHUNK - 1:CHUNK], b.shape)
    m_prev = jnp.zeros((2 * H, CHUNK), F32)
    m_prevs = []
    for i in range(n_chunks):
        rs = slice(i * 2 * H, (i + 1) * 2 * H)
        m_prevs.append(m_prev)
        m_prev = g_tot[rs] + jnp.maximum(m_prev, u_max[rs])
    m_prev = jnp.concatenate(m_prevs, axis=0)
    mm = jnp.maximum(m_prev, u_max)
    sp = jnp.exp(m_prev - mm)
    sl = jnp.exp(u_max - mm)
    wa = jnp.exp(u - u_max)
    m_in = jnp.maximum(m_prev, cm)
    e_w = jnp.exp(m_prev - m_in)
    e_m = jnp.exp(-(b + m_in))
    per = 2 * H * ML_CHUNKS_PER_STEP
    assert 4 * per == LANES
    pad = jnp.zeros((per, CHUNK), F32)
    for s in range(n_rows // per):
        rs = slice(s * per, (s + 1) * per)
        cols_ref[s] = jnp.concatenate([m_in[rs], e_w[rs], e_m[rs], pad], axis=0).T
        rows_ref[s] = jnp.concatenate([u[rs], sp[rs], sl[rs], wa[rs]], axis=0)


def _gates(gatest, gb_col, B, S):
    G = gatest.shape[0]
    n_steps = S // (ML_CHUNKS_PER_STEP * CHUNK)
    tri = jnp.tril(jnp.ones((CHUNK, CHUNK), F32)).astype(BF16)
    fac = pl.BlockSpec((None, n_steps, LANES, CHUNK), lambda b: (b, 0, 0, 0))
    return pl.pallas_call(
        _gates_kernel,
        grid=(B,),
        in_specs=[pl.BlockSpec((G, S), lambda b: (0, b)),
                  _const_spec(gb_col.shape), _const_spec(tri.shape)],
        out_specs=[fac, fac],
        out_shape=[jax.ShapeDtypeStruct((B, n_steps, LANES, CHUNK), F32)] * 2,
        compiler_params=pltpu.CompilerParams(
            dimension_semantics=("parallel",), vmem_limit_bytes=VMEM_LIMIT),
        name="gates",
    )(gatest, gb_col, tri)


def _mlstm_kernel(qk_ref, v_ref, og_ref, cols_ref, rows_ref, cw_ref, cb_ref, mlg_ref,
                  out_ref, xbuf, ybuf, ct_s, nb_s):
    step = pl.program_id(1)
    E = LANES
    H = ML_HEADS
    n_slab, rows = qk_ref.shape[0], qk_ref.shape[1]
    half = rows // 2
    n_chunks = rows // CHUNK
    assert n_slab == 2 * H

    @pl.when(step == 0)
    def _():
        xbuf[:, 0:8, :] = jnp.zeros((n_slab, 8, LANES), F32)
        ct_s[...] = jnp.zeros(ct_s.shape, F32)
        nb_s[...] = jnp.zeros(nb_s.shape, F32)

    @pl.when(step > 0)
    def _():
        xbuf[:, 0:8, :] = xbuf[:, rows:rows + 8, :]

    xbuf[:, 8:8 + rows, :] = qk_ref[...]
    for c in range(n_slab):
        cl = slice(c * LANES, (c + 1) * LANES)
        for par in range(2):
            acc = cb_ref[:, cl]
            for j in range(CONV_K):
                off = 8 - (CONV_K - 1) + j + par
                acc = acc + xbuf[c, pl.ds(off, half, stride=2), :] * cw_ref[j:j + 1, cl]
            act = _silu(acc)
            if c >= H:
                act = act * (E ** -0.5)
            ybuf[c, pl.ds(par, half, stride=2), :] = act

    n_rows = 2 * H * n_chunks
    cols = cols_ref[...]
    u = rows_ref[0:n_rows, :]
    sp_all = rows_ref[n_rows:2 * n_rows, :]
    sl_all = rows_ref[2 * n_rows:3 * n_rows, :]
    wa_all = rows_ref[3 * n_rows:4 * n_rows, :]

    ti = lax.broadcasted_iota(jnp.int32, (CHUNK, CHUNK), 0)
    si = lax.broadcasted_iota(jnp.int32, (CHUNK, CHUNK), 1)
    causal = si <= ti
    ones = jnp.ones((CHUNK, E), BF16)

    cts = [ct_s[h] for h in range(H)]
    nbs = [nb_s[h] for h in range(H)]
    for i in range(n_chunks):
        r0 = i * CHUNK
        for h in range(H):
            row = i * 2 * H + h
            col = lambda q_, row=row: cols[:, q_ * n_rows + row:q_ * n_rows + row + 1]
            m_in_c, e_w, e_m = col(0), col(1), col(2)
            q = ybuf[h, r0:r0 + CHUNK, :]
            k = ybuf[H + h, r0:r0 + CHUNK, :]
            v1 = jnp.concatenate([v_ref[r0:r0 + CHUNK, h * E:(h + 1) * E], ones], axis=1)
            qb = q.astype(BF16)

            d_w = jnp.where(causal, jnp.exp(u[row:row + 1, :] - m_in_c), 0.0)
            s_qk = lax.dot_general(qb, k.astype(BF16), (((1,), (1,)), ((), ())),
                                   preferred_element_type=F32) * d_w
            st = jnp.concatenate([cts[h].astype(BF16), nbs[h].astype(BF16)], axis=1)
            inter = jnp.dot(qb, st, preferred_element_type=F32)
            intra = jnp.dot(s_qk.astype(BF16), v1, preferred_element_type=F32)
            num = e_w * inter[:, :E] + intra[:, :E]
            den = e_w * inter[:, E:] + intra[:, E:]
            hh = num / jnp.maximum(jnp.abs(den), e_m)

            hg = og_ref[r0:r0 + CHUNK, h * E:(h + 1) * E] * hh
            mu = jnp.mean(hg, -1, keepdims=True)
            ex2 = jnp.mean(hg * hg, -1, keepdims=True)
            hc = hg - mu
            var = jnp.maximum(ex2 - mu * mu, 0.0)
            yn = hc * lax.rsqrt(var + LN_EPS) * mlg_ref[:, h * E:(h + 1) * E]
            out_ref[r0:r0 + CHUNK, h * E:(h + 1) * E] = yn.astype(out_ref.dtype)

            kw = (k.T * wa_all[row:row + 1, :]).astype(BF16)
            loc = jnp.dot(kw, v1, preferred_element_type=F32)
            sp_h, sl_h = sp_all[row:row + 1, :], sl_all[row:row + 1, :]
            cts[h] = sp_h * cts[h] + sl_h * loc[:, :E]
            nbs[h] = sp_h * nbs[h] + sl_h * loc[:, E:]
    for h in range(H):
        ct_s[h] = cts[h]
        nb_s[h] = nbs[h]


def _mlstm(qk, vm, og, cols, facs, conv_w, conv_b, ml_g, B, S):
    rows = ML_CHUNKS_PER_STEP * CHUNK
    ns = S // rows
    W = vm.shape[1]
    row = lambda w: pl.BlockSpec((rows, w), lambda b, c: (b * ns + c, 0))
    fac = pl.BlockSpec((None, None, LANES, CHUNK), lambda b, c: (b, c, 0, 0))
    consts = (conv_w, conv_b, ml_g)
    return pl.pallas_call(
        _mlstm_kernel,
        grid=(B, ns),
        in_specs=[pl.BlockSpec((qk.shape[0], rows, LANES), lambda b, c: (0, b * ns + c, 0)),
                  row(W), row(W), fac, fac]
                 + [_const_spec(a.shape) for a in consts],
        out_specs=row(W),
        out_shape=jax.ShapeDtypeStruct((B * S, W), BF16),
        scratch_shapes=[pltpu.VMEM((qk.shape[0], rows + 8, LANES), F32),
                        pltpu.VMEM((qk.shape[0], rows, LANES), F32),
                        pltpu.VMEM((ML_HEADS, LANES, LANES), F32),
                        pltpu.VMEM((ML_HEADS, LANES, LANES), F32)],
        compiler_params=pltpu.CompilerParams(
            dimension_semantics=("parallel", "arbitrary"), vmem_limit_bytes=VMEM_LIMIT),
        name="mlstm",
    )(qk, vm, og, cols, facs, *consts)


def _tail_kernel(x1_ref, att_ref, hm_ref, kt_ref, v_ref, wout_ref, wq_ref, wo_ref,
                 wg_ref, wu_ref, wd_ref, g_ref, b_ref, out_ref):
    sub = x1_ref.shape[0] // ROW_SUBTILES
    tiles = [slice(t * sub, (t + 1) * sub) for t in range(ROW_SUBTILES)]
    n_a = att_ref.shape[1]
    hd = wq_ref.shape[1] // XA_HEADS

    def mix(rs):
        return (jnp.dot(att_ref[rs, :], wout_ref[0:n_a, :], preferred_element_type=F32)
                + jnp.dot(hm_ref[rs, :], wout_ref[n_a:, :], preferred_element_type=F32))

    def cross(x2):
        q = jnp.dot(x2.astype(BF16), wq_ref[...], preferred_element_type=F32)
        heads = []
        for h in range(XA_HEADS):
            qh = q[:, h * hd:(h + 1) * hd].astype(BF16)
            lg = jnp.dot(qh, kt_ref[h * hd:(h + 1) * hd, :], preferred_element_type=F32)
            p = jnp.exp(lg - jnp.max(lg, -1, keepdims=True))
            s = jnp.sum(p, -1, keepdims=True)
            o = jnp.dot(p.astype(BF16), v_ref[:, h * hd:(h + 1) * hd],
                        preferred_element_type=F32)
            heads.append((o / s).astype(BF16))
        return jnp.dot(jnp.concatenate(heads, axis=1), wo_ref[...], preferred_element_type=F32)

    mixes = [mix(rs) for rs in tiles]
    x2s = [_layer_norm(ALPHA * x1_ref[rs, :] + m, g_ref[0:1, :], b_ref[0:1, :])
           for rs, m in zip(tiles, mixes)]
    xas = [cross(x2) for x2 in x2s]
    x3s = [_layer_norm(ALPHA * x2 + xa, g_ref[1:2, :], b_ref[1:2, :])
           for x2, xa in zip(x2s, xas)]
    ffs = [_swiglu(x3.astype(BF16), wg_ref, wu_ref, wd_ref) for x3 in x3s]
    for rs, x3, ff in zip(tiles, x3s, ffs):
        out_ref[rs, :] = _layer_norm(ALPHA * x3 + 0.5 * ff, g_ref[2:3, :], b_ref[2:3, :])


def _tail(x1, att, hm, kt, v, wout, wq, wo, wg, wu, wd, ffn_index, g, b, S):
    T, D = x1.shape
    tm = ROW_TILE
    per_batch = S // tm
    row = lambda w: pl.BlockSpec((tm, w), lambda i: (i, 0))
    L = v.shape[1]
    weights = (wout, wq, wo, wg, wu, wd, g, b)
    return pl.pallas_call(
        _tail_kernel,
        grid=(T // tm,),
        in_specs=[row(D), row(att.shape[1]), row(hm.shape[1]),
                  pl.BlockSpec((None, D, L), lambda i: (i // per_batch, 0, 0)),
                  pl.BlockSpec((None, L, D), lambda i: (i // per_batch, 0, 0))]
                 + _weight_specs(weights, ffn_index),
        out_specs=row(D),
        out_shape=jax.ShapeDtypeStruct((T, D), F32),
        compiler_params=pltpu.CompilerParams(
            dimension_semantics=("parallel",), vmem_limit_bytes=VMEM_LIMIT),
        name="tail",
    )(x1, att, hm, kt, v, *weights)


def kernel(x, mem, rel_bias, ln_g, ln_b, ffn_w_gate, ffn_w_up, ffn_w_down, w_in, conv_w, conv_b,
           ig_bias, fg_bias, ml_norm_g, w_out, xq_w, xkv_w, xo_w):
    B, S, D = x.shape
    att_w = ATT_HEADS * ATT_HD
    ml_w = ML_HEADS * LANES
    bias = _bias_tables(rel_bias)
    xf = x.reshape(B * S, D)
    bf = lambda a: a.astype(BF16)
    wg_all, wu_all, wd_all = bf(ffn_w_gate), bf(ffn_w_up), bf(ffn_w_down)
    for l in range(DEPTH):
        wint = bf(w_in[l].T)
        gb = jnp.concatenate([ig_bias[l], fg_bias[l]]).astype(F32)

        x1, qa, kv1, kvf1, kvf2, qk, vm, og, gatest = _ffn_in(
            xf, wg_all, wu_all, wd_all, (l, 0),
            ln_g[l, 0][None], ln_b[l, 0][None], wint, 3 * att_w, 2 * ml_w, ml_w, S)
        att = _attention(qa, [kv1.reshape(B, S, -1), kvf1, kvf2], bias, B, S)
        att = att.reshape(B * S, att_w)
        cols, facs = _gates(gatest, gb[:, None], B, S)
        hm = _mlstm(qk, vm, og, cols, facs, conv_w[l], conv_b[l][None], ml_norm_g[l][None],
                    B, S)

        hd = D // XA_HEADS
        kt, v = _memkv(mem, bf(xkv_w[l]))
        xf = _tail(x1, att, hm, kt, v, bf(w_out[l]), bf(xq_w[l] * hd ** -0.5), bf(xo_w[l]),
                   wg_all, wu_all, wd_all, (l, 1), ln_g[l, 1:4], ln_b[l, 1:4], S)
    return xf.reshape(B, S, D)
```

```python
import functools
import math

import jax
import jax.numpy as jnp
from jax import lax
from jax.experimental import pallas as pl
from jax.experimental.pallas import tpu as pltpu

F32 = jnp.float32
BF16 = jnp.bfloat16

ATT_HD = 64
ATT_HEADS = 8
DILATED = ((128, 1), (512, 4), (2048, 16))
BLK = 128
ML_HEADS = 4
CHUNK = 128
CONV_K = 4
XA_HEADS = 4
REL_BUCKETS = 32
REL_MAX_DIST = 2048
DEPTH = 1
ALPHA = (2 * DEPTH) ** 0.25
LN_EPS = 1e-5
NEG = -1e30
LOG2E = math.log2(math.e)

LANES = 128
VMEM_LIMIT = 60 * 1024 * 1024
ROW_TILE = 512
ROW_SUBTILES = 2
ATT_ORDER = (2, 1, 0)
ATT_UNROLL = 32
ML_CHUNKS_PER_STEP = 4


def _const_spec(shape, lead=()):
    block = (None,) * len(lead) + tuple(shape[len(lead):])
    index = tuple(lead) + (0,) * (len(shape) - len(lead))
    return pl.BlockSpec(block, lambda *_: index, pipeline_mode=pl.Buffered(1))


def _weight_specs(weights, ffn_index):
    return [_const_spec(w.shape, ffn_index if w.ndim == 4 else ()) for w in weights]


def _layer_norm(y, g, b):
    mu = jnp.mean(y, -1, keepdims=True)
    yc = y - mu
    var = jnp.mean(yc * yc, -1, keepdims=True)
    return yc * lax.rsqrt(var + LN_EPS) * g + b


def _silu(x):
    return x * jax.nn.sigmoid(x)


def _swiglu(xb, wg_ref, wu_ref, wd_ref):
    g = jnp.dot(xb, wg_ref[...], preferred_element_type=F32)
    u = jnp.dot(xb, wu_ref[...], preferred_element_type=F32)
    h = (_silu(g) * u).astype(BF16)
    return jnp.dot(h, wd_ref[...], preferred_element_type=F32)


def _bias_kernel(rel_ref, bkt_ref, out_ref):
    bkt = bkt_ref[0]
    prev_half = lax.broadcasted_iota(jnp.int32, bkt.shape, 1) < BLK
    accs = [jnp.full(bkt.shape, NEG, F32) for _ in range(ATT_HEADS)]
    for b in range(REL_BUCKETS):
        hit = bkt == b
        for h in range(ATT_HEADS):
            accs[h] = jnp.where(hit, rel_ref[b, h] * LOG2E, accs[h])
    for h in range(ATT_HEADS):
        out_ref[0, 0, h] = accs[h]
        out_ref[1, 0, h] = jnp.where(prev_half, NEG, accs[h])


def _bias_tables(rel_bias):
    qi = jnp.arange(BLK)[:, None]
    ki = jnp.arange(2 * BLK)[None, :]
    off = qi + BLK - ki
    exact = REL_BUCKETS // 2
    n_log = REL_BUCKETS - exact
    starts = [math.ceil(exact * (REL_MAX_DIST / exact) ** (k / n_log)) for k in range(1, n_log)]
    tabs = []
    for window, dil in DILATED:
        n_keys = window // dil
        dist = dil * jnp.clip(off, 0, n_keys)
        large = exact + sum((dist >= s).astype(jnp.int32) for s in starts)
        bucket = jnp.where(dist < exact, dist, large)
        band = (off >= 0) & (off <= n_keys)
        tabs.append(jnp.where(band, bucket, -1))
    bkt = jnp.stack(tabs, 0).astype(jnp.int32)
    nd = len(DILATED)
    return pl.pallas_call(
        _bias_kernel,
        grid=(nd,),
        in_specs=[pl.BlockSpec(memory_space=pltpu.SMEM),
                  pl.BlockSpec((1, BLK, 2 * BLK), lambda d: (d, 0, 0))],
        out_specs=pl.BlockSpec((2, 1, ATT_HEADS, BLK, 2 * BLK), lambda d: (0, d, 0, 0, 0)),
        out_shape=jax.ShapeDtypeStruct((2, nd, ATT_HEADS, BLK, 2 * BLK), F32),
        name="bias",
    )(rel_bias.astype(F32), bkt)


def _memkv_kernel(mem_ref, wkv_ref, kt_ref, v_ref):
    D = mem_ref.shape[1]
    mb = mem_ref[...].astype(BF16)
    k = jnp.dot(mb, wkv_ref[:, 0:D], preferred_element_type=F32)
    kt_ref[...] = k.T.astype(BF16)
    v_ref[...] = jnp.dot(mb, wkv_ref[:, D:], preferred_element_type=F32).astype(BF16)


def _memkv(mem, wkv):
    B, L, D = mem.shape
    return pl.pallas_call(
        _memkv_kernel,
        grid=(B,),
        in_specs=[pl.BlockSpec((None, L, D), lambda b: (b, 0, 0)),
                  _const_spec((D, 2 * D))],
        out_specs=[pl.BlockSpec((None, D, L), lambda b: (b, 0, 0)),
                   pl.BlockSpec((None, L, D), lambda b: (b, 0, 0))],
        out_shape=[jax.ShapeDtypeStruct((B, D, L), BF16),
                   jax.ShapeDtypeStruct((B, L, D), BF16)],
        compiler_params=pltpu.CompilerParams(vmem_limit_bytes=VMEM_LIMIT),
        name="memkv",
    )(mem, wkv)


def _fold_rows(kv, t, s1, s2, fold_refs):
    (d1, ref1), (d2, ref2) = fold_refs
    ratio = d2 // d1
    sub = kv.shape[0]
    n1, n2 = sub // d1, sub // d2
    for c in range(kv.shape[1] // LANES):
        cl = slice(c * LANES, (c + 1) * LANES)
        s1[c] = kv[:, cl]
        for r1 in range(d1):
            p1 = s1[c, pl.ds(r1, n1, stride=d1), :]
            ref1[r1, t * n1:(t + 1) * n1, cl] = p1.astype(BF16)
            s2[c, r1 * n1:(r1 + 1) * n1, :] = p1
        for r1 in range(d1):
            for rp in range(ratio):
                p2 = s2[c, pl.ds(r1 * n1 + rp, n2, stride=ratio), :]
                ref2[d1 * rp + r1, t * n2:(t + 1) * n2, cl] = p2.astype(BF16)


def _ffn_in_kernel(x_ref, wg_ref, wu_ref, wd_ref, g_ref, b_ref, wint_ref,
                   x1_ref, qa_ref, kv1_ref, kvf1_ref, kvf2_ref, qk_ref, vm_ref, og_ref,
                   gatest_ref, s1, s2):
    n_q = qa_ref.shape[1]
    c1 = n_q + kv1_ref.shape[1]
    c2 = c1 + qk_ref.shape[0] * LANES
    c3 = c2 + vm_ref.shape[1]
    c4 = c3 + og_ref.shape[1]
    nt = (((1,), (1,)), ((), ()))
    sub = x_ref.shape[0] // ROW_SUBTILES
    tiles = [slice(t * sub, (t + 1) * sub) for t in range(ROW_SUBTILES)]
    ffs = [_swiglu(x_ref[rs, :].astype(BF16), wg_ref, wu_ref, wd_ref) for rs in tiles]
    fold_dils = [d for _, d in DILATED if d > 1]
    for t, (rs, ff) in enumerate(zip(tiles, ffs)):
        x1 = _layer_norm(ALPHA * x_ref[rs, :] + 0.5 * ff, g_ref[...], b_ref[...])
        x1_ref[rs, :] = x1
        xb = x1.astype(BF16)
        proj = lambda lo, hi, xb=xb: lax.dot_general(xb, wint_ref[lo:hi, :], nt,
                                                     preferred_element_type=F32)

        qkm = proj(c1, c2)
        for c in range(qk_ref.shape[0]):
            qk_ref[c, rs, :] = qkm[:, c * LANES:(c + 1) * LANES]

        qa_ref[rs, :] = proj(0, n_q) * (ATT_HD ** -0.5 * LOG2E)
        kv = proj(n_q, c1)
        kv1_ref[rs, :] = kv.astype(BF16)
        _fold_rows(kv, t, s1, s2, list(zip(fold_dils, (kvf1_ref, kvf2_ref))))
        vm_ref[rs, :] = proj(c2, c3).astype(BF16)
        og_ref[rs, :] = jax.nn.sigmoid(proj(c3, c4))
        gatest_ref[:, rs] = lax.dot_general(wint_ref[c4:c4 + gatest_ref.shape[0], :], xb, nt,
                                            preferred_element_type=F32)


def _ffn_in(x2d, wg, wu, wd, ffn_index, g, b, wint, n_att, n_qkm, n_ml, S):
    T, D = x2d.shape
    tm = ROW_TILE
    tps = S // tm
    row = lambda w: pl.BlockSpec((tm, w), lambda i: (i, 0))
    n_g = wint.shape[0] - n_att - n_qkm - 2 * n_ml
    n_slab = n_qkm // LANES
    n_q, n_kv = n_att // 3, 2 * n_att // 3
    fold = lambda d: pl.BlockSpec((None, d, tm // d, n_kv), lambda i: (i // tps, 0, i % tps, 0))
    fold_shape = lambda d: jax.ShapeDtypeStruct((T // S, d, S // d, n_kv), BF16)
    d1, d2 = [d for _, d in DILATED if d > 1]
    assert d2 % d1 == 0 and (tm // ROW_SUBTILES) % (16 * d2) == 0
    weights = (wg, wu, wd, g, b, wint)
    sub = tm // ROW_SUBTILES
    return pl.pallas_call(
        _ffn_in_kernel,
        grid=(T // tm,),
        in_specs=[row(D)] + _weight_specs(weights, ffn_index),
        out_specs=[row(D), row(n_q), row(n_kv), fold(d1), fold(d2),
                   pl.BlockSpec((n_slab, tm, LANES), lambda i: (0, i, 0)),
                   row(n_ml), row(n_ml),
                   pl.BlockSpec((n_g, tm), lambda i: (0, i))],
        out_shape=[jax.ShapeDtypeStruct((T, D), F32),
                   jax.ShapeDtypeStruct((T, n_q), F32),
                   jax.ShapeDtypeStruct((T, n_kv), BF16),
                   fold_shape(d1), fold_shape(d2),
                   jax.ShapeDtypeStruct((n_slab, T, LANES), F32),
                   jax.ShapeDtypeStruct((T, n_ml), BF16),
                   jax.ShapeDtypeStruct((T, n_ml), F32),
                   jax.ShapeDtypeStruct((n_g, T), F32)],
        scratch_shapes=[pltpu.VMEM((n_kv // LANES, sub, LANES), F32)] * 2,
        compiler_params=pltpu.CompilerParams(
            dimension_semantics=("parallel",), vmem_limit_bytes=VMEM_LIMIT),
        name="ffn_in",
    )(x2d, *weights)


def _attn_block(q_ref, kv_refs, bias_ref, out_ref, acc_s, m_s, l_s, j, step):
    di = ATT_ORDER[step]
    dil = DILATED[di][1]
    k_ref, v_ref = kv_refs[di]
    lane = lax.broadcasted_iota(jnp.int32, (BLK, LANES), 1)
    head0 = lane < ATT_HD
    r = j % dil
    n = j // dil

    def rows_of(start, stride):
        if stride == 1:
            return pl.ds(pl.multiple_of(start, BLK), BLK)
        return pl.ds(start, BLK, stride=stride)

    cur = rows_of(n * BLK, 1)
    prev = rows_of(jnp.maximum(n - 1, 0) * BLK, 1)
    if dil == 1:
        take = lambda ref, rows: ref[rows, :]
    else:
        take = lambda ref, rows: ref[r, rows, :]
    rows = rows_of(r + n * (BLK * dil), dil)
    q = q_ref[rows, :]
    kk = jnp.concatenate([take(k_ref, prev), take(k_ref, cur)], axis=0)
    vv = jnp.concatenate([take(v_ref, prev), take(v_ref, cur)], axis=0)
    first = jnp.where(n == 0, 1, 0)
    ms, ls, os_ = [], [], []
    for h in range(2):
        sel = head0 if h == 0 else jnp.logical_not(head0)
        qh = jnp.where(sel, q, 0.0).astype(BF16)
        lg = lax.dot_general(qh, kk, (((1,), (1,)), ((), ())), preferred_element_type=F32)
        lg = lg + bias_ref[first, di, h]
        mh = jnp.max(lg, -1, keepdims=True)
        p = jnp.exp2(lg - mh)
        ls.append(jnp.sum(p, -1, keepdims=True))
        ms.append(mh)
        os_.append(jnp.dot(p.astype(BF16), vv, preferred_element_type=F32))
    m_b = jnp.where(head0, ms[0], ms[1])
    l_b = jnp.where(head0, ls[0], ls[1])
    o_b = jnp.where(head0, os_[0], os_[1])
    if step > 0:
        m_o = m_s[rows, :]
        m_n = jnp.maximum(m_o, m_b)
        e_o = jnp.exp2(m_o - m_n)
        e_b = jnp.exp2(m_b - m_n)
        l_b = l_s[rows, :] * e_o + l_b * e_b
        o_b = acc_s[rows, :] * e_o + o_b * e_b
        m_b = m_n
    if step < len(ATT_ORDER) - 1:
        m_s[rows, :] = m_b
        l_s[rows, :] = l_b
        acc_s[rows, :] = o_b
    else:
        out_ref[rows, :] = (o_b / l_b).astype(out_ref.dtype)


def _attn_kernel(q_ref, *refs):
    n_br = len(DILATED)
    kv_refs = [(refs[2 * i], refs[2 * i + 1]) for i in range(n_br)]
    bias_ref, out_ref, acc_s, m_s, l_s = refs[2 * n_br:]
    n_blocks = q_ref.shape[0] // BLK
    for step in range(len(ATT_ORDER)):
        def body(i, _, step=step):
            for u in range(ATT_UNROLL):
                _attn_block(q_ref, kv_refs, bias_ref, out_ref, acc_s, m_s, l_s,
                            i * ATT_UNROLL + u, step)
            return 0
        lax.fori_loop(0, n_blocks // ATT_UNROLL, body, 0)


def _attention(q, kvs, bias, B, S):
    n_pairs = ATT_HEADS * ATT_HD // LANES
    dils = [d for _, d in DILATED]
    assert dils[ATT_ORDER[-1]] == 1
    in_specs = [pl.BlockSpec((None, S, LANES), lambda b, p: (b, 0, p))]
    operands = [q.reshape(B, S, q.shape[-1])]
    for d, kv in zip(dils, kvs):
        for off in (0, n_pairs):
            if d == 1:
                in_specs.append(pl.BlockSpec((None, S, LANES),
                                             lambda b, p, off=off: (b, 0, off + p)))
            else:
                in_specs.append(pl.BlockSpec((None, d, S // d, LANES),
                                             lambda b, p, off=off: (b, 0, 0, off + p)))
            operands.append(kv)
    in_specs.append(pl.BlockSpec((2, len(DILATED), 2, BLK, 2 * BLK),
                                 lambda b, p: (0, 0, p, 0, 0)))
    return pl.pallas_call(
        _attn_kernel,
        grid=(B, n_pairs),
        in_specs=in_specs,
        out_specs=pl.BlockSpec((None, S, LANES), lambda b, p: (b, 0, p)),
        out_shape=jax.ShapeDtypeStruct((B, S, n_pairs * LANES), BF16),
        scratch_shapes=[pltpu.VMEM((S, LANES), F32)] * 3,
        compiler_params=pltpu.CompilerParams(
            dimension_semantics=("parallel", "parallel"), vmem_limit_bytes=VMEM_LIMIT),
        name="attn",
    )(*operands, bias)


def _split3(a):
    hi = a.astype(BF16)
    r1 = a - hi.astype(F32)
    mid = r1.astype(BF16)
    lo = (r1 - mid.astype(F32)).astype(BF16)
    return hi, mid, lo


def _log_sigmoid(x):
    return jnp.minimum(x, 0.0) - jnp.log1p(jnp.exp(-jnp.abs(x)))


def _prefix_max_lanes(x):
    lane = lax.broadcasted_iota(jnp.int32, x.shape, 1)
    sh = 1
    while sh < x.shape[1]:
        x = jnp.maximum(x, jnp.where(lane >= sh, pltpu.roll(x, sh, axis=1), -jnp.inf))
        sh *= 2
    return x


def _gates_kernel(gt_ref, gbc_ref, tri_ref, cols_ref, rows_ref):
    H = ML_HEADS
    n_chunks = gt_ref.shape[1] // CHUNK
    n_rows = 2 * H * n_chunks
    tri = tri_ref[...]
    head_row = (lax.broadcasted_iota(jnp.int32, (n_rows, CHUNK), 0) & (2 * H - 1)) < H
    gr = jnp.concatenate([gt_ref[:, i * CHUNK:(i + 1) * CHUNK] + gbc_ref[...]
                          for i in range(n_chunks)], axis=0)
    b_all = sum(lax.dot_general(part, tri, (((1,), (1,)), ((), ())),
                                preferred_element_type=F32)
                for part in _split3(_log_sigmoid(gr)))
    b = jnp.concatenate([pltpu.roll(b_all[i * 2 * H:(i + 1) * 2 * H], H, axis=0)
                         for i in range(n_chunks)], axis=0)
    b = jnp.where(head_row, b, 0.0)
    u = jnp.where(head_row, gr - b, 0.0)
    cm = _prefix_max_lanes(u)
    u_max = jnp.broadcast_to(cm[:, CHUNK - 1:CHUNK], cm.shape)
    g_tot = jnp.broadcast_to(b[:, CHUNK - 1:CHUNK], b.shape)
    m_prev = jnp.zeros((2 * H, CHUNK), F32)
    m_prevs = []
    for i in range(n_chunks):
        rs = slice(i * 2 * H, (i + 1) * 2 * H)
        m_prevs.append(m_prev)
        m_prev = g_tot[rs] + jnp.maximum(m_prev, u_max[rs])
    m_prev = jnp.concatenate(m_prevs, axis=0)
    mm = jnp.maximum(m_prev, u_max)
    sp = jnp.exp(m_prev - mm)
    sl = jnp.exp(u_max - mm)
    wa = jnp.exp(u - u_max)
    m_in = jnp.maximum(m_prev, cm)
    e_w = jnp.exp(m_prev - m_in)
    e_m = jnp.exp(-(b + m_in))
    per = 2 * H * ML_CHUNKS_PER_STEP
    assert 4 * per == LANES
    pad = jnp.zeros((per, CHUNK), F32)
    for s in range(n_rows // per):
        rs = slice(s * per, (s + 1) * per)
        cols_ref[s] = jnp.concatenate([m_in[rs], e_w[rs], e_m[rs], pad], axis=0).T
        rows_ref[s] = jnp.concatenate([u[rs], sp[rs], sl[rs], wa[rs]], axis=0)


def _gates(gatest, gb_col, B, S):
    G = gatest.shape[0]
    n_steps = S // (ML_CHUNKS_PER_STEP * CHUNK)
    tri = jnp.tril(jnp.ones((CHUNK, CHUNK), F32)).astype(BF16)
    fac = pl.BlockSpec((None, n_steps, LANES, CHUNK), lambda b: (b, 0, 0, 0))
    return pl.pallas_call(
        _gates_kernel,
        grid=(B,),
        in_specs=[pl.BlockSpec((G, S), lambda b: (0, b)),
                  _const_spec(gb_col.shape), _const_spec(tri.shape)],
        out_specs=[fac, fac],
        out_shape=[jax.ShapeDtypeStruct((B, n_steps, LANES, CHUNK), F32)] * 2,
        compiler_params=pltpu.CompilerParams(
            dimension_semantics=("parallel",), vmem_limit_bytes=VMEM_LIMIT),
        name="gates",
    )(gatest, gb_col, tri)


def _mlstm_kernel(qk_ref, v_ref, og_ref, cols_ref, rows_ref, cw_ref, cb_ref, mlg_ref,
                  out_ref, xbuf, ybuf, ct_s, nb_s):
    step = pl.program_id(1)
    E = LANES
    H = ML_HEADS
    n_slab, rows = qk_ref.shape[0], qk_ref.shape[1]
    half = rows // 2
    n_chunks = rows // CHUNK
    assert n_slab == 2 * H

    @pl.when(step == 0)
    def _():
        xbuf[:, 0:8, :] = jnp.zeros((n_slab, 8, LANES), F32)
        ct_s[...] = jnp.zeros(ct_s.shape, F32)
        nb_s[...] = jnp.zeros(nb_s.shape, F32)

    @pl.when(step > 0)
    def _():
        xbuf[:, 0:8, :] = xbuf[:, rows:rows + 8, :]

    xbuf[:, 8:8 + rows, :] = qk_ref[...]
    for c in range(n_slab):
        cl = slice(c * LANES, (c + 1) * LANES)
        for par in range(2):
            acc = cb_ref[:, cl]
            for j in range(CONV_K):
                off = 8 - (CONV_K - 1) + j + par
                acc = acc + xbuf[c, pl.ds(off, half, stride=2), :] * cw_ref[j:j + 1, cl]
            act = _silu(acc)
            if c >= H:
                act = act * (E ** -0.5)
            ybuf[c, pl.ds(par, half, stride=2), :] = act

    n_rows = 2 * H * n_chunks
    cols = cols_ref[...]
    u = rows_ref[0:n_rows, :]
    sp_all = rows_ref[n_rows:2 * n_rows, :]
    sl_all = rows_ref[2 * n_rows:3 * n_rows, :]
    wa_all = rows_ref[3 * n_rows:4 * n_rows, :]

    ti = lax.broadcasted_iota(jnp.int32, (CHUNK, CHUNK), 0)
    si = lax.broadcasted_iota(jnp.int32, (CHUNK, CHUNK), 1)
    causal = si <= ti
    ones = jnp.ones((CHUNK, E), BF16)

    cts = [ct_s[h] for h in range(H)]
    nbs = [nb_s[h] for h in range(H)]
    for i in range(n_chunks):
        r0 = i * CHUNK
        for h in range(H):
            row = i * 2 * H + h
            col = lambda q_, row=row: cols[:, q_ * n_rows + row:q_ * n_rows + row + 1]
            m_in_c, e_w, e_m = col(0), col(1), col(2)
            q = ybuf[h, r0:r0 + CHUNK, :]
            k = ybuf[H + h, r0:r0 + CHUNK, :]
            v1 = jnp.concatenate([v_ref[r0:r0 + CHUNK, h * E:(h + 1) * E], ones], axis=1)
            qb = q.astype(BF16)

            d_w = jnp.where(causal, jnp.exp(u[row:row + 1, :] - m_in_c), 0.0)
            s_qk = lax.dot_general(qb, k.astype(BF16), (((1,), (1,)), ((), ())),
                                   preferred_element_type=F32) * d_w
            st = jnp.concatenate([cts[h].astype(BF16), nbs[h].astype(BF16)], axis=1)
            inter = jnp.dot(qb, st, preferred_element_type=F32)
            intra = jnp.dot(s_qk.astype(BF16), v1, preferred_element_type=F32)
            num = e_w * inter[:, :E] + intra[:, :E]
            den = e_w * inter[:, E:] + intra[:, E:]
            hh = num / jnp.maximum(jnp.abs(den), e_m)

            hg = og_ref[r0:r0 + CHUNK, h * E:(h + 1) * E] * hh
            mu = jnp.mean(hg, -1, keepdims=True)
            ex2 = jnp.mean(hg * hg, -1, keepdims=True)
            hc = hg - mu
            var = jnp.maximum(ex2 - mu * mu, 0.0)
            yn = hc * lax.rsqrt(var + LN_EPS) * mlg_ref[:, h * E:(h + 1) * E]
            out_ref[r0:r0 + CHUNK, h * E:(h + 1) * E] = yn.astype(out_ref.dtype)

            kw = (k.T * wa_all[row:row + 1, :]).astype(BF16)
            loc = jnp.dot(kw, v1, preferred_element_type=F32)
            sp_h, sl_h = sp_all[row:row + 1, :], sl_all[row:row + 1, :]
            cts[h] = sp_h * cts[h] + sl_h * loc[:, :E]
            nbs[h] = sp_h * nbs[h] + sl_h * loc[:, E:]
    for h in range(H):
        ct_s[h] = cts[h]
        nb_s[h] = nbs[h]


def _mlstm(qk, vm, og, cols, facs, conv_w, conv_b, ml_g, B, S):
    rows = ML_CHUNKS_PER_STEP * CHUNK
    ns = S // rows
    W = vm.shape[1]
    row = lambda w: pl.BlockSpec((rows, w), lambda b, c: (b * ns + c, 0))
    fac = pl.BlockSpec((None, None, LANES, CHUNK), lambda b, c: (b, c, 0, 0))
    consts = (conv_w, conv_b, ml_g)
    return pl.pallas_call(
        _mlstm_kernel,
        grid=(B, ns),
        in_specs=[pl.BlockSpec((qk.shape[0], rows, LANES), lambda b, c: (0, b * ns + c, 0)),
                  row(W), row(W), fac, fac]
                 + [_const_spec(a.shape) for a in consts],
        out_specs=row(W),
        out_shape=jax.ShapeDtypeStruct((B * S, W), BF16),
        scratch_shapes=[pltpu.VMEM((qk.shape[0], rows + 8, LANES), F32),
                        pltpu.VMEM((qk.shape[0], rows, LANES), F32),
                        pltpu.VMEM((ML_HEADS, LANES, LANES), F32),
                        pltpu.VMEM((ML_HEADS, LANES, LANES), F32)],
        compiler_params=pltpu.CompilerParams(
            dimension_semantics=("parallel", "arbitrary"), vmem_limit_bytes=VMEM_LIMIT),
        name="mlstm",
    )(qk, vm, og, cols, facs, *consts)


def _tail_kernel(x1_ref, att_ref, hm_ref, kt_ref, v_ref, wout_ref, wq_ref, wo_ref,
                 wg_ref, wu_ref, wd_ref, g_ref, b_ref, out_ref):
    sub = x1_ref.shape[0] // ROW_SUBTILES
    tiles = [slice(t * sub, (t + 1) * sub) for t in range(ROW_SUBTILES)]
    n_a = att_ref.shape[1]
    hd = wq_ref.shape[1] // XA_HEADS

    def mix(rs):
        return (jnp.dot(att_ref[rs, :], wout_ref[0:n_a, :], preferred_element_type=F32)
                + jnp.dot(hm_ref[rs, :], wout_ref[n_a:, :], preferred_element_type=F32))

    def cross(x2):
        q = jnp.dot(x2.astype(BF16), wq_ref[...], preferred_element_type=F32)
        heads = []
        for h in range(XA_HEADS):
            qh = q[:, h * hd:(h + 1) * hd].astype(BF16)
            lg = jnp.dot(qh, kt_ref[h * hd:(h + 1) * hd, :], preferred_element_type=F32)
            p = jnp.exp(lg - jnp.max(lg, -1, keepdims=True))
            s = jnp.sum(p, -1, keepdims=True)
            o = jnp.dot(p.astype(BF16), v_ref[:, h * hd:(h + 1) * hd],
                        preferred_element_type=F32)
            heads.append((o / s).astype(BF16))
        return jnp.dot(jnp.concatenate(heads, axis=1), wo_ref[...], preferred_element_type=F32)

    mixes = [mix(rs) for rs in tiles]
    x2s = [_layer_norm(ALPHA * x1_ref[rs, :] + m, g_ref[0:1, :], b_ref[0:1, :])
           for rs, m in zip(tiles, mixes)]
    xas = [cross(x2) for x2 in x2s]
    x3s = [_layer_norm(ALPHA * x2 + xa, g_ref[1:2, :], b_ref[1:2, :])
           for x2, xa in zip(x2s, xas)]
    ffs = [_swiglu(x3.astype(BF16), wg_ref, wu_ref, wd_ref) for x3 in x3s]
    for rs, x3, ff in zip(tiles, x3s, ffs):
        out_ref[rs, :] = _layer_norm(ALPHA * x3 + 0.5 * ff, g_ref[2:3, :], b_ref[2:3, :])


def _tail(x1, att, hm, kt, v, wout, wq, wo, wg, wu, wd, ffn_index, g, b, S):
    T, D = x1.shape
    tm = ROW_TILE
    per_batch = S // tm
    row = lambda w: pl.BlockSpec((tm, w), lambda i: (i, 0))
    L = v.shape[1]
    weights = (wout, wq, wo, wg, wu, wd, g, b)
    return pl.pallas_call(
        _tail_kernel,
        grid=(T // tm,),
        in_specs=[row(D), row(att.shape[1]), row(hm.shape[1]),
                  pl.BlockSpec((None, D, L), lambda i: (i // per_batch, 0, 0)),
                  pl.BlockSpec((None, L, D), lambda i: (i // per_batch, 0, 0))]
                 + _weight_specs(weights, ffn_index),
        out_specs=row(D),
        out_shape=jax.ShapeDtypeStruct((T, D), F32),
        compiler_params=pltpu.CompilerParams(
            dimension_semantics=("parallel",), vmem_limit_bytes=VMEM_LIMIT),
        name="tail",
    )(x1, att, hm, kt, v, *weights)


def kernel(x, mem, rel_bias, ln_g, ln_b, ffn_w_gate, ffn_w_up, ffn_w_down, w_in, conv_w, conv_b,
           ig_bias, fg_bias, ml_norm_g, w_out, xq_w, xkv_w, xo_w):
    B, S, D = x.shape
    att_w = ATT_HEADS * ATT_HD
    ml_w = ML_HEADS * LANES
    bias = _bias_tables(rel_bias)
    xf = x.reshape(B * S, D)
    bf = lambda a: a.astype(BF16)
    wg_all, wu_all, wd_all = bf(ffn_w_gate), bf(ffn_w_up), bf(ffn_w_down)
    for l in range(DEPTH):
        wint = bf(w_in[l].T)
        gb = jnp.concatenate([ig_bias[l], fg_bias[l]]).astype(F32)

        x1, qa, kv1, kvf1, kvf2, qk, vm, og, gatest = _ffn_in(
            xf, wg_all, wu_all, wd_all, (l, 0),
            ln_g[l, 0][None], ln_b[l, 0][None], wint, 3 * att_w, 2 * ml_w, ml_w, S)
        att = _attention(qa, [kv1.reshape(B, S, -1), kvf1, kvf2], bias, B, S)
        att = att.reshape(B * S, att_w)
        cols, facs = _gates(gatest, gb[:, None], B, S)
        hm = _mlstm(qk, vm, og, cols, facs, conv_w[l], conv_b[l][None], ml_norm_g[l][None],
                    B, S)

        hd = D // XA_HEADS
        kt, v = _memkv(mem, bf(xkv_w[l]))
        xf = _tail(x1, att, hm, kt, v, bf(w_out[l]), bf(xq_w[l] * hd ** -0.5), bf(xo_w[l]),
                   wg_all, wu_all, wd_all, (l, 1), ln_g[l, 1:4], ln_b[l, 1:4], S)
    return xf.reshape(B, S, D)
```

```python
import functools
import math

import jax
import jax.numpy as jnp
from jax import lax
from jax.experimental import pallas as pl
from jax.experimental.pallas import tpu as pltpu

F32 = jnp.float32
BF16 = jnp.bfloat16

ATT_HD = 64
ATT_HEADS = 8
DILATED = ((128, 1), (512, 4), (2048, 16))
BLK = 128
ML_HEADS = 4
CHUNK = 128
CONV_K = 4
XA_HEADS = 4
REL_BUCKETS = 32
REL_MAX_DIST = 2048
DEPTH = 1
ALPHA = (2 * DEPTH) ** 0.25
LN_EPS = 1e-5
NEG = -1e30
LOG2E = math.log2(math.e)

LANES = 128
VMEM_LIMIT = 60 * 1024 * 1024
ROW_TILE = 512
ROW_SUBTILES = 2
TAIL_ROW_TILE = 1024
TAIL_SUB_ROWS = 512
ATT_ORDER = (2, 1, 0)
ATT_UNROLL = 32
ML_CHUNKS_PER_STEP = 4


def _const_spec(shape, lead=()):
    block = (None,) * len(lead) + tuple(shape[len(lead):])
    index = tuple(lead) + (0,) * (len(shape) - len(lead))
    return pl.BlockSpec(block, lambda *_: index, pipeline_mode=pl.Buffered(1))


def _weight_specs(weights, ffn_index):
    return [_const_spec(w.shape, ffn_index if w.ndim == 4 else ()) for w in weights]


def _layer_norm(y, g, b):
    mu = jnp.mean(y, -1, keepdims=True)
    yc = y - mu
    var = jnp.mean(yc * yc, -1, keepdims=True)
    return yc * lax.rsqrt(var + LN_EPS) * g + b


def _silu(x):
    return x * jax.nn.sigmoid(x)


def _swiglu(xb, wg_ref, wu_ref, wd_ref):
    g = jnp.dot(xb, wg_ref[...], preferred_element_type=F32)
    u = jnp.dot(xb, wu_ref[...], preferred_element_type=F32)
    h = (_silu(g) * u).astype(BF16)
    return jnp.dot(h, wd_ref[...], preferred_element_type=F32)


def _bias_kernel(rel_ref, bkt_ref, out_ref):
    bkt = bkt_ref[0]
    prev_half = lax.broadcasted_iota(jnp.int32, bkt.shape, 1) < BLK
    accs = [jnp.full(bkt.shape, NEG, F32) for _ in range(ATT_HEADS)]
    for b in range(REL_BUCKETS):
        hit = bkt == b
        for h in range(ATT_HEADS):
            accs[h] = jnp.where(hit, rel_ref[b, h] * LOG2E, accs[h])
    for h in range(ATT_HEADS):
        out_ref[0, 0, h] = accs[h]
        out_ref[1, 0, h] = jnp.where(prev_half, NEG, accs[h])


def _bias_tables(rel_bias):
    qi = jnp.arange(BLK)[:, None]
    ki = jnp.arange(2 * BLK)[None, :]
    off = qi + BLK - ki
    exact = REL_BUCKETS // 2
    n_log = REL_BUCKETS - exact
    starts = [math.ceil(exact * (REL_MAX_DIST / exact) ** (k / n_log)) for k in range(1, n_log)]
    tabs = []
    for window, dil in DILATED:
        n_keys = window // dil
        dist = dil * jnp.clip(off, 0, n_keys)
        large = exact + sum((dist >= s).astype(jnp.int32) for s in starts)
        bucket = jnp.where(dist < exact, dist, large)
        band = (off >= 0) & (off <= n_keys)
        tabs.append(jnp.where(band, bucket, -1))
    bkt = jnp.stack(tabs, 0).astype(jnp.int32)
    nd = len(DILATED)
    return pl.pallas_call(
        _bias_kernel,
        grid=(nd,),
        in_specs=[pl.BlockSpec(memory_space=pltpu.SMEM),
                  pl.BlockSpec((1, BLK, 2 * BLK), lambda d: (d, 0, 0))],
        out_specs=pl.BlockSpec((2, 1, ATT_HEADS, BLK, 2 * BLK), lambda d: (0, d, 0, 0, 0)),
        out_shape=jax.ShapeDtypeStruct((2, nd, ATT_HEADS, BLK, 2 * BLK), F32),
        name="bias",
    )(rel_bias.astype(F32), bkt)


def _memkv_kernel(mem_ref, wkv_ref, kt_ref, v_ref):
    D = mem_ref.shape[1]
    mb = mem_ref[...].astype(BF16)
    k = jnp.dot(mb, wkv_ref[:, 0:D], preferred_element_type=F32)
    kt_ref[...] = k.T.astype(BF16)
    v_ref[...] = jnp.dot(mb, wkv_ref[:, D:], preferred_element_type=F32).astype(BF16)


def _memkv(mem, wkv):
    B, L, D = mem.shape
    return pl.pallas_call(
        _memkv_kernel,
        grid=(B,),
        in_specs=[pl.BlockSpec((None, L, D), lambda b: (b, 0, 0)),
                  _const_spec((D, 2 * D))],
        out_specs=[pl.BlockSpec((None, D, L), lambda b: (b, 0, 0)),
                   pl.BlockSpec((None, L, D), lambda b: (b, 0, 0))],
        out_shape=[jax.ShapeDtypeStruct((B, D, L), BF16),
                   jax.ShapeDtypeStruct((B, L, D), BF16)],
        compiler_params=pltpu.CompilerParams(vmem_limit_bytes=VMEM_LIMIT),
        name="memkv",
    )(mem, wkv)


def _fold_rows(kv, t, s1, s2, fold_refs):
    (d1, ref1), (d2, ref2) = fold_refs
    ratio = d2 // d1
    sub = kv.shape[0]
    n1, n2 = sub // d1, sub // d2
    for c in range(kv.shape[1] // LANES):
        cl = slice(c * LANES, (c + 1) * LANES)
        s1[c] = kv[:, cl]
        for r1 in range(d1):
            p1 = s1[c, pl.ds(r1, n1, stride=d1), :]
            ref1[r1, t * n1:(t + 1) * n1, cl] = p1.astype(BF16)
            s2[c, r1 * n1:(r1 + 1) * n1, :] = p1
        for r1 in range(d1):
            for rp in range(ratio):
                p2 = s2[c, pl.ds(r1 * n1 + rp, n2, stride=ratio), :]
                ref2[d1 * rp + r1, t * n2:(t + 1) * n2, cl] = p2.astype(BF16)


def _ffn_in_kernel(x_ref, wg_ref, wu_ref, wd_ref, g_ref, b_ref, wint_ref,
                   x1_ref, qa_ref, kv1_ref, kvf1_ref, kvf2_ref, qk_ref, vm_ref, og_ref,
                   gatest_ref, s1, s2):
    n_q = qa_ref.shape[1]
    c1 = n_q + kv1_ref.shape[1]
    c2 = c1 + qk_ref.shape[0] * LANES
    c3 = c2 + vm_ref.shape[1]
    c4 = c3 + og_ref.shape[1]
    nt = (((1,), (1,)), ((), ()))
    sub = x_ref.shape[0] // ROW_SUBTILES
    tiles = [slice(t * sub, (t + 1) * sub) for t in range(ROW_SUBTILES)]
    ffs = [_swiglu(x_ref[rs, :].astype(BF16), wg_ref, wu_ref, wd_ref) for rs in tiles]
    fold_dils = [d for _, d in DILATED if d > 1]
    for t, (rs, ff) in enumerate(zip(tiles, ffs)):
        x1 = _layer_norm(ALPHA * x_ref[rs, :] + 0.5 * ff, g_ref[...], b_ref[...])
        x1_ref[rs, :] = x1
        xb = x1.astype(BF16)
        proj = lambda lo, hi, xb=xb: lax.dot_general(xb, wint_ref[lo:hi, :], nt,
                                                     preferred_element_type=F32)

        qkm = proj(c1, c2)
        for c in range(qk_ref.shape[0]):
            qk_ref[c, rs, :] = qkm[:, c * LANES:(c + 1) * LANES]

        qa_ref[rs, :] = proj(0, n_q) * (ATT_HD ** -0.5 * LOG2E)
        kv = proj(n_q, c1)
        kv1_ref[rs, :] = kv.astype(BF16)
        _fold_rows(kv, t, s1, s2, list(zip(fold_dils, (kvf1_ref, kvf2_ref))))
        vm_ref[rs, :] = proj(c2, c3).astype(BF16)
        og_ref[rs, :] = jax.nn.sigmoid(proj(c3, c4))
        gatest_ref[:, rs] = lax.dot_general(wint_ref[c4:c4 + gatest_ref.shape[0], :], xb, nt,
                                            preferred_element_type=F32)


def _ffn_in(x2d, wg, wu, wd, ffn_index, g, b, wint, n_att, n_qkm, n_ml, S):
    T, D = x2d.shape
    tm = ROW_TILE
    tps = S // tm
    row = lambda w: pl.BlockSpec((tm, w), lambda i: (i, 0))
    n_g = wint.shape[0] - n_att - n_qkm - 2 * n_ml
    n_slab = n_qkm // LANES
    n_q, n_kv = n_att // 3, 2 * n_att // 3
    fold = lambda d: pl.BlockSpec((None, d, tm // d, n_kv), lambda i: (i // tps, 0, i % tps, 0))
    fold_shape = lambda d: jax.ShapeDtypeStruct((T // S, d, S // d, n_kv), BF16)
    d1, d2 = [d for _, d in DILATED if d > 1]
    assert d2 % d1 == 0 and (tm // ROW_SUBTILES) % (16 * d2) == 0
    weights = (wg, wu, wd, g, b, wint)
    sub = tm // ROW_SUBTILES
    return pl.pallas_call(
        _ffn_in_kernel,
        grid=(T // tm,),
        in_specs=[row(D)] + _weight_specs(weights, ffn_index),
        out_specs=[row(D), row(n_q), row(n_kv), fold(d1), fold(d2),
                   pl.BlockSpec((n_slab, tm, LANES), lambda i: (0, i, 0)),
                   row(n_ml), row(n_ml),
                   pl.BlockSpec((n_g, tm), lambda i: (0, i))],
        out_shape=[jax.ShapeDtypeStruct((T, D), F32),
                   jax.ShapeDtypeStruct((T, n_q), F32),
                   jax.ShapeDtypeStruct((T, n_kv), BF16),
                   fold_shape(d1), fold_shape(d2),
                   jax.ShapeDtypeStruct((n_slab, T, LANES), F32),
                   jax.ShapeDtypeStruct((T, n_ml), BF16),
                   jax.ShapeDtypeStruct((T, n_ml), F32),
                   jax.ShapeDtypeStruct((n_g, T), F32)],
        scratch_shapes=[pltpu.VMEM((n_kv // LANES, sub, LANES), F32)] * 2,
        compiler_params=pltpu.CompilerParams(
            dimension_semantics=("parallel",), vmem_limit_bytes=VMEM_LIMIT),
        name="ffn_in",
    )(x2d, *weights)


def _attn_block(q_ref, kv_refs, bias_ref, out_ref, acc_s, m_s, l_s, j, step):
    di = ATT_ORDER[step]
    dil = DILATED[di][1]
    k_ref, v_ref = kv_refs[di]
    lane = lax.broadcasted_iota(jnp.int32, (BLK, LANES), 1)
    head0 = lane < ATT_HD
    r = j % dil
    n = j // dil

    def rows_of(start, stride):
        if stride == 1:
            return pl.ds(pl.multiple_of(start, BLK), BLK)
        return pl.ds(start, BLK, stride=stride)

    cur = rows_of(n * BLK, 1)
    prev = rows_of(jnp.maximum(n - 1, 0) * BLK, 1)
    if dil == 1:
        take = lambda ref, rows: ref[rows, :]
    else:
        take = lambda ref, rows: ref[r, rows, :]
    rows = rows_of(r + n * (BLK * dil), dil)
    q = q_ref[rows, :]
    kk = jnp.concatenate([take(k_ref, prev), take(k_ref, cur)], axis=0)
    vv = jnp.concatenate([take(v_ref, prev), take(v_ref, cur)], axis=0)
    first = jnp.where(n == 0, 1, 0)
    ms, ls, os_ = [], [], []
    for h in range(2):
        sel = head0 if h == 0 else jnp.logical_not(head0)
        qh = jnp.where(sel, q, 0.0).astype(BF16)
        lg = lax.dot_general(qh, kk, (((1,), (1,)), ((), ())), preferred_element_type=F32)
        lg = lg + bias_ref[first, di, h]
        mh = jnp.max(lg, -1, keepdims=True)
        p = jnp.exp2(lg - mh)
        ls.append(jnp.sum(p, -1, keepdims=True))
        ms.append(mh)
        os_.append(jnp.dot(p.astype(BF16), vv, preferred_element_type=F32))
    m_b = jnp.where(head0, ms[0], ms[1])
    l_b = jnp.where(head0, ls[0], ls[1])
    o_b = jnp.where(head0, os_[0], os_[1])
    if step > 0:
        m_o = m_s[rows, :]
        m_n = jnp.maximum(m_o, m_b)
        e_o = jnp.exp2(m_o - m_n)
        e_b = jnp.exp2(m_b - m_n)
        l_b = l_s[rows, :] * e_o + l_b * e_b
        o_b = acc_s[rows, :] * e_o + o_b * e_b
        m_b = m_n
    if step < len(ATT_ORDER) - 1:
        m_s[rows, :] = m_b
        l_s[rows, :] = l_b
        acc_s[rows, :] = o_b
    else:
        out_ref[rows, :] = (o_b / l_b).astype(out_ref.dtype)


def _attn_kernel(q_ref, *refs):
    n_br = len(DILATED)
    kv_refs = [(refs[2 * i], refs[2 * i + 1]) for i in range(n_br)]
    bias_ref, out_ref, acc_s, m_s, l_s = refs[2 * n_br:]
    n_blocks = q_ref.shape[0] // BLK
    for step in range(len(ATT_ORDER)):
        def body(i, _, step=step):
            for u in range(ATT_UNROLL):
                _attn_block(q_ref, kv_refs, bias_ref, out_ref, acc_s, m_s, l_s,
                            i * ATT_UNROLL + u, step)
            return 0
        lax.fori_loop(0, n_blocks // ATT_UNROLL, body, 0)


def _attention(q, kvs, bias, B, S):
    n_pairs = ATT_HEADS * ATT_HD // LANES
    dils = [d for _, d in DILATED]
    assert dils[ATT_ORDER[-1]] == 1
    in_specs = [pl.BlockSpec((None, S, LANES), lambda b, p: (b, 0, p))]
    operands = [q.reshape(B, S, q.shape[-1])]
    for d, kv in zip(dils, kvs):
        for off in (0, n_pairs):
            if d == 1:
                in_specs.append(pl.BlockSpec((None, S, LANES),
                                             lambda b, p, off=off: (b, 0, off + p)))
            else:
                in_specs.append(pl.BlockSpec((None, d, S // d, LANES),
                                             lambda b, p, off=off: (b, 0, 0, off + p)))
            operands.append(kv)
    in_specs.append(pl.BlockSpec((2, len(DILATED), 2, BLK, 2 * BLK),
                                 lambda b, p: (0, 0, p, 0, 0)))
    return pl.pallas_call(
        _attn_kernel,
        grid=(B, n_pairs),
        in_specs=in_specs,
        out_specs=pl.BlockSpec((None, S, LANES), lambda b, p: (b, 0, p)),
        out_shape=jax.ShapeDtypeStruct((B, S, n_pairs * LANES), BF16),
        scratch_shapes=[pltpu.VMEM((S, LANES), F32)] * 3,
        compiler_params=pltpu.CompilerParams(
            dimension_semantics=("parallel", "parallel"), vmem_limit_bytes=VMEM_LIMIT),
        name="attn",
    )(*operands, bias)


def _split3(a):
    hi = a.astype(BF16)
    r1 = a - hi.astype(F32)
    mid = r1.astype(BF16)
    lo = (r1 - mid.astype(F32)).astype(BF16)
    return hi, mid, lo


def _log_sigmoid(x):
    return jnp.minimum(x, 0.0) - jnp.log1p(jnp.exp(-jnp.abs(x)))


def _prefix_max_lanes(x):
    lane = lax.broadcasted_iota(jnp.int32, x.shape, 1)
    sh = 1
    while sh < x.shape[1]:
        x = jnp.maximum(x, jnp.where(lane >= sh, pltpu.roll(x, sh, axis=1), -jnp.inf))
        sh *= 2
    return x


def _gates_kernel(gt_ref, gbc_ref, tri_ref, cols_ref, rows_ref):
    H = ML_HEADS
    n_chunks = gt_ref.shape[1] // CHUNK
    n_rows = 2 * H * n_chunks
    tri = tri_ref[...]
    head_row = (lax.broadcasted_iota(jnp.int32, (n_rows, CHUNK), 0) & (2 * H - 1)) < H
    gr = jnp.concatenate([gt_ref[:, i * CHUNK:(i + 1) * CHUNK] + gbc_ref[...]
                          for i in range(n_chunks)], axis=0)
    b_all = sum(lax.dot_general(part, tri, (((1,), (1,)), ((), ())),
                                preferred_element_type=F32)
                for part in _split3(_log_sigmoid(gr)))
    b = jnp.concatenate([pltpu.roll(b_all[i * 2 * H:(i + 1) * 2 * H], H, axis=0)
                         for i in range(n_chunks)], axis=0)
    b = jnp.where(head_row, b, 0.0)
    u = jnp.where(head_row, gr - b, 0.0)
    cm = _prefix_max_lanes(u)
    u_max = jnp.broadcast_to(cm[:, CHUNK - 1:CHUNK], cm.shape)
    g_tot = jnp.broadcast_to(b[:, CHUNK - 1:CHUNK], b.shape)
    m_prev = jnp.zeros((2 * H, CHUNK), F32)
    m_prevs = []
    for i in range(n_chunks):
        rs = slice(i * 2 * H, (i + 1) * 2 * H)
        m_prevs.append(m_prev)
        m_prev = g_tot[rs] + jnp.maximum(m_prev, u_max[rs])
    m_prev = jnp.concatenate(m_prevs, axis=0)
    mm = jnp.maximum(m_prev, u_max)
    sp = jnp.exp(m_prev - mm)
    sl = jnp.exp(u_max - mm)
    wa = jnp.exp(u - u_max)
    m_in = jnp.maximum(m_prev, cm)
    e_w = jnp.exp(m_prev - m_in)
    e_m = jnp.exp(-(b + m_in))
    per = 2 * H * ML_CHUNKS_PER_STEP
    assert 4 * per == LANES
    pad = jnp.zeros((per, CHUNK), F32)
    for s in range(n_rows // per):
        rs = slice(s * per, (s + 1) * per)
        cols_ref[s] = jnp.concatenate([m_in[rs], e_w[rs], e_m[rs], pad], axis=0).T
        rows_ref[s] = jnp.concatenate([u[rs], sp[rs], sl[rs], wa[rs]], axis=0)


def _gates(gatest, gb_col, B, S):
    G = gatest.shape[0]
    n_steps = S // (ML_CHUNKS_PER_STEP * CHUNK)
    tri = jnp.tril(jnp.ones((CHUNK, CHUNK), F32)).astype(BF16)
    fac = pl.BlockSpec((None, n_steps, LANES, CHUNK), lambda b: (b, 0, 0, 0))
    return pl.pallas_call(
        _gates_kernel,
        grid=(B,),
        in_specs=[pl.BlockSpec((G, S), lambda b: (0, b)),
                  _const_spec(gb_col.shape), _const_spec(tri.shape)],
        out_specs=[fac, fac],
        out_shape=[jax.ShapeDtypeStruct((B, n_steps, LANES, CHUNK), F32)] * 2,
        compiler_params=pltpu.CompilerParams(
            dimension_semantics=("parallel",), vmem_limit_bytes=VMEM_LIMIT),
        name="gates",
    )(gatest, gb_col, tri)


def _mlstm_kernel(qk_ref, v_ref, og_ref, cols_ref, rows_ref, cw_ref, cb_ref, mlg_ref,
                  out_ref, xbuf, ybuf, ct_s, nb_s):
    step = pl.program_id(1)
    E = LANES
    H = ML_HEADS
    n_slab, rows = qk_ref.shape[0], qk_ref.shape[1]
    half = rows // 2
    n_chunks = rows // CHUNK
    assert n_slab == 2 * H

    @pl.when(step == 0)
    def _():
        xbuf[:, 0:8, :] = jnp.zeros((n_slab, 8, LANES), F32)
        ct_s[...] = jnp.zeros(ct_s.shape, F32)
        nb_s[...] = jnp.zeros(nb_s.shape, F32)

    @pl.when(step > 0)
    def _():
        xbuf[:, 0:8, :] = xbuf[:, rows:rows + 8, :]

    xbuf[:, 8:8 + rows, :] = qk_ref[...]
    for c in range(n_slab):
        cl = slice(c * LANES, (c + 1) * LANES)
        for par in range(2):
            acc = cb_ref[:, cl]
            for j in range(CONV_K):
                off = 8 - (CONV_K - 1) + j + par
                acc = acc + xbuf[c, pl.ds(off, half, stride=2), :] * cw_ref[j:j + 1, cl]
            act = _silu(acc)
            if c >= H:
                act = act * (E ** -0.5)
            ybuf[c, pl.ds(par, half, stride=2), :] = act

    n_rows = 2 * H * n_chunks
    cols = cols_ref[...]
    u = rows_ref[0:n_rows, :]
    sp_all = rows_ref[n_rows:2 * n_rows, :]
    sl_all = rows_ref[2 * n_rows:3 * n_rows, :]
    wa_all = rows_ref[3 * n_rows:4 * n_rows, :]

    ti = lax.broadcasted_iota(jnp.int32, (CHUNK, CHUNK), 0)
    si = lax.broadcasted_iota(jnp.int32, (CHUNK, CHUNK), 1)
    causal = si <= ti
    ones = jnp.ones((CHUNK, E), BF16)

    cts = [ct_s[h] for h in range(H)]
    nbs = [nb_s[h] for h in range(H)]
    for i in range(n_chunks):
        r0 = i * CHUNK
        for h in range(H):
            row = i * 2 * H + h
            col = lambda q_, row=row: cols[:, q_ * n_rows + row:q_ * n_rows + row + 1]
            m_in_c, e_w, e_m = col(0), col(1), col(2)
            q = ybuf[h, r0:r0 + CHUNK, :]
            k = ybuf[H + h, r0:r0 + CHUNK, :]
            v1 = jnp.concatenate([v_ref[r0:r0 + CHUNK, h * E:(h + 1) * E], ones], axis=1)
            qb = q.astype(BF16)

            d_w = jnp.where(causal, jnp.exp(u[row:row + 1, :] - m_in_c), 0.0)
            s_qk = lax.dot_general(qb, k.astype(BF16), (((1,), (1,)), ((), ())),
                                   preferred_element_type=F32) * d_w
            st = jnp.concatenate([cts[h].astype(BF16), nbs[h].astype(BF16)], axis=1)
            inter = jnp.dot(qb, st, preferred_element_type=F32)
            intra = jnp.dot(s_qk.astype(BF16), v1, preferred_element_type=F32)
            num = e_w * inter[:, :E] + intra[:, :E]
            den = e_w * inter[:, E:] + intra[:, E:]
            hh = num / jnp.maximum(jnp.abs(den), e_m)

            hg = og_ref[r0:r0 + CHUNK, h * E:(h + 1) * E] * hh
            mu = jnp.mean(hg, -1, keepdims=True)
            ex2 = jnp.mean(hg * hg, -1, keepdims=True)
            hc = hg - mu
            var = jnp.maximum(ex2 - mu * mu, 0.0)
            yn = hc * lax.rsqrt(var + LN_EPS) * mlg_ref[:, h * E:(h + 1) * E]
            out_ref[r0:r0 + CHUNK, h * E:(h + 1) * E] = yn.astype(out_ref.dtype)

            kw = (k.T * wa_all[row:row + 1, :]).astype(BF16)
            loc = jnp.dot(kw, v1, preferred_element_type=F32)
            sp_h, sl_h = sp_all[row:row + 1, :], sl_all[row:row + 1, :]
            cts[h] = sp_h * cts[h] + sl_h * loc[:, :E]
            nbs[h] = sp_h * nbs[h] + sl_h * loc[:, E:]
    for h in range(H):
        ct_s[h] = cts[h]
        nb_s[h] = nbs[h]


def _mlstm(qk, vm, og, cols, facs, conv_w, conv_b, ml_g, B, S):
    rows = ML_CHUNKS_PER_STEP * CHUNK
    ns = S // rows
    W = vm.shape[1]
    row = lambda w: pl.BlockSpec((rows, w), lambda b, c: (b * ns + c, 0))
    fac = pl.BlockSpec((None, None, LANES, CHUNK), lambda b, c: (b, c, 0, 0))
    consts = (conv_w, conv_b, ml_g)
    return pl.pallas_call(
        _mlstm_kernel,
        grid=(B, ns),
        in_specs=[pl.BlockSpec((qk.shape[0], rows, LANES), lambda b, c: (0, b * ns + c, 0)),
                  row(W), row(W), fac, fac]
                 + [_const_spec(a.shape) for a in consts],
        out_specs=row(W),
        out_shape=jax.ShapeDtypeStruct((B * S, W), BF16),
        scratch_shapes=[pltpu.VMEM((qk.shape[0], rows + 8, LANES), F32),
                        pltpu.VMEM((qk.shape[0], rows, LANES), F32),
                        pltpu.VMEM((ML_HEADS, LANES, LANES), F32),
                        pltpu.VMEM((ML_HEADS, LANES, LANES), F32)],
        compiler_params=pltpu.CompilerParams(
            dimension_semantics=("parallel", "arbitrary"), vmem_limit_bytes=VMEM_LIMIT),
        name="mlstm",
    )(qk, vm, og, cols, facs, *consts)


def _tail_kernel(x1_ref, att_ref, hm_ref, kt_ref, v_ref, wout_ref, wq_ref, wo_ref,
                 wg_ref, wu_ref, wd_ref, g_ref, b_ref, out_ref):
    n_sub = x1_ref.shape[0] // TAIL_SUB_ROWS
    tiles = [slice(t * TAIL_SUB_ROWS, (t + 1) * TAIL_SUB_ROWS) for t in range(n_sub)]
    n_a = att_ref.shape[1]
    hd = wq_ref.shape[1] // XA_HEADS

    def mix(rs):
        return (jnp.dot(att_ref[rs, :], wout_ref[0:n_a, :], preferred_element_type=F32)
                + jnp.dot(hm_ref[rs, :], wout_ref[n_a:, :], preferred_element_type=F32))

    def cross(x2):
        q = jnp.dot(x2.astype(BF16), wq_ref[...], preferred_element_type=F32)
        heads = []
        for h in range(XA_HEADS):
            qh = q[:, h * hd:(h + 1) * hd].astype(BF16)
            lg = jnp.dot(qh, kt_ref[h * hd:(h + 1) * hd, :], preferred_element_type=F32)
            p = jnp.exp(lg - jnp.max(lg, -1, keepdims=True))
            s = jnp.sum(p, -1, keepdims=True)
            o = jnp.dot(p.astype(BF16), v_ref[:, h * hd:(h + 1) * hd],
                        preferred_element_type=F32)
            heads.append((o / s).astype(BF16))
        return jnp.dot(jnp.concatenate(heads, axis=1), wo_ref[...], preferred_element_type=F32)

    mixes = [mix(rs) for rs in tiles]
    x2s = [_layer_norm(ALPHA * x1_ref[rs, :] + m, g_ref[0:1, :], b_ref[0:1, :])
           for rs, m in zip(tiles, mixes)]
    xas = [cross(x2) for x2 in x2s]
    x3s = [_layer_norm(ALPHA * x2 + xa, g_ref[1:2, :], b_ref[1:2, :])
           for x2, xa in zip(x2s, xas)]
    ffs = [_swiglu(x3.astype(BF16), wg_ref, wu_ref, wd_ref) for x3 in x3s]
    for rs, x3, ff in zip(tiles, x3s, ffs):
        out_ref[rs, :] = _layer_norm(ALPHA * x3 + 0.5 * ff, g_ref[2:3, :], b_ref[2:3, :])


def _tail(x1, att, hm, kt, v, wout, wq, wo, wg, wu, wd, ffn_index, g, b, S):
    T, D = x1.shape
    tm = TAIL_ROW_TILE
    per_batch = S // tm
    row = lambda w: pl.BlockSpec((tm, w), lambda i: (i, 0))
    L = v.shape[1]
    weights = (wout, wq, wo, wg, wu, wd, g, b)
    return pl.pallas_call(
        _tail_kernel,
        grid=(T // tm,),
        in_specs=[row(D), row(att.shape[1]), row(hm.shape[1]),
                  pl.BlockSpec((None, D, L), lambda i: (i // per_batch, 0, 0)),
                  pl.BlockSpec((None, L, D), lambda i: (i // per_batch, 0, 0))]
                 + _weight_specs(weights, ffn_index),
        out_specs=row(D),
        out_shape=jax.ShapeDtypeStruct((T, D), F32),
        compiler_params=pltpu.CompilerParams(
            dimension_semantics=("parallel",), vmem_limit_bytes=VMEM_LIMIT),
        name="tail",
    )(x1, att, hm, kt, v, *weights)


def kernel(x, mem, rel_bias, ln_g, ln_b, ffn_w_gate, ffn_w_up, ffn_w_down, w_in, conv_w, conv_b,
           ig_bias, fg_bias, ml_norm_g, w_out, xq_w, xkv_w, xo_w):
    B, S, D = x.shape
    att_w = ATT_HEADS * ATT_HD
    ml_w = ML_HEADS * LANES
    bias = _bias_tables(rel_bias)
    xf = x.reshape(B * S, D)
    bf = lambda a: a.astype(BF16)
    wg_all, wu_all, wd_all = bf(ffn_w_gate), bf(ffn_w_up), bf(ffn_w_down)
    for l in range(DEPTH):
        wint = bf(w_in[l].T)
        gb = jnp.concatenate([ig_bias[l], fg_bias[l]]).astype(F32)

        x1, qa, kv1, kvf1, kvf2, qk, vm, og, gatest = _ffn_in(
            xf, wg_all, wu_all, wd_all, (l, 0),
            ln_g[l, 0][None], ln_b[l, 0][None], wint, 3 * att_w, 2 * ml_w, ml_w, S)
        att = _attention(qa, [kv1.reshape(B, S, -1), kvf1, kvf2], bias, B, S)
        att = att.reshape(B * S, att_w)
        cols, facs = _gates(gatest, gb[:, None], B, S)
        hm = _mlstm(qk, vm, og, cols, facs, conv_w[l], conv_b[l][None], ml_norm_g[l][None],
                    B, S)

        hd = D // XA_HEADS
        kt, v = _memkv(mem, bf(xkv_w[l]))
        xf = _tail(x1, att, hm, kt, v, bf(w_out[l]), bf(xq_w[l] * hd ** -0.5), bf(xo_w[l]),
                   wg_all, wu_all, wd_all, (l, 1), ln_g[l, 1:4], ln_b[l, 1:4], S)
    return xf.reshape(B, S, D)
```

```python
import functools
import math

import jax
import jax.numpy as jnp
from jax import lax
from jax.experimental import pallas as pl
from jax.experimental.pallas import tpu as pltpu

F32 = jnp.float32
BF16 = jnp.bfloat16

ATT_HD = 64
ATT_HEADS = 8
DILATED = ((128, 1), (512, 4), (2048, 16))
BLK = 128
ML_HEADS = 4
CHUNK = 128
CONV_K = 4
XA_HEADS = 4
REL_BUCKETS = 32
REL_MAX_DIST = 2048
DEPTH = 1
ALPHA = (2 * DEPTH) ** 0.25
LN_EPS = 1e-5
NEG = -1e30
LOG2E = math.log2(math.e)

LANES = 128
VMEM_LIMIT = 60 * 1024 * 1024
ROW_TILE = 512
ROW_SUBTILES = 2
TAIL_ROW_TILE = 1024
TAIL_SUB_ROWS = 512
ATT_ORDER = (2, 1, 0)
ATT_UNROLL = 32
ML_CHUNKS_PER_STEP = 4


def _const_spec(shape, lead=()):
    block = (None,) * len(lead) + tuple(shape[len(lead):])
    index = tuple(lead) + (0,) * (len(shape) - len(lead))
    return pl.BlockSpec(block, lambda *_: index, pipeline_mode=pl.Buffered(1))


def _weight_specs(weights, ffn_index):
    return [_const_spec(w.shape, ffn_index if w.ndim == 4 else ()) for w in weights]


def _layer_norm(y, g, b):
    mu = jnp.mean(y, -1, keepdims=True)
    yc = y - mu
    var = jnp.mean(yc * yc, -1, keepdims=True)
    return yc * lax.rsqrt(var + LN_EPS) * g + b


def _silu(x):
    return x * jax.nn.sigmoid(x)


def _swiglu(xb, wg_ref, wu_ref, wd_ref):
    g = jnp.dot(xb, wg_ref[...], preferred_element_type=F32)
    u = jnp.dot(xb, wu_ref[...], preferred_element_type=F32)
    h = (_silu(g) * u).astype(BF16)
    return jnp.dot(h, wd_ref[...], preferred_element_type=F32)


def _bias_kernel(rel_ref, bkt_ref, out_ref):
    bkt = bkt_ref[0]
    prev_half = lax.broadcasted_iota(jnp.int32, bkt.shape, 1) < BLK
    accs = [jnp.full(bkt.shape, NEG, F32) for _ in range(ATT_HEADS)]
    for b in range(REL_BUCKETS):
        hit = bkt == b
        for h in range(ATT_HEADS):
            accs[h] = jnp.where(hit, rel_ref[b, h] * LOG2E, accs[h])
    for h in range(ATT_HEADS):
        out_ref[0, 0, h] = accs[h]
        out_ref[1, 0, h] = jnp.where(prev_half, NEG, accs[h])


def _bias_tables(rel_bias):
    qi = jnp.arange(BLK)[:, None]
    ki = jnp.arange(2 * BLK)[None, :]
    off = qi + BLK - ki
    exact = REL_BUCKETS // 2
    n_log = REL_BUCKETS - exact
    starts = [math.ceil(exact * (REL_MAX_DIST / exact) ** (k / n_log)) for k in range(1, n_log)]
    tabs = []
    for window, dil in DILATED:
        n_keys = window // dil
        dist = dil * jnp.clip(off, 0, n_keys)
        large = exact + sum((dist >= s).astype(jnp.int32) for s in starts)
        bucket = jnp.where(dist < exact, dist, large)
        band = (off >= 0) & (off <= n_keys)
        tabs.append(jnp.where(band, bucket, -1))
    bkt = jnp.stack(tabs, 0).astype(jnp.int32)
    nd = len(DILATED)
    return pl.pallas_call(
        _bias_kernel,
        grid=(nd,),
        in_specs=[pl.BlockSpec(memory_space=pltpu.SMEM),
                  pl.BlockSpec((1, BLK, 2 * BLK), lambda d: (d, 0, 0))],
        out_specs=pl.BlockSpec((2, 1, ATT_HEADS, BLK, 2 * BLK), lambda d: (0, d, 0, 0, 0)),
        out_shape=jax.ShapeDtypeStruct((2, nd, ATT_HEADS, BLK, 2 * BLK), F32),
        name="bias",
    )(rel_bias.astype(F32), bkt)


def _memkv_kernel(mem_ref, wkv_ref, kt_ref, v_ref):
    D = mem_ref.shape[1]
    mb = mem_ref[...].astype(BF16)
    k = jnp.dot(mb, wkv_ref[:, 0:D], preferred_element_type=F32)
    kt_ref[...] = k.T.astype(BF16)
    v_ref[...] = jnp.dot(mb, wkv_ref[:, D:], preferred_element_type=F32).astype(BF16)


def _memkv(mem, wkv):
    B, L, D = mem.shape
    return pl.pallas_call(
        _memkv_kernel,
        grid=(B,),
        in_specs=[pl.BlockSpec((None, L, D), lambda b: (b, 0, 0)),
                  _const_spec((D, 2 * D))],
        out_specs=[pl.BlockSpec((None, D, L), lambda b: (b, 0, 0)),
                   pl.BlockSpec((None, L, D), lambda b: (b, 0, 0))],
        out_shape=[jax.ShapeDtypeStruct((B, D, L), BF16),
                   jax.ShapeDtypeStruct((B, L, D), BF16)],
        compiler_params=pltpu.CompilerParams(vmem_limit_bytes=VMEM_LIMIT),
        name="memkv",
    )(mem, wkv)


def _fold_rows(kv, t, s1, s2, fold_refs):
    (d1, ref1), (d2, ref2) = fold_refs
    ratio = d2 // d1
    sub = kv.shape[0]
    n1, n2 = sub // d1, sub // d2
    for c in range(kv.shape[1] // LANES):
        cl = slice(c * LANES, (c + 1) * LANES)
        s1[c] = kv[:, cl]
        for r1 in range(d1):
            p1 = s1[c, pl.ds(r1, n1, stride=d1), :]
            ref1[r1, t * n1:(t + 1) * n1, cl] = p1.astype(BF16)
            s2[c, r1 * n1:(r1 + 1) * n1, :] = p1
        for r1 in range(d1):
            for rp in range(ratio):
                p2 = s2[c, pl.ds(r1 * n1 + rp, n2, stride=ratio), :]
                ref2[d1 * rp + r1, t * n2:(t + 1) * n2, cl] = p2.astype(BF16)


def _ffn_in_kernel(x_ref, wg_ref, wu_ref, wd_ref, g_ref, b_ref, wint_ref,
                   x1_ref, qa_ref, kv1_ref, kvf1_ref, kvf2_ref, qk_ref, vm_ref, og_ref,
                   gatest_ref, s1, s2):
    n_q = qa_ref.shape[1]
    c1 = n_q + kv1_ref.shape[1]
    c2 = c1 + qk_ref.shape[0] * LANES
    c3 = c2 + vm_ref.shape[1]
    c4 = c3 + og_ref.shape[1]
    nt = (((1,), (1,)), ((), ()))
    sub = x_ref.shape[0] // ROW_SUBTILES
    tiles = [slice(t * sub, (t + 1) * sub) for t in range(ROW_SUBTILES)]
    ffs = [_swiglu(x_ref[rs, :].astype(BF16), wg_ref, wu_ref, wd_ref) for rs in tiles]
    fold_dils = [d for _, d in DILATED if d > 1]
    for t, (rs, ff) in enumerate(zip(tiles, ffs)):
        x1 = _layer_norm(ALPHA * x_ref[rs, :] + 0.5 * ff, g_ref[...], b_ref[...])
        x1_ref[rs, :] = x1
        xb = x1.astype(BF16)
        proj = lambda lo, hi, xb=xb: lax.dot_general(xb, wint_ref[lo:hi, :], nt,
                                                     preferred_element_type=F32)

        qkm = proj(c1, c2)
        for c in range(qk_ref.shape[0]):
            qk_ref[c, rs, :] = qkm[:, c * LANES:(c + 1) * LANES]

        qa_ref[rs, :] = proj(0, n_q) * (ATT_HD ** -0.5 * LOG2E)
        kv = proj(n_q, c1)
        kv1_ref[rs, :] = kv.astype(BF16)
        _fold_rows(kv, t, s1, s2, list(zip(fold_dils, (kvf1_ref, kvf2_ref))))
        vm_ref[rs, :] = proj(c2, c3).astype(BF16)
        og_ref[rs, :] = jax.nn.sigmoid(proj(c3, c4))
        gatest_ref[:, rs] = lax.dot_general(wint_ref[c4:c4 + gatest_ref.shape[0], :], xb, nt,
                                            preferred_element_type=F32)


def _ffn_in(x2d, wg, wu, wd, ffn_index, g, b, wint, n_att, n_qkm, n_ml, S):
    T, D = x2d.shape
    tm = ROW_TILE
    tps = S // tm
    row = lambda w: pl.BlockSpec((tm, w), lambda i: (i, 0))
    n_g = wint.shape[0] - n_att - n_qkm - 2 * n_ml
    n_slab = n_qkm // LANES
    n_q, n_kv = n_att // 3, 2 * n_att // 3
    fold = lambda d: pl.BlockSpec((None, d, tm // d, n_kv), lambda i: (i // tps, 0, i % tps, 0))
    fold_shape = lambda d: jax.ShapeDtypeStruct((T // S, d, S // d, n_kv), BF16)
    d1, d2 = [d for _, d in DILATED if d > 1]
    assert d2 % d1 == 0 and (tm // ROW_SUBTILES) % (16 * d2) == 0
    weights = (wg, wu, wd, g, b, wint)
    sub = tm // ROW_SUBTILES
    return pl.pallas_call(
        _ffn_in_kernel,
        grid=(T // tm,),
        in_specs=[row(D)] + _weight_specs(weights, ffn_index),
        out_specs=[row(D), row(n_q), row(n_kv), fold(d1), fold(d2),
                   pl.BlockSpec((n_slab, tm, LANES), lambda i: (0, i, 0)),
                   row(n_ml), row(n_ml),
                   pl.BlockSpec((n_g, tm), lambda i: (0, i))],
        out_shape=[jax.ShapeDtypeStruct((T, D), F32),
                   jax.ShapeDtypeStruct((T, n_q), F32),
                   jax.ShapeDtypeStruct((T, n_kv), BF16),
                   fold_shape(d1), fold_shape(d2),
                   jax.ShapeDtypeStruct((n_slab, T, LANES), F32),
                   jax.ShapeDtypeStruct((T, n_ml), BF16),
                   jax.ShapeDtypeStruct((T, n_ml), F32),
                   jax.ShapeDtypeStruct((n_g, T), F32)],
        scratch_shapes=[pltpu.VMEM((n_kv // LANES, sub, LANES), F32)] * 2,
        compiler_params=pltpu.CompilerParams(
            dimension_semantics=("parallel",), vmem_limit_bytes=VMEM_LIMIT),
        name="ffn_in",
    )(x2d, *weights)


def _attn_block(q_ref, kv_refs, bias_ref, out_ref, acc_s, m_s, l_s, j, step):
    di = ATT_ORDER[step]
    dil = DILATED[di][1]
    k_ref, v_ref = kv_refs[di]
    lane = lax.broadcasted_iota(jnp.int32, (BLK, LANES), 1)
    head0 = lane < ATT_HD
    r = j % dil
    n = j // dil

    def rows_of(start, stride):
        if stride == 1:
            return pl.ds(pl.multiple_of(start, BLK), BLK)
        return pl.ds(start, BLK, stride=stride)

    cur = rows_of(n * BLK, 1)
    prev = rows_of(jnp.maximum(n - 1, 0) * BLK, 1)
    if dil == 1:
        take = lambda ref, rows: ref[rows, :]
    else:
        take = lambda ref, rows: ref[r, rows, :]
    rows = rows_of(r + n * (BLK * dil), dil)
    q = q_ref[rows, :]
    kk = jnp.concatenate([take(k_ref, prev), take(k_ref, cur)], axis=0)
    vv = jnp.concatenate([take(v_ref, prev), take(v_ref, cur)], axis=0)
    first = jnp.where(n == 0, 1, 0)
    ms, ls, os_ = [], [], []
    for h in range(2):
        sel = head0 if h == 0 else jnp.logical_not(head0)
        qh = jnp.where(sel, q, 0.0).astype(BF16)
        lg = lax.dot_general(qh, kk, (((1,), (1,)), ((), ())), preferred_element_type=F32)
        lg = lg + bias_ref[first, di, h]
        mh = jnp.max(lg, -1, keepdims=True)
        p = jnp.exp2(lg - mh)
        ls.append(jnp.sum(p, -1, keepdims=True))
        ms.append(mh)
        os_.append(jnp.dot(p.astype(BF16), vv, preferred_element_type=F32))
    m_b = jnp.where(head0, ms[0], ms[1])
    l_b = jnp.where(head0, ls[0], ls[1])
    o_b = jnp.where(head0, os_[0], os_[1])
    if step > 0:
        m_o = m_s[rows, :]
        m_n = jnp.maximum(m_o, m_b)
        e_o = jnp.exp2(m_o - m_n)
        e_b = jnp.exp2(m_b - m_n)
        l_b = l_s[rows, :] * e_o + l_b * e_b
        o_b = acc_s[rows, :] * e_o + o_b * e_b
        m_b = m_n
    if step < len(ATT_ORDER) - 1:
        m_s[rows, :] = m_b
        l_s[rows, :] = l_b
        acc_s[rows, :] = o_b
    else:
        out_ref[rows, :] = (o_b / l_b).astype(out_ref.dtype)


def _attn_kernel(q_ref, *refs):
    n_br = len(DILATED)
    kv_refs = [(refs[2 * i], refs[2 * i + 1]) for i in range(n_br)]
    bias_ref, out_ref, acc_s, m_s, l_s = refs[2 * n_br:]
    n_blocks = q_ref.shape[0] // BLK
    for step in range(len(ATT_ORDER)):
        def body(i, _, step=step):
            for u in range(ATT_UNROLL):
                _attn_block(q_ref, kv_refs, bias_ref, out_ref, acc_s, m_s, l_s,
                            i * ATT_UNROLL + u, step)
            return 0
        lax.fori_loop(0, n_blocks // ATT_UNROLL, body, 0)


def _attention(q, kvs, bias, B, S):
    n_pairs = ATT_HEADS * ATT_HD // LANES
    dils = [d for _, d in DILATED]
    assert dils[ATT_ORDER[-1]] == 1
    in_specs = [pl.BlockSpec((None, S, LANES), lambda b, p: (b, 0, p))]
    operands = [q.reshape(B, S, q.shape[-1])]
    for d, kv in zip(dils, kvs):
        for off in (0, n_pairs):
            if d == 1:
                in_specs.append(pl.BlockSpec((None, S, LANES),
                                             lambda b, p, off=off: (b, 0, off + p)))
            else:
                in_specs.append(pl.BlockSpec((None, d, S // d, LANES),
                                             lambda b, p, off=off: (b, 0, 0, off + p)))
            operands.append(kv)
    in_specs.append(pl.BlockSpec((2, len(DILATED), 2, BLK, 2 * BLK),
                                 lambda b, p: (0, 0, p, 0, 0)))
    return pl.pallas_call(
        _attn_kernel,
        grid=(B, n_pairs),
        in_specs=in_specs,
        out_specs=pl.BlockSpec((None, S, LANES), lambda b, p: (b, 0, p)),
        out_shape=jax.ShapeDtypeStruct((B, S, n_pairs * LANES), BF16),
        scratch_shapes=[pltpu.VMEM((S, LANES), F32)] * 3,
        compiler_params=pltpu.CompilerParams(
            dimension_semantics=("parallel", "parallel"), vmem_limit_bytes=VMEM_LIMIT),
        name="attn",
    )(*operands, bias)


def _split3(a):
    hi = a.astype(BF16)
    r1 = a - hi.astype(F32)
    mid = r1.astype(BF16)
    lo = (r1 - mid.astype(F32)).astype(BF16)
    return hi, mid, lo


def _log_sigmoid(x):
    return jnp.minimum(x, 0.0) - jnp.log1p(jnp.exp(-jnp.abs(x)))


def _prefix_max_lanes(x):
    lane = lax.broadcasted_iota(jnp.int32, x.shape, 1)
    sh = 1
    while sh < x.shape[1]:
        x = jnp.maximum(x, jnp.where(lane >= sh, pltpu.roll(x, sh, axis=1), -jnp.inf))
        sh *= 2
    return x


def _gate_factors(gt_ref, gbc_ref, tri_ref, cols_ref, rows_ref):
    H = ML_HEADS
    n_chunks = gt_ref.shape[1] // CHUNK
    n_rows = 2 * H * n_chunks
    tri = tri_ref[...]
    head_row = (lax.broadcasted_iota(jnp.int32, (n_rows, CHUNK), 0) & (2 * H - 1)) < H
    gr = jnp.concatenate([gt_ref[:, i * CHUNK:(i + 1) * CHUNK] + gbc_ref[...]
                          for i in range(n_chunks)], axis=0)
    b_all = sum(lax.dot_general(part, tri, (((1,), (1,)), ((), ())),
                                preferred_element_type=F32)
                for part in _split3(_log_sigmoid(gr)))
    b = jnp.concatenate([pltpu.roll(b_all[i * 2 * H:(i + 1) * 2 * H], H, axis=0)
                         for i in range(n_chunks)], axis=0)
    b = jnp.where(head_row, b, 0.0)
    u = jnp.where(head_row, gr - b, 0.0)
    cm = _prefix_max_lanes(u)
    u_max = jnp.broadcast_to(cm[:, CHUNK - 1:CHUNK], cm.shape)
    g_tot = jnp.broadcast_to(b[:, CHUNK - 1:CHUNK], b.shape)
    m_prev = jnp.zeros((2 * H, CHUNK), F32)
    m_prevs = []
    for i in range(n_chunks):
        rs = slice(i * 2 * H, (i + 1) * 2 * H)
        m_prevs.append(m_prev)
        m_prev = g_tot[rs] + jnp.maximum(m_prev, u_max[rs])
    m_prev = jnp.concatenate(m_prevs, axis=0)
    mm = jnp.maximum(m_prev, u_max)
    sp = jnp.exp(m_prev - mm)
    sl = jnp.exp(u_max - mm)
    wa = jnp.exp(u - u_max)
    m_in = jnp.maximum(m_prev, cm)
    e_w = jnp.exp(m_prev - m_in)
    e_m = jnp.exp(-(b + m_in))
    per = 2 * H * ML_CHUNKS_PER_STEP
    assert 4 * per == LANES
    pad = jnp.zeros((per, CHUNK), F32)
    for s in range(n_rows // per):
        rs = slice(s * per, (s + 1) * per)
        cols_ref[s] = jnp.concatenate([m_in[rs], e_w[rs], e_m[rs], pad], axis=0).T
        rows_ref[s] = jnp.concatenate([u[rs], sp[rs], sl[rs], wa[rs]], axis=0)


def _mlstm_kernel(qk_ref, v_ref, og_ref, gt_ref, gbc_ref, tri_ref, cw_ref, cb_ref, mlg_ref,
                  out_ref, xbuf, ybuf, ct_s, nb_s, cols_s, rows_s):
    step = pl.program_id(1)
    E = LANES
    H = ML_HEADS
    n_slab, rows = qk_ref.shape[0], qk_ref.shape[1]
    half = rows // 2
    n_chunks = rows // CHUNK
    assert n_slab == 2 * H

    @pl.when(step == 0)
    def _():
        xbuf[:, 0:8, :] = jnp.zeros((n_slab, 8, LANES), F32)
        ct_s[...] = jnp.zeros(ct_s.shape, F32)
        nb_s[...] = jnp.zeros(nb_s.shape, F32)
        _gate_factors(gt_ref, gbc_ref, tri_ref, cols_s, rows_s)

    @pl.when(step > 0)
    def _():
        xbuf[:, 0:8, :] = xbuf[:, rows:rows + 8, :]

    xbuf[:, 8:8 + rows, :] = qk_ref[...]
    for c in range(n_slab):
        cl = slice(c * LANES, (c + 1) * LANES)
        for par in range(2):
            acc = cb_ref[:, cl]
            for j in range(CONV_K):
                off = 8 - (CONV_K - 1) + j + par
                acc = acc + xbuf[c, pl.ds(off, half, stride=2), :] * cw_ref[j:j + 1, cl]
            act = _silu(acc)
            if c >= H:
                act = act * (E ** -0.5)
            ybuf[c, pl.ds(par, half, stride=2), :] = act

    n_rows = 2 * H * n_chunks
    cols = cols_s[step]
    u = rows_s[step, 0:n_rows, :]
    sp_all = rows_s[step, n_rows:2 * n_rows, :]
    sl_all = rows_s[step, 2 * n_rows:3 * n_rows, :]
    wa_all = rows_s[step, 3 * n_rows:4 * n_rows, :]

    ti = lax.broadcasted_iota(jnp.int32, (CHUNK, CHUNK), 0)
    si = lax.broadcasted_iota(jnp.int32, (CHUNK, CHUNK), 1)
    causal = si <= ti
    ones = jnp.ones((CHUNK, E), BF16)

    cts = [ct_s[h] for h in range(H)]
    nbs = [nb_s[h] for h in range(H)]
    for i in range(n_chunks):
        r0 = i * CHUNK
        for h in range(H):
            row = i * 2 * H + h
            col = lambda q_, row=row: cols[:, q_ * n_rows + row:q_ * n_rows + row + 1]
            m_in_c, e_w, e_m = col(0), col(1), col(2)
            q = ybuf[h, r0:r0 + CHUNK, :]
            k = ybuf[H + h, r0:r0 + CHUNK, :]
            v1 = jnp.concatenate([v_ref[r0:r0 + CHUNK, h * E:(h + 1) * E], ones], axis=1)
            qb = q.astype(BF16)

            d_w = jnp.where(causal, jnp.exp(u[row:row + 1, :] - m_in_c), 0.0)
            s_qk = lax.dot_general(qb, k.astype(BF16), (((1,), (1,)), ((), ())),
                                   preferred_element_type=F32) * d_w
            st = jnp.concatenate([cts[h].astype(BF16), nbs[h].astype(BF16)], axis=1)
            inter = jnp.dot(qb, st, preferred_element_type=F32)
            intra = jnp.dot(s_qk.astype(BF16), v1, preferred_element_type=F32)
            num = e_w * inter[:, :E] + intra[:, :E]
            den = e_w * inter[:, E:] + intra[:, E:]
            hh = num / jnp.maximum(jnp.abs(den), e_m)

            hg = og_ref[r0:r0 + CHUNK, h * E:(h + 1) * E] * hh
            mu = jnp.mean(hg, -1, keepdims=True)
            ex2 = jnp.mean(hg * hg, -1, keepdims=True)
            hc = hg - mu
            var = jnp.maximum(ex2 - mu * mu, 0.0)
            yn = hc * lax.rsqrt(var + LN_EPS) * mlg_ref[:, h * E:(h + 1) * E]
            out_ref[r0:r0 + CHUNK, h * E:(h + 1) * E] = yn.astype(out_ref.dtype)

            kw = (k.T * wa_all[row:row + 1, :]).astype(BF16)
            loc = jnp.dot(kw, v1, preferred_element_type=F32)
            sp_h, sl_h = sp_all[row:row + 1, :], sl_all[row:row + 1, :]
            cts[h] = sp_h * cts[h] + sl_h * loc[:, :E]
            nbs[h] = sp_h * nbs[h] + sl_h * loc[:, E:]
    for h in range(H):
        ct_s[h] = cts[h]
        nb_s[h] = nbs[h]


def _mlstm(qk, vm, og, gatest, gb_col, conv_w, conv_b, ml_g, B, S):
    rows = ML_CHUNKS_PER_STEP * CHUNK
    ns = S // rows
    W = vm.shape[1]
    tri = jnp.tril(jnp.ones((CHUNK, CHUNK), F32)).astype(BF16)
    row = lambda w: pl.BlockSpec((rows, w), lambda b, c: (b * ns + c, 0))
    consts = (gb_col, tri, conv_w, conv_b, ml_g)
    return pl.pallas_call(
        _mlstm_kernel,
        grid=(B, ns),
        in_specs=[pl.BlockSpec((qk.shape[0], rows, LANES), lambda b, c: (0, b * ns + c, 0)),
                  row(W), row(W),
                  pl.BlockSpec((gatest.shape[0], S), lambda b, c: (0, b))]
                 + [_const_spec(a.shape) for a in consts],
        out_specs=row(W),
        out_shape=jax.ShapeDtypeStruct((B * S, W), BF16),
        scratch_shapes=[pltpu.VMEM((qk.shape[0], rows + 8, LANES), F32),
                        pltpu.VMEM((qk.shape[0], rows, LANES), F32),
                        pltpu.VMEM((ML_HEADS, LANES, LANES), F32),
                        pltpu.VMEM((ML_HEADS, LANES, LANES), F32),
                        pltpu.VMEM((ns, LANES, CHUNK), F32),
                        pltpu.VMEM((ns, LANES, CHUNK), F32)],
        compiler_params=pltpu.CompilerParams(
            dimension_semantics=("parallel", "arbitrary"), vmem_limit_bytes=VMEM_LIMIT),
        name="mlstm",
    )(qk, vm, og, gatest, *consts)


def _tail_kernel(x1_ref, att_ref, hm_ref, kt_ref, v_ref, wout_ref, wq_ref, wo_ref,
                 wg_ref, wu_ref, wd_ref, g_ref, b_ref, out_ref):
    n_sub = x1_ref.shape[0] // TAIL_SUB_ROWS
    tiles = [slice(t * TAIL_SUB_ROWS, (t + 1) * TAIL_SUB_ROWS) for t in range(n_sub)]
    n_a = att_ref.shape[1]
    hd = wq_ref.shape[1] // XA_HEADS

    def mix(rs):
        return (jnp.dot(att_ref[rs, :], wout_ref[0:n_a, :], preferred_element_type=F32)
                + jnp.dot(hm_ref[rs, :], wout_ref[n_a:, :], preferred_element_type=F32))

    def cross(x2):
        q = jnp.dot(x2.astype(BF16), wq_ref[...], preferred_element_type=F32)
        heads = []
        for h in range(XA_HEADS):
            qh = q[:, h * hd:(h + 1) * hd].astype(BF16)
            lg = jnp.dot(qh, kt_ref[h * hd:(h + 1) * hd, :], preferred_element_type=F32)
            p = jnp.exp(lg - jnp.max(lg, -1, keepdims=True))
            s = jnp.sum(p, -1, keepdims=True)
            o = jnp.dot(p.astype(BF16), v_ref[:, h * hd:(h + 1) * hd],
                        preferred_element_type=F32)
            heads.append((o / s).astype(BF16))
        return jnp.dot(jnp.concatenate(heads, axis=1), wo_ref[...], preferred_element_type=F32)

    mixes = [mix(rs) for rs in tiles]
    x2s = [_layer_norm(ALPHA * x1_ref[rs, :] + m, g_ref[0:1, :], b_ref[0:1, :])
           for rs, m in zip(tiles, mixes)]
    xas = [cross(x2) for x2 in x2s]
    x3s = [_layer_norm(ALPHA * x2 + xa, g_ref[1:2, :], b_ref[1:2, :])
           for x2, xa in zip(x2s, xas)]
    ffs = [_swiglu(x3.astype(BF16), wg_ref, wu_ref, wd_ref) for x3 in x3s]
    for rs, x3, ff in zip(tiles, x3s, ffs):
        out_ref[rs, :] = _layer_norm(ALPHA * x3 + 0.5 * ff, g_ref[2:3, :], b_ref[2:3, :])


def _tail(x1, att, hm, kt, v, wout, wq, wo, wg, wu, wd, ffn_index, g, b, S):
    T, D = x1.shape
    tm = TAIL_ROW_TILE
    per_batch = S // tm
    row = lambda w: pl.BlockSpec((tm, w), lambda i: (i, 0))
    L = v.shape[1]
    weights = (wout, wq, wo, wg, wu, wd, g, b)
    return pl.pallas_call(
        _tail_kernel,
        grid=(T // tm,),
        in_specs=[row(D), row(att.shape[1]), row(hm.shape[1]),
                  pl.BlockSpec((None, D, L), lambda i: (i // per_batch, 0, 0)),
                  pl.BlockSpec((None, L, D), lambda i: (i // per_batch, 0, 0))]
                 + _weight_specs(weights, ffn_index),
        out_specs=row(D),
        out_shape=jax.ShapeDtypeStruct((T, D), F32),
        compiler_params=pltpu.CompilerParams(
            dimension_semantics=("parallel",), vmem_limit_bytes=VMEM_LIMIT),
        name="tail",
    )(x1, att, hm, kt, v, *weights)


def kernel(x, mem, rel_bias, ln_g, ln_b, ffn_w_gate, ffn_w_up, ffn_w_down, w_in, conv_w, conv_b,
           ig_bias, fg_bias, ml_norm_g, w_out, xq_w, xkv_w, xo_w):
    B, S, D = x.shape
    att_w = ATT_HEADS * ATT_HD
    ml_w = ML_HEADS * LANES
    bias = _bias_tables(rel_bias)
    xf = x.reshape(B * S, D)
    bf = lambda a: a.astype(BF16)
    wg_all, wu_all, wd_all = bf(ffn_w_gate), bf(ffn_w_up), bf(ffn_w_down)
    for l in range(DEPTH):
        wint = bf(w_in[l].T)
        gb = jnp.concatenate([ig_bias[l], fg_bias[l]]).astype(F32)

        x1, qa, kv1, kvf1, kvf2, qk, vm, og, gatest = _ffn_in(
            xf, wg_all, wu_all, wd_all, (l, 0),
            ln_g[l, 0][None], ln_b[l, 0][None], wint, 3 * att_w, 2 * ml_w, ml_w, S)
        att = _attention(qa, [kv1.reshape(B, S, -1), kvf1, kvf2], bias, B, S)
        att = att.reshape(B * S, att_w)
        hm = _mlstm(qk, vm, og, gatest, gb[:, None], conv_w[l], conv_b[l][None],
                    ml_norm_g[l][None], B, S)

        hd = D // XA_HEADS
        kt, v = _memkv(mem, bf(xkv_w[l]))
        xf = _tail(x1, att, hm, kt, v, bf(w_out[l]), bf(xq_w[l] * hd ** -0.5), bf(xo_w[l]),
                   wg_all, wu_all, wd_all, (l, 1), ln_g[l, 1:4], ln_b[l, 1:4], S)
    return xf.reshape(B, S, D)
```

```python
import functools
import math

import jax
import jax.numpy as jnp
from jax import lax
from jax.experimental import pallas as pl
from jax.experimental.pallas import tpu as pltpu

F32 = jnp.float32
BF16 = jnp.bfloat16

ATT_HD = 64
ATT_HEADS = 8
DILATED = ((128, 1), (512, 4), (2048, 16))
BLK = 128
ML_HEADS = 4
CHUNK = 128
CONV_K = 4
XA_HEADS = 4
REL_BUCKETS = 32
REL_MAX_DIST = 2048
DEPTH = 1
ALPHA = (2 * DEPTH) ** 0.25
LN_EPS = 1e-5
NEG = -1e30
LOG2E = math.log2(math.e)

LANES = 128
VMEM_LIMIT = 60 * 1024 * 1024
ROW_TILE = 512
ROW_SUBTILES = 2
TAIL_ROW_TILE = 1024
TAIL_SUB_ROWS = 512
ATT_ORDER = (2, 1, 0)
ATT_UNROLL = 32
ML_CHUNKS_PER_STEP = 4


def _const_spec(shape, lead=()):
    block = (None,) * len(lead) + tuple(shape[len(lead):])
    index = tuple(lead) + (0,) * (len(shape) - len(lead))
    return pl.BlockSpec(block, lambda *_: index, pipeline_mode=pl.Buffered(1))


def _weight_specs(weights, ffn_index):
    return [_const_spec(w.shape, ffn_index if w.ndim == 4 else ()) for w in weights]


def _layer_norm(y, g, b):
    mu = jnp.mean(y, -1, keepdims=True)
    yc = y - mu
    var = jnp.mean(yc * yc, -1, keepdims=True)
    return yc * lax.rsqrt(var + LN_EPS) * g + b


def _silu(x):
    return x * jax.nn.sigmoid(x)


def _swiglu(xb, wg_ref, wu_ref, wd_ref):
    g = jnp.dot(xb, wg_ref[...], preferred_element_type=F32)
    u = jnp.dot(xb, wu_ref[...], preferred_element_type=F32)
    h = (_silu(g) * u).astype(BF16)
    return jnp.dot(h, wd_ref[...], preferred_element_type=F32)


def _bias_kernel(rel_ref, bkt_ref, out_ref):
    bkt = bkt_ref[0]
    prev_half = lax.broadcasted_iota(jnp.int32, bkt.shape, 1) < BLK
    accs = [jnp.full(bkt.shape, NEG, F32) for _ in range(ATT_HEADS)]
    for b in range(REL_BUCKETS):
        hit = bkt == b
        for h in range(ATT_HEADS):
            accs[h] = jnp.where(hit, rel_ref[b, h] * LOG2E, accs[h])
    for h in range(ATT_HEADS):
        out_ref[0, 0, h] = accs[h]
        out_ref[1, 0, h] = jnp.where(prev_half, NEG, accs[h])


def _bias_tables(rel_bias):
    qi = jnp.arange(BLK)[:, None]
    ki = jnp.arange(2 * BLK)[None, :]
    off = qi + BLK - ki
    exact = REL_BUCKETS // 2
    n_log = REL_BUCKETS - exact
    starts = [math.ceil(exact * (REL_MAX_DIST / exact) ** (k / n_log)) for k in range(1, n_log)]
    tabs = []
    for window, dil in DILATED:
        n_keys = window // dil
        dist = dil * jnp.clip(off, 0, n_keys)
        large = exact + sum((dist >= s).astype(jnp.int32) for s in starts)
        bucket = jnp.where(dist < exact, dist, large)
        band = (off >= 0) & (off <= n_keys)
        tabs.append(jnp.where(band, bucket, -1))
    bkt = jnp.stack(tabs, 0).astype(jnp.int32)
    nd = len(DILATED)
    return pl.pallas_call(
        _bias_kernel,
        grid=(nd,),
        in_specs=[pl.BlockSpec(memory_space=pltpu.SMEM),
                  pl.BlockSpec((1, BLK, 2 * BLK), lambda d: (d, 0, 0))],
        out_specs=pl.BlockSpec((2, 1, ATT_HEADS, BLK, 2 * BLK), lambda d: (0, d, 0, 0, 0)),
        out_shape=jax.ShapeDtypeStruct((2, nd, ATT_HEADS, BLK, 2 * BLK), F32),
        name="bias",
    )(rel_bias.astype(F32), bkt)


def _memkv_kernel(mem_ref, wkv_ref, kt_ref, v_ref):
    D = mem_ref.shape[1]
    mb = mem_ref[...].astype(BF16)
    k = jnp.dot(mb, wkv_ref[:, 0:D], preferred_element_type=F32)
    kt_ref[...] = k.T.astype(BF16)
    v_ref[...] = jnp.dot(mb, wkv_ref[:, D:], preferred_element_type=F32).astype(BF16)


def _memkv(mem, wkv):
    B, L, D = mem.shape
    return pl.pallas_call(
        _memkv_kernel,
        grid=(B,),
        in_specs=[pl.BlockSpec((None, L, D), lambda b: (b, 0, 0)),
                  _const_spec((D, 2 * D))],
        out_specs=[pl.BlockSpec((None, D, L), lambda b: (b, 0, 0)),
                   pl.BlockSpec((None, L, D), lambda b: (b, 0, 0))],
        out_shape=[jax.ShapeDtypeStruct((B, D, L), BF16),
                   jax.ShapeDtypeStruct((B, L, D), BF16)],
        compiler_params=pltpu.CompilerParams(vmem_limit_bytes=VMEM_LIMIT),
        name="memkv",
    )(mem, wkv)


def _fold_rows(kv, t, s1, s2, fold_refs):
    (d1, ref1), (d2, ref2) = fold_refs
    ratio = d2 // d1
    sub = kv.shape[0]
    n1, n2 = sub // d1, sub // d2
    for c in range(kv.shape[1] // LANES):
        cl = slice(c * LANES, (c + 1) * LANES)
        s1[c] = kv[:, cl]
        for r1 in range(d1):
            p1 = s1[c, pl.ds(r1, n1, stride=d1), :]
            ref1[r1, t * n1:(t + 1) * n1, cl] = p1.astype(BF16)
            s2[c, r1 * n1:(r1 + 1) * n1, :] = p1
        for r1 in range(d1):
            for rp in range(ratio):
                p2 = s2[c, pl.ds(r1 * n1 + rp, n2, stride=ratio), :]
                ref2[d1 * rp + r1, t * n2:(t + 1) * n2, cl] = p2.astype(BF16)


def _ffn_in_kernel(x_ref, wg_ref, wu_ref, wd_ref, g_ref, b_ref, wint_ref,
                   x1_ref, qa_ref, kv1_ref, kvf1_ref, kvf2_ref, qk_ref, vm_ref, og_ref,
                   gatest_ref, s1, s2):
    n_q = qa_ref.shape[1]
    c1 = n_q + kv1_ref.shape[1]
    c2 = c1 + qk_ref.shape[0] * LANES
    c3 = c2 + vm_ref.shape[1]
    c4 = c3 + og_ref.shape[1]
    nt = (((1,), (1,)), ((), ()))
    sub = x_ref.shape[0] // ROW_SUBTILES
    tiles = [slice(t * sub, (t + 1) * sub) for t in range(ROW_SUBTILES)]
    ffs = [_swiglu(x_ref[rs, :].astype(BF16), wg_ref, wu_ref, wd_ref) for rs in tiles]
    fold_dils = [d for _, d in DILATED if d > 1]
    for t, (rs, ff) in enumerate(zip(tiles, ffs)):
        x1 = _layer_norm(ALPHA * x_ref[rs, :] + 0.5 * ff, g_ref[...], b_ref[...])
        x1_ref[rs, :] = x1
        xb = x1.astype(BF16)
        proj = lambda lo, hi, xb=xb: lax.dot_general(xb, wint_ref[lo:hi, :], nt,
                                                     preferred_element_type=F32)

        qkm = proj(c1, c2)
        for c in range(qk_ref.shape[0]):
            qk_ref[c, rs, :] = qkm[:, c * LANES:(c + 1) * LANES]

        qa_ref[rs, :] = proj(0, n_q) * (ATT_HD ** -0.5 * LOG2E)
        kv = proj(n_q, c1)
        kv1_ref[rs, :] = kv.astype(BF16)
        _fold_rows(kv, t, s1, s2, list(zip(fold_dils, (kvf1_ref, kvf2_ref))))
        vm_ref[rs, :] = proj(c2, c3).astype(BF16)
        og_ref[rs, :] = jax.nn.sigmoid(proj(c3, c4))
        gatest_ref[:, rs] = lax.dot_general(wint_ref[c4:c4 + gatest_ref.shape[0], :], xb, nt,
                                            preferred_element_type=F32)


def _ffn_in(x2d, wg, wu, wd, ffn_index, g, b, wint, n_att, n_qkm, n_ml, S):
    T, D = x2d.shape
    tm = ROW_TILE
    tps = S // tm
    row = lambda w: pl.BlockSpec((tm, w), lambda i: (i, 0))
    n_g = wint.shape[0] - n_att - n_qkm - 2 * n_ml
    n_slab = n_qkm // LANES
    n_q, n_kv = n_att // 3, 2 * n_att // 3
    fold = lambda d: pl.BlockSpec((None, d, tm // d, n_kv), lambda i: (i // tps, 0, i % tps, 0))
    fold_shape = lambda d: jax.ShapeDtypeStruct((T // S, d, S // d, n_kv), BF16)
    d1, d2 = [d for _, d in DILATED if d > 1]
    assert d2 % d1 == 0 and (tm // ROW_SUBTILES) % (16 * d2) == 0
    weights = (wg, wu, wd, g, b, wint)
    sub = tm // ROW_SUBTILES
    return pl.pallas_call(
        _ffn_in_kernel,
        grid=(T // tm,),
        in_specs=[row(D)] + _weight_specs(weights, ffn_index),
        out_specs=[row(D), row(n_q), row(n_kv), fold(d1), fold(d2),
                   pl.BlockSpec((n_slab, tm, LANES), lambda i: (0, i, 0)),
                   row(n_ml), row(n_ml),
                   pl.BlockSpec((n_g, tm), lambda i: (0, i))],
        out_shape=[jax.ShapeDtypeStruct((T, D), F32),
                   jax.ShapeDtypeStruct((T, n_q), F32),
                   jax.ShapeDtypeStruct((T, n_kv), BF16),
                   fold_shape(d1), fold_shape(d2),
                   jax.ShapeDtypeStruct((n_slab, T, LANES), F32),
                   jax.ShapeDtypeStruct((T, n_ml), BF16),
                   jax.ShapeDtypeStruct((T, n_ml), F32),
                   jax.ShapeDtypeStruct((n_g, T), F32)],
        scratch_shapes=[pltpu.VMEM((n_kv // LANES, sub, LANES), F32)] * 2,
        compiler_params=pltpu.CompilerParams(
            dimension_semantics=("parallel",), vmem_limit_bytes=VMEM_LIMIT),
        name="ffn_in",
    )(x2d, *weights)


def _attn_block(q_ref, kv_refs, bias_ref, out_ref, acc_s, m_s, l_s, j, step):
    di = ATT_ORDER[step]
    dil = DILATED[di][1]
    k_ref, v_ref = kv_refs[di]
    lane = lax.broadcasted_iota(jnp.int32, (BLK, LANES), 1)
    head0 = lane < ATT_HD
    r = j % dil
    n = j // dil

    def rows_of(start, stride):
        if stride == 1:
            return pl.ds(pl.multiple_of(start, BLK), BLK)
        return pl.ds(start, BLK, stride=stride)

    cur = rows_of(n * BLK, 1)
    prev = rows_of(jnp.maximum(n - 1, 0) * BLK, 1)
    if dil == 1:
        take = lambda ref, rows: ref[rows, :]
    else:
        take = lambda ref, rows: ref[r, rows, :]
    rows = rows_of(r + n * (BLK * dil), dil)
    q = q_ref[rows, :]
    kk = jnp.concatenate([take(k_ref, prev), take(k_ref, cur)], axis=0)
    vv = jnp.concatenate([take(v_ref, prev), take(v_ref, cur)], axis=0)
    first = jnp.where(n == 0, 1, 0)
    ms, ls, os_ = [], [], []
    for h in range(2):
        sel = head0 if h == 0 else jnp.logical_not(head0)
        qh = jnp.where(sel, q, 0.0).astype(BF16)
        lg = lax.dot_general(qh, kk, (((1,), (1,)), ((), ())), preferred_element_type=F32)
        lg = lg + bias_ref[first, di, h]
        mh = jnp.max(lg, -1, keepdims=True)
        p = jnp.exp2(lg - mh)
        ls.append(jnp.sum(p, -1, keepdims=True))
        ms.append(mh)
        os_.append(jnp.dot(p.astype(BF16), vv, preferred_element_type=F32))
    m_b = jnp.where(head0, ms[0], ms[1])
    l_b = jnp.where(head0, ls[0], ls[1])
    o_b = jnp.where(head0, os_[0], os_[1])
    if step > 0:
        m_o = m_s[rows, :]
        m_n = jnp.maximum(m_o, m_b)
        e_o = jnp.exp2(m_o - m_n)
        e_b = jnp.exp2(m_b - m_n)
        l_b = l_s[rows, :] * e_o + l_b * e_b
        o_b = acc_s[rows, :] * e_o + o_b * e_b
        m_b = m_n
    if step < len(ATT_ORDER) - 1:
        m_s[rows, :] = m_b
        l_s[rows, :] = l_b
        acc_s[rows, :] = o_b
    else:
        out_ref[rows, :] = (o_b / l_b).astype(out_ref.dtype)


def _attn_kernel(q_ref, *refs):
    n_br = len(DILATED)
    kv_refs = [(refs[2 * i], refs[2 * i + 1]) for i in range(n_br)]
    bias_ref, out_ref, acc_s, m_s, l_s = refs[2 * n_br:]
    n_blocks = q_ref.shape[0] // BLK
    for step in range(len(ATT_ORDER)):
        def body(i, _, step=step):
            for u in range(ATT_UNROLL):
                _attn_block(q_ref, kv_refs, bias_ref, out_ref, acc_s, m_s, l_s,
                            i * ATT_UNROLL + u, step)
            return 0
        lax.fori_loop(0, n_blocks // ATT_UNROLL, body, 0)


def _attention(q, kvs, bias, B, S):
    n_pairs = ATT_HEADS * ATT_HD // LANES
    dils = [d for _, d in DILATED]
    assert dils[ATT_ORDER[-1]] == 1
    in_specs = [pl.BlockSpec((None, S, LANES), lambda b, p: (b, 0, p))]
    operands = [q.reshape(B, S, q.shape[-1])]
    for d, kv in zip(dils, kvs):
        for off in (0, n_pairs):
            if d == 1:
                in_specs.append(pl.BlockSpec((None, S, LANES),
                                             lambda b, p, off=off: (b, 0, off + p)))
            else:
                in_specs.append(pl.BlockSpec((None, d, S // d, LANES),
                                             lambda b, p, off=off: (b, 0, 0, off + p)))
            operands.append(kv)
    in_specs.append(pl.BlockSpec((2, len(DILATED), 2, BLK, 2 * BLK),
                                 lambda b, p: (0, 0, p, 0, 0)))
    return pl.pallas_call(
        _attn_kernel,
        grid=(B, n_pairs),
        in_specs=in_specs,
        out_specs=pl.BlockSpec((None, S, LANES), lambda b, p: (b, 0, p)),
        out_shape=jax.ShapeDtypeStruct((B, S, n_pairs * LANES), BF16),
        scratch_shapes=[pltpu.VMEM((S, LANES), F32)] * 3,
        compiler_params=pltpu.CompilerParams(
            dimension_semantics=("parallel", "parallel"), vmem_limit_bytes=VMEM_LIMIT),
        name="attn",
    )(*operands, bias)


def _split3(a):
    hi = a.astype(BF16)
    r1 = a - hi.astype(F32)
    mid = r1.astype(BF16)
    lo = (r1 - mid.astype(F32)).astype(BF16)
    return hi, mid, lo


def _log_sigmoid(x):
    return jnp.minimum(x, 0.0) - jnp.log1p(jnp.exp(-jnp.abs(x)))


def _prefix_max_lanes(x):
    lane = lax.broadcasted_iota(jnp.int32, x.shape, 1)
    sh = 1
    while sh < x.shape[1]:
        x = jnp.maximum(x, jnp.where(lane >= sh, pltpu.roll(x, sh, axis=1), -jnp.inf))
        sh *= 2
    return x


def _gate_factors(gt_ref, gbc_ref, tri_ref, cols_ref, rows_ref):
    H = ML_HEADS
    n_chunks = gt_ref.shape[1] // CHUNK
    n_rows = 2 * H * n_chunks
    tri = tri_ref[...]
    head_row = (lax.broadcasted_iota(jnp.int32, (n_rows, CHUNK), 0) & (2 * H - 1)) < H
    gr = jnp.concatenate([gt_ref[:, i * CHUNK:(i + 1) * CHUNK] + gbc_ref[...]
                          for i in range(n_chunks)], axis=0)
    b_all = sum(lax.dot_general(part, tri, (((1,), (1,)), ((), ())),
                                preferred_element_type=F32)
                for part in _split3(_log_sigmoid(gr)))
    b = jnp.concatenate([pltpu.roll(b_all[i * 2 * H:(i + 1) * 2 * H], H, axis=0)
                         for i in range(n_chunks)], axis=0)
    b = jnp.where(head_row, b, 0.0)
    u = jnp.where(head_row, gr - b, 0.0)
    cm = _prefix_max_lanes(u)
    u_max = jnp.broadcast_to(cm[:, CHUNK - 1:CHUNK], cm.shape)
    g_tot = jnp.broadcast_to(b[:, CHUNK - 1:CHUNK], b.shape)
    m_prev = jnp.zeros((2 * H, CHUNK), F32)
    m_prevs = []
    for i in range(n_chunks):
        rs = slice(i * 2 * H, (i + 1) * 2 * H)
        m_prevs.append(m_prev)
        m_prev = g_tot[rs] + jnp.maximum(m_prev, u_max[rs])
    m_prev = jnp.concatenate(m_prevs, axis=0)
    mm = jnp.maximum(m_prev, u_max)
    sp = jnp.exp(m_prev - mm)
    sl = jnp.exp(u_max - mm)
    wa = jnp.exp(u - u_max)
    m_in = jnp.maximum(m_prev, cm)
    e_w = jnp.exp(m_prev - m_in)
    e_m = jnp.exp(-(b + m_in))
    per = 2 * H * ML_CHUNKS_PER_STEP
    assert 4 * per == LANES
    pad = jnp.zeros((per, CHUNK), F32)
    for s in range(n_rows // per):
        rs = slice(s * per, (s + 1) * per)
        cols_ref[s] = jnp.concatenate([m_in[rs], e_w[rs], e_m[rs], pad], axis=0).T
        rows_ref[s] = jnp.concatenate([u[rs], sp[rs], sl[rs], wa[rs]], axis=0)


def _mlstm_kernel(qk_ref, v_ref, og_ref, gt_ref, gbc_ref, tri_ref, cw_ref, cb_ref, mlg_ref,
                  out_ref, xbuf, ybuf, ct_s, nb_s, cols_s, rows_s):
    step = pl.program_id(0)
    E = LANES
    H = ML_HEADS
    n_slab, n_seq, rows = qk_ref.shape[0], qk_ref.shape[1], qk_ref.shape[2]
    half = rows // 2
    n_chunks = rows // CHUNK
    assert n_slab == 2 * H

    @pl.when(step == 0)
    def _():
        xbuf[:, :, 0:8, :] = jnp.zeros((n_slab, n_seq, 8, LANES), F32)
        ct_s[...] = jnp.zeros(ct_s.shape, F32)
        nb_s[...] = jnp.zeros(nb_s.shape, F32)
        for sq in range(n_seq):
            _gate_factors(gt_ref.at[sq], gbc_ref, tri_ref, cols_s.at[sq], rows_s.at[sq])

    @pl.when(step > 0)
    def _():
        xbuf[:, :, 0:8, :] = xbuf[:, :, rows:rows + 8, :]

    xbuf[:, :, 8:8 + rows, :] = qk_ref[...]
    for sq in range(n_seq):
        for c in range(n_slab):
            cl = slice(c * LANES, (c + 1) * LANES)
            for par in range(2):
                acc = cb_ref[:, cl]
                for j in range(CONV_K):
                    off = 8 - (CONV_K - 1) + j + par
                    acc = acc + xbuf[c, sq, pl.ds(off, half, stride=2), :] * cw_ref[j:j + 1, cl]
                act = _silu(acc)
                if c >= H:
                    act = act * (E ** -0.5)
                ybuf[c, sq, pl.ds(par, half, stride=2), :] = act

    n_rows = 2 * H * n_chunks
    cols = [cols_s[sq, step] for sq in range(n_seq)]
    fac = lambda sq, q_: rows_s[sq, step, q_ * n_rows:(q_ + 1) * n_rows, :]
    u = [fac(sq, 0) for sq in range(n_seq)]
    sp_all = [fac(sq, 1) for sq in range(n_seq)]
    sl_all = [fac(sq, 2) for sq in range(n_seq)]
    wa_all = [fac(sq, 3) for sq in range(n_seq)]

    ti = lax.broadcasted_iota(jnp.int32, (CHUNK, CHUNK), 0)
    si = lax.broadcasted_iota(jnp.int32, (CHUNK, CHUNK), 1)
    causal = si <= ti
    ones = jnp.ones((CHUNK, E), BF16)

    cts = [[ct_s[sq, h] for h in range(H)] for sq in range(n_seq)]
    nbs = [[nb_s[sq, h] for h in range(H)] for sq in range(n_seq)]
    for i in range(n_chunks):
        r0 = i * CHUNK
        for sq in range(n_seq):
            for h in range(H):
                row = i * 2 * H + h
                col = lambda q_, row=row, sq=sq: cols[sq][:, q_ * n_rows + row:
                                                          q_ * n_rows + row + 1]
                m_in_c, e_w, e_m = col(0), col(1), col(2)
                q = ybuf[h, sq, r0:r0 + CHUNK, :]
                k = ybuf[H + h, sq, r0:r0 + CHUNK, :]
                v1 = jnp.concatenate([v_ref[sq, r0:r0 + CHUNK, h * E:(h + 1) * E], ones],
                                     axis=1)
                qb = q.astype(BF16)

                d_w = jnp.where(causal, jnp.exp(u[sq][row:row + 1, :] - m_in_c), 0.0)
                s_qk = lax.dot_general(qb, k.astype(BF16), (((1,), (1,)), ((), ())),
                                       preferred_element_type=F32) * d_w
                st = jnp.concatenate([cts[sq][h].astype(BF16), nbs[sq][h].astype(BF16)], axis=1)
                inter = jnp.dot(qb, st, preferred_element_type=F32)
                intra = jnp.dot(s_qk.astype(BF16), v1, preferred_element_type=F32)
                num = e_w * inter[:, :E] + intra[:, :E]
                den = e_w * inter[:, E:] + intra[:, E:]
                hh = num / jnp.maximum(jnp.abs(den), e_m)

                hg = og_ref[sq, r0:r0 + CHUNK, h * E:(h + 1) * E] * hh
                mu = jnp.mean(hg, -1, keepdims=True)
                ex2 = jnp.mean(hg * hg, -1, keepdims=True)
                hc = hg - mu
                var = jnp.maximum(ex2 - mu * mu, 0.0)
                yn = hc * lax.rsqrt(var + LN_EPS) * mlg_ref[:, h * E:(h + 1) * E]
                out_ref[sq, r0:r0 + CHUNK, h * E:(h + 1) * E] = yn.astype(out_ref.dtype)

                kw = (k.T * wa_all[sq][row:row + 1, :]).astype(BF16)
                loc = jnp.dot(kw, v1, preferred_element_type=F32)
                sp_h, sl_h = sp_all[sq][row:row + 1, :], sl_all[sq][row:row + 1, :]
                cts[sq][h] = sp_h * cts[sq][h] + sl_h * loc[:, :E]
                nbs[sq][h] = sp_h * nbs[sq][h] + sl_h * loc[:, E:]
    for sq in range(n_seq):
        for h in range(H):
            ct_s[sq, h] = cts[sq][h]
            nb_s[sq, h] = nbs[sq][h]


def _mlstm(qk, vm, og, gatest, gb_col, conv_w, conv_b, ml_g, B, S):
    rows = ML_CHUNKS_PER_STEP * CHUNK
    ns = S // rows
    W = vm.shape[1]
    n_slab, G = qk.shape[0], gatest.shape[0]
    tri = jnp.tril(jnp.ones((CHUNK, CHUNK), F32)).astype(BF16)
    seq = lambda w: pl.BlockSpec((B, rows, w), lambda c: (0, c, 0))
    consts = (gb_col, tri, conv_w, conv_b, ml_g)
    out = pl.pallas_call(
        _mlstm_kernel,
        grid=(ns,),
        in_specs=[pl.BlockSpec((n_slab, B, rows, LANES), lambda c: (0, 0, c, 0)),
                  seq(W), seq(W),
                  pl.BlockSpec((B, G, S), lambda c: (0, 0, 0))]
                 + [_const_spec(a.shape) for a in consts],
        out_specs=seq(W),
        out_shape=jax.ShapeDtypeStruct((B, S, W), BF16),
        scratch_shapes=[pltpu.VMEM((n_slab, B, rows + 8, LANES), F32),
                        pltpu.VMEM((n_slab, B, rows, LANES), F32),
                        pltpu.VMEM((B, ML_HEADS, LANES, LANES), F32),
                        pltpu.VMEM((B, ML_HEADS, LANES, LANES), F32),
                        pltpu.VMEM((B, ns, LANES, CHUNK), F32),
                        pltpu.VMEM((B, ns, LANES, CHUNK), F32)],
        compiler_params=pltpu.CompilerParams(
            dimension_semantics=("arbitrary",), vmem_limit_bytes=VMEM_LIMIT),
        name="mlstm",
    )(qk.reshape(n_slab, B, S, LANES), vm.reshape(B, S, W),
      og.reshape(B, S, W), jnp.swapaxes(gatest.reshape(G, B, S), 0, 1), *consts)
    return out.reshape(B * S, W)


def _tail_kernel(x1_ref, att_ref, hm_ref, kt_ref, v_ref, wout_ref, wq_ref, wo_ref,
                 wg_ref, wu_ref, wd_ref, g_ref, b_ref, out_ref):
    n_sub = x1_ref.shape[0] // TAIL_SUB_ROWS
    tiles = [slice(t * TAIL_SUB_ROWS, (t + 1) * TAIL_SUB_ROWS) for t in range(n_sub)]
    n_a = att_ref.shape[1]
    hd = wq_ref.shape[1] // XA_HEADS

    def mix(rs):
        return (jnp.dot(att_ref[rs, :], wout_ref[0:n_a, :], preferred_element_type=F32)
                + jnp.dot(hm_ref[rs, :], wout_ref[n_a:, :], preferred_element_type=F32))

    def cross(x2):
        q = jnp.dot(x2.astype(BF16), wq_ref[...], preferred_element_type=F32)
        heads = []
        for h in range(XA_HEADS):
            qh = q[:, h * hd:(h + 1) * hd].astype(BF16)
            lg = jnp.dot(qh, kt_ref[h * hd:(h + 1) * hd, :], preferred_element_type=F32)
            p = jnp.exp(lg - jnp.max(lg, -1, keepdims=True))
            s = jnp.sum(p, -1, keepdims=True)
            o = jnp.dot(p.astype(BF16), v_ref[:, h * hd:(h + 1) * hd],
                        preferred_element_type=F32)
            heads.append((o / s).astype(BF16))
        return jnp.dot(jnp.concatenate(heads, axis=1), wo_ref[...], preferred_element_type=F32)

    mixes = [mix(rs) for rs in tiles]
    x2s = [_layer_norm(ALPHA * x1_ref[rs, :] + m, g_ref[0:1, :], b_ref[0:1, :])
           for rs, m in zip(tiles, mixes)]
    xas = [cross(x2) for x2 in x2s]
    x3s = [_layer_norm(ALPHA * x2 + xa, g_ref[1:2, :], b_ref[1:2, :])
           for x2, xa in zip(x2s, xas)]
    ffs = [_swiglu(x3.astype(BF16), wg_ref, wu_ref, wd_ref) for x3 in x3s]
    for rs, x3, ff in zip(tiles, x3s, ffs):
        out_ref[rs, :] = _layer_norm(ALPHA * x3 + 0.5 * ff, g_ref[2:3, :], b_ref[2:3, :])


def _tail(x1, att, hm, kt, v, wout, wq, wo, wg, wu, wd, ffn_index, g, b, S):
    T, D = x1.shape
    tm = TAIL_ROW_TILE
    per_batch = S // tm
    row = lambda w: pl.BlockSpec((tm, w), lambda i: (i, 0))
    L = v.shape[1]
    weights = (wout, wq, wo, wg, wu, wd, g, b)
    return pl.pallas_call(
        _tail_kernel,
        grid=(T // tm,),
        in_specs=[row(D), row(att.shape[1]), row(hm.shape[1]),
                  pl.BlockSpec((None, D, L), lambda i: (i // per_batch, 0, 0)),
                  pl.BlockSpec((None, L, D), lambda i: (i // per_batch, 0, 0))]
                 + _weight_specs(weights, ffn_index),
        out_specs=row(D),
        out_shape=jax.ShapeDtypeStruct((T, D), F32),
        compiler_params=pltpu.CompilerParams(
            dimension_semantics=("parallel",), vmem_limit_bytes=VMEM_LIMIT),
        name="tail",
    )(x1, att, hm, kt, v, *weights)


def kernel(x, mem, rel_bias, ln_g, ln_b, ffn_w_gate, ffn_w_up, ffn_w_down, w_in, conv_w, conv_b,
           ig_bias, fg_bias, ml_norm_g, w_out, xq_w, xkv_w, xo_w):
    B, S, D = x.shape
    att_w = ATT_HEADS * ATT_HD
    ml_w = ML_HEADS * LANES
    bias = _bias_tables(rel_bias)
    xf = x.reshape(B * S, D)
    bf = lambda a: a.astype(BF16)
    wg_all, wu_all, wd_all = bf(ffn_w_gate), bf(ffn_w_up), bf(ffn_w_down)
    for l in range(DEPTH):
        wint = bf(w_in[l].T)
        gb = jnp.concatenate([ig_bias[l], fg_bias[l]]).astype(F32)

        x1, qa, kv1, kvf1, kvf2, qk, vm, og, gatest = _ffn_in(
            xf, wg_all, wu_all, wd_all, (l, 0),
            ln_g[l, 0][None], ln_b[l, 0][None], wint, 3 * att_w, 2 * ml_w, ml_w, S)
        att = _attention(qa, [kv1.reshape(B, S, -1), kvf1, kvf2], bias, B, S)
        att = att.reshape(B * S, att_w)
        hm = _mlstm(qk, vm, og, gatest, gb[:, None], conv_w[l], conv_b[l][None],
                    ml_norm_g[l][None], B, S)

        hd = D // XA_HEADS
        kt, v = _memkv(mem, bf(xkv_w[l]))
        xf = _tail(x1, att, hm, kt, v, bf(w_out[l]), bf(xq_w[l] * hd ** -0.5), bf(xo_w[l]),
                   wg_all, wu_all, wd_all, (l, 1), ln_g[l, 1:4], ln_b[l, 1:4], S)
    return xf.reshape(B, S, D)
```

```python
import functools
import math

import jax
import jax.numpy as jnp
from jax import lax
from jax.experimental import pallas as pl
from jax.experimental.pallas import tpu as pltpu

F32 = jnp.float32
BF16 = jnp.bfloat16

ATT_HD = 64
ATT_HEADS = 8
DILATED = ((128, 1), (512, 4), (2048, 16))
BLK = 128
ML_HEADS = 4
CHUNK = 128
CONV_K = 4
XA_HEADS = 4
REL_BUCKETS = 32
REL_MAX_DIST = 2048
DEPTH = 1
ALPHA = (2 * DEPTH) ** 0.25
LN_EPS = 1e-5
NEG = -1e30
LOG2E = math.log2(math.e)

LANES = 128
SUBLANES = 8
VMEM_LIMIT = 60 * 1024 * 1024
ROW_TILE = 512
ROW_SUBTILES = 2
TAIL_ROW_TILE = 1024
TAIL_SUB_ROWS = 512
ATT_ORDER = (2, 1, 0)
ATT_UNROLL = 32
ML_CHUNKS_PER_STEP = 4


def _const_spec(shape, lead=()):
    block = (None,) * len(lead) + tuple(shape[len(lead):])
    index = tuple(lead) + (0,) * (len(shape) - len(lead))
    return pl.BlockSpec(block, lambda *_: index, pipeline_mode=pl.Buffered(1))


def _weight_specs(weights, ffn_index):
    return [_const_spec(w.shape, ffn_index if w.ndim == 4 else ()) for w in weights]


def _layer_norm(y, g, b):
    mu = jnp.mean(y, -1, keepdims=True)
    yc = y - mu
    var = jnp.mean(yc * yc, -1, keepdims=True)
    return yc * lax.rsqrt(var + LN_EPS) * g + b


def _silu(x):
    return x * jax.nn.sigmoid(x)


def _swiglu(xb, wg_ref, wu_ref, wd_ref):
    g = jnp.dot(xb, wg_ref[...], preferred_element_type=F32)
    u = jnp.dot(xb, wu_ref[...], preferred_element_type=F32)
    h = (_silu(g) * u).astype(BF16)
    return jnp.dot(h, wd_ref[...], preferred_element_type=F32)


def _bias_kernel(rel_ref, bkt_ref, out_ref):
    bkt = bkt_ref[0]
    prev_half = lax.broadcasted_iota(jnp.int32, bkt.shape, 1) < BLK
    accs = [jnp.full(bkt.shape, NEG, F32) for _ in range(ATT_HEADS)]
    for b in range(REL_BUCKETS):
        hit = bkt == b
        for h in range(ATT_HEADS):
            accs[h] = jnp.where(hit, rel_ref[b, h] * LOG2E, accs[h])
    for h in range(ATT_HEADS):
        out_ref[0, 0, h] = accs[h]
        out_ref[1, 0, h] = jnp.where(prev_half, NEG, accs[h])


def _bias_tables(rel_bias):
    qi = jnp.arange(BLK)[:, None]
    ki = jnp.arange(2 * BLK)[None, :]
    off = qi + BLK - ki
    exact = REL_BUCKETS // 2
    n_log = REL_BUCKETS - exact
    starts = [math.ceil(exact * (REL_MAX_DIST / exact) ** (k / n_log)) for k in range(1, n_log)]
    tabs = []
    for window, dil in DILATED:
        n_keys = window // dil
        dist = dil * jnp.clip(off, 0, n_keys)
        large = exact + sum((dist >= s).astype(jnp.int32) for s in starts)
        bucket = jnp.where(dist < exact, dist, large)
        band = (off >= 0) & (off <= n_keys)
        tabs.append(jnp.where(band, bucket, -1))
    bkt = jnp.stack(tabs, 0).astype(jnp.int32)
    nd = len(DILATED)
    return pl.pallas_call(
        _bias_kernel,
        grid=(nd,),
        in_specs=[pl.BlockSpec(memory_space=pltpu.SMEM),
                  pl.BlockSpec((1, BLK, 2 * BLK), lambda d: (d, 0, 0))],
        out_specs=pl.BlockSpec((2, 1, ATT_HEADS, BLK, 2 * BLK), lambda d: (0, d, 0, 0, 0)),
        out_shape=jax.ShapeDtypeStruct((2, nd, ATT_HEADS, BLK, 2 * BLK), F32),
        name="bias",
    )(rel_bias.astype(F32), bkt)


def _memkv_kernel(mem_ref, wkv_ref, kt_ref, v_ref):
    D = mem_ref.shape[1]
    mb = mem_ref[...].astype(BF16)
    k = jnp.dot(mb, wkv_ref[:, 0:D], preferred_element_type=F32)
    kt_ref[...] = k.T.astype(BF16)
    v_ref[...] = jnp.dot(mb, wkv_ref[:, D:], preferred_element_type=F32).astype(BF16)


def _memkv(mem, wkv):
    B, L, D = mem.shape
    return pl.pallas_call(
        _memkv_kernel,
        grid=(B,),
        in_specs=[pl.BlockSpec((None, L, D), lambda b: (b, 0, 0)),
                  _const_spec((D, 2 * D))],
        out_specs=[pl.BlockSpec((None, D, L), lambda b: (b, 0, 0)),
                   pl.BlockSpec((None, L, D), lambda b: (b, 0, 0))],
        out_shape=[jax.ShapeDtypeStruct((B, D, L), BF16),
                   jax.ShapeDtypeStruct((B, L, D), BF16)],
        compiler_params=pltpu.CompilerParams(vmem_limit_bytes=VMEM_LIMIT),
        name="memkv",
    )(mem, wkv)


def _fold_rows(kv, t, s1, s2, fold_refs):
    (d1, ref1), (d2, ref2) = fold_refs
    ratio = d2 // d1
    sub = kv.shape[0]
    n1, n2 = sub // d1, sub // d2
    for c in range(kv.shape[1] // LANES):
        cl = slice(c * LANES, (c + 1) * LANES)
        s1[c] = kv[:, cl]
        for r1 in range(d1):
            p1 = s1[c, pl.ds(r1, n1, stride=d1), :]
            ref1[r1, t * n1:(t + 1) * n1, cl] = p1.astype(BF16)
            s2[c, r1 * n1:(r1 + 1) * n1, :] = p1
        for r1 in range(d1):
            for rp in range(ratio):
                p2 = s2[c, pl.ds(r1 * n1 + rp, n2, stride=ratio), :]
                ref2[d1 * rp + r1, t * n2:(t + 1) * n2, cl] = p2.astype(BF16)


def _ffn_in_kernel(x_ref, wg_ref, wu_ref, wd_ref, g_ref, b_ref, wint_ref,
                   x1_ref, qa_ref, kv1_ref, kvf1_ref, kvf2_ref, qk_ref, vm_ref, og_ref,
                   gatest_ref, s1, s2):
    n_q = qa_ref.shape[1]
    c1 = n_q + kv1_ref.shape[1]
    c2 = c1 + qk_ref.shape[0] * LANES
    c3 = c2 + vm_ref.shape[1]
    c4 = c3 + og_ref.shape[1]
    nt = (((1,), (1,)), ((), ()))
    sub = x_ref.shape[0] // ROW_SUBTILES
    tiles = [slice(t * sub, (t + 1) * sub) for t in range(ROW_SUBTILES)]
    ffs = [_swiglu(x_ref[rs, :].astype(BF16), wg_ref, wu_ref, wd_ref) for rs in tiles]
    fold_dils = [d for _, d in DILATED if d > 1]
    for t, (rs, ff) in enumerate(zip(tiles, ffs)):
        x1 = _layer_norm(ALPHA * x_ref[rs, :] + 0.5 * ff, g_ref[...], b_ref[...])
        x1_ref[rs, :] = x1
        xb = x1.astype(BF16)
        proj = lambda lo, hi, xb=xb: lax.dot_general(xb, wint_ref[lo:hi, :], nt,
                                                     preferred_element_type=F32)

        qkm = proj(c1, c2)
        for c in range(qk_ref.shape[0]):
            qk_ref[c, rs, :] = qkm[:, c * LANES:(c + 1) * LANES]

        qa_ref[rs, :] = proj(0, n_q) * (ATT_HD ** -0.5 * LOG2E)
        kv = proj(n_q, c1)
        kv1_ref[rs, :] = kv.astype(BF16)
        _fold_rows(kv, t, s1, s2, list(zip(fold_dils, (kvf1_ref, kvf2_ref))))
        vm_ref[rs, :] = proj(c2, c3).astype(BF16)
        og_ref[rs, :] = jax.nn.sigmoid(proj(c3, c4))
        gatest_ref[:, rs] = lax.dot_general(wint_ref[c4:c4 + gatest_ref.shape[0], :], xb, nt,
                                            preferred_element_type=F32)


def _ffn_in(x2d, wg, wu, wd, ffn_index, g, b, wint, n_att, n_qkm, n_ml, S):
    T, D = x2d.shape
    tm = ROW_TILE
    tps = S // tm
    row = lambda w: pl.BlockSpec((tm, w), lambda i: (i, 0))
    n_g = wint.shape[0] - n_att - n_qkm - 2 * n_ml
    n_slab = n_qkm // LANES
    n_q, n_kv = n_att // 3, 2 * n_att // 3
    fold = lambda d: pl.BlockSpec((None, d, tm // d, n_kv), lambda i: (i // tps, 0, i % tps, 0))
    fold_shape = lambda d: jax.ShapeDtypeStruct((T // S, d, S // d, n_kv), BF16)
    d1, d2 = [d for _, d in DILATED if d > 1]
    assert d2 % d1 == 0 and (tm // ROW_SUBTILES) % (2 * SUBLANES * d2) == 0
    weights = (wg, wu, wd, g, b, wint)
    sub = tm // ROW_SUBTILES
    return pl.pallas_call(
        _ffn_in_kernel,
        grid=(T // tm,),
        in_specs=[row(D)] + _weight_specs(weights, ffn_index),
        out_specs=[row(D), row(n_q), row(n_kv), fold(d1), fold(d2),
                   pl.BlockSpec((n_slab, tm, LANES), lambda i: (0, i, 0)),
                   row(n_ml), row(n_ml),
                   pl.BlockSpec((n_g, tm), lambda i: (0, i))],
        out_shape=[jax.ShapeDtypeStruct((T, D), F32),
                   jax.ShapeDtypeStruct((T, n_q), F32),
                   jax.ShapeDtypeStruct((T, n_kv), BF16),
                   fold_shape(d1), fold_shape(d2),
                   jax.ShapeDtypeStruct((n_slab, T, LANES), F32),
                   jax.ShapeDtypeStruct((T, n_ml), BF16),
                   jax.ShapeDtypeStruct((T, n_ml), F32),
                   jax.ShapeDtypeStruct((n_g, T), F32)],
        scratch_shapes=[pltpu.VMEM((n_kv // LANES, sub, LANES), F32)] * 2,
        compiler_params=pltpu.CompilerParams(
            dimension_semantics=("parallel",), vmem_limit_bytes=VMEM_LIMIT),
        name="ffn_in",
    )(x2d, *weights)


def _attn_block(q_ref, kv_refs, bias_ref, out_ref, acc_s, m_s, l_s, j, step):
    di = ATT_ORDER[step]
    dil = DILATED[di][1]
    k_ref, v_ref = kv_refs[di]
    lane = lax.broadcasted_iota(jnp.int32, (BLK, LANES), 1)
    head0 = lane < ATT_HD
    r = j % dil
    n = j // dil

    def rows_of(start, stride):
        if stride == 1:
            return pl.ds(pl.multiple_of(start, BLK), BLK)
        return pl.ds(start, BLK, stride=stride)

    cur = rows_of(n * BLK, 1)
    prev = rows_of(jnp.maximum(n - 1, 0) * BLK, 1)
    if dil == 1:
        take = lambda ref, rows: ref[rows, :]
    else:
        take = lambda ref, rows: ref[r, rows, :]
    rows = rows_of(r + n * (BLK * dil), dil)
    q = q_ref[rows, :]
    kk = jnp.concatenate([take(k_ref, prev), take(k_ref, cur)], axis=0)
    vv = jnp.concatenate([take(v_ref, prev), take(v_ref, cur)], axis=0)
    first = jnp.where(n == 0, 1, 0)
    ms, ls, os_ = [], [], []
    for h in range(2):
        sel = head0 if h == 0 else jnp.logical_not(head0)
        qh = jnp.where(sel, q, 0.0).astype(BF16)
        lg = lax.dot_general(qh, kk, (((1,), (1,)), ((), ())), preferred_element_type=F32)
        lg = lg + bias_ref[first, di, h]
        mh = jnp.max(lg, -1, keepdims=True)
        p = jnp.exp2(lg - mh)
        ls.append(jnp.sum(p, -1, keepdims=True))
        ms.append(mh)
        os_.append(jnp.dot(p.astype(BF16), vv, preferred_element_type=F32))
    m_b = jnp.where(head0, ms[0], ms[1])
    l_b = jnp.where(head0, ls[0], ls[1])
    o_b = jnp.where(head0, os_[0], os_[1])
    if step > 0:
        m_o = m_s[rows, :]
        m_n = jnp.maximum(m_o, m_b)
        e_o = jnp.exp2(m_o - m_n)
        e_b = jnp.exp2(m_b - m_n)
        l_b = l_s[rows, :] * e_o + l_b * e_b
        o_b = acc_s[rows, :] * e_o + o_b * e_b
        m_b = m_n
    if step < len(ATT_ORDER) - 1:
        m_s[rows, :] = m_b
        l_s[rows, :] = l_b
        acc_s[rows, :] = o_b
    else:
        out_ref[rows, :] = (o_b / l_b).astype(out_ref.dtype)


def _attn_kernel(q_ref, *refs):
    n_br = len(DILATED)
    kv_refs = [(refs[2 * i], refs[2 * i + 1]) for i in range(n_br)]
    bias_ref, out_ref, acc_s, m_s, l_s = refs[2 * n_br:]
    n_blocks = q_ref.shape[0] // BLK
    for step in range(len(ATT_ORDER)):
        def body(i, _, step=step):
            for u in range(ATT_UNROLL):
                _attn_block(q_ref, kv_refs, bias_ref, out_ref, acc_s, m_s, l_s,
                            i * ATT_UNROLL + u, step)
            return 0
        lax.fori_loop(0, n_blocks // ATT_UNROLL, body, 0)


def _attention(q, kvs, bias, B, S):
    n_pairs = ATT_HEADS * ATT_HD // LANES
    dils = [d for _, d in DILATED]
    assert dils[ATT_ORDER[-1]] == 1
    in_specs = [pl.BlockSpec((None, S, LANES), lambda b, p: (b, 0, p))]
    operands = [q.reshape(B, S, q.shape[-1])]
    for d, kv in zip(dils, kvs):
        for off in (0, n_pairs):
            if d == 1:
                in_specs.append(pl.BlockSpec((None, S, LANES),
                                             lambda b, p, off=off: (b, 0, off + p)))
            else:
                in_specs.append(pl.BlockSpec((None, d, S // d, LANES),
                                             lambda b, p, off=off: (b, 0, 0, off + p)))
            operands.append(kv)
    in_specs.append(pl.BlockSpec((2, len(DILATED), 2, BLK, 2 * BLK),
                                 lambda b, p: (0, 0, p, 0, 0)))
    return pl.pallas_call(
        _attn_kernel,
        grid=(B, n_pairs),
        in_specs=in_specs,
        out_specs=pl.BlockSpec((None, S, LANES), lambda b, p: (b, 0, p)),
        out_shape=jax.ShapeDtypeStruct((B, S, n_pairs * LANES), BF16),
        scratch_shapes=[pltpu.VMEM((S, LANES), F32)] * 3,
        compiler_params=pltpu.CompilerParams(
            dimension_semantics=("parallel", "parallel"), vmem_limit_bytes=VMEM_LIMIT),
        name="attn",
    )(*operands, bias)


def _split3(a):
    hi = a.astype(BF16)
    r1 = a - hi.astype(F32)
    mid = r1.astype(BF16)
    lo = (r1 - mid.astype(F32)).astype(BF16)
    return hi, mid, lo


def _log_sigmoid(x):
    return jnp.minimum(x, 0.0) - jnp.log1p(jnp.exp(-jnp.abs(x)))


def _prefix_max_lanes(x):
    lane = lax.broadcasted_iota(jnp.int32, x.shape, 1)
    sh = 1
    while sh < x.shape[1]:
        x = jnp.maximum(x, jnp.where(lane >= sh, pltpu.roll(x, sh, axis=1), -jnp.inf))
        sh *= 2
    return x


def _gate_factors(gt_ref, gbc_ref, tri_ref, cols_ref, rows_ref):
    H = ML_HEADS
    n_chunks = gt_ref.shape[1] // CHUNK
    n_rows = 2 * H * n_chunks
    tri = tri_ref[...]
    head_row = (lax.broadcasted_iota(jnp.int32, (n_rows, CHUNK), 0) & (2 * H - 1)) < H
    gr = jnp.concatenate([gt_ref[:, i * CHUNK:(i + 1) * CHUNK] + gbc_ref[...]
                          for i in range(n_chunks)], axis=0)
    b_all = sum(lax.dot_general(part, tri, (((1,), (1,)), ((), ())),
                                preferred_element_type=F32)
                for part in _split3(_log_sigmoid(gr)))
    b = jnp.concatenate([pltpu.roll(b_all[i * 2 * H:(i + 1) * 2 * H], H, axis=0)
                         for i in range(n_chunks)], axis=0)
    b = jnp.where(head_row, b, 0.0)
    u = jnp.where(head_row, gr - b, 0.0)
    cm = _prefix_max_lanes(u)
    u_max = jnp.broadcast_to(cm[:, CHUNK - 1:CHUNK], cm.shape)
    g_tot = jnp.broadcast_to(b[:, CHUNK - 1:CHUNK], b.shape)
    m_prev = jnp.zeros((2 * H, CHUNK), F32)
    m_prevs = []
    for i in range(n_chunks):
        rs = slice(i * 2 * H, (i + 1) * 2 * H)
        m_prevs.append(m_prev)
        m_prev = g_tot[rs] + jnp.maximum(m_prev, u_max[rs])
    m_prev = jnp.concatenate(m_prevs, axis=0)
    mm = jnp.maximum(m_prev, u_max)
    sp = jnp.exp(m_prev - mm)
    sl = jnp.exp(u_max - mm)
    wa = jnp.exp(u - u_max)
    m_in = jnp.maximum(m_prev, cm)
    e_w = jnp.exp(m_prev - m_in)
    e_m = jnp.exp(-(b + m_in))
    per = 2 * H * ML_CHUNKS_PER_STEP
    assert 4 * per == LANES
    pad = jnp.zeros((per, CHUNK), F32)
    for s in range(n_rows // per):
        rs = slice(s * per, (s + 1) * per)
        cols_ref[s] = jnp.concatenate([m_in[rs], e_w[rs], e_m[rs], pad], axis=0).T
        rows_ref[s] = jnp.concatenate([u[rs], sp[rs], sl[rs], wa[rs]], axis=0)


def _mlstm_kernel(qk_ref, v_ref, og_ref, gt_ref, gbc_ref, tri_ref, cw_ref, cb_ref, mlg_ref,
                  out_ref, xbuf, ybuf, ct_s, nb_s, cols_s, rows_s):
    step = pl.program_id(0)
    E = LANES
    H = ML_HEADS
    n_slab, n_seq, rows = qk_ref.shape[0], qk_ref.shape[1], qk_ref.shape[2]
    half = rows // 2
    n_chunks = rows // CHUNK
    assert n_slab == 2 * H

    @pl.when(step == 0)
    def _():
        xbuf[:, :, 0:SUBLANES, :] = jnp.zeros((n_slab, n_seq, SUBLANES, LANES), F32)
        ct_s[...] = jnp.zeros(ct_s.shape, F32)
        nb_s[...] = jnp.zeros(nb_s.shape, F32)
        for sq in range(n_seq):
            _gate_factors(gt_ref.at[sq], gbc_ref, tri_ref, cols_s.at[sq], rows_s.at[sq])

    @pl.when(step > 0)
    def _():
        xbuf[:, :, 0:SUBLANES, :] = xbuf[:, :, rows:rows + SUBLANES, :]

    xbuf[:, :, SUBLANES:SUBLANES + rows, :] = qk_ref[...]
    for sq in range(n_seq):
        for c in range(n_slab):
            cl = slice(c * LANES, (c + 1) * LANES)
            for par in range(2):
                acc = cb_ref[:, cl]
                for j in range(CONV_K):
                    off = SUBLANES - (CONV_K - 1) + j + par
                    acc = acc + xbuf[c, sq, pl.ds(off, half, stride=2), :] * cw_ref[j:j + 1, cl]
                act = _silu(acc)
                if c >= H:
                    act = act * (E ** -0.5)
                ybuf[c, sq, pl.ds(par, half, stride=2), :] = act

    n_rows = 2 * H * n_chunks
    cols = [cols_s[sq, step] for sq in range(n_seq)]
    fac = lambda sq, q_: rows_s[sq, step, q_ * n_rows:(q_ + 1) * n_rows, :]
    u = [fac(sq, 0) for sq in range(n_seq)]
    sp_all = [fac(sq, 1) for sq in range(n_seq)]
    sl_all = [fac(sq, 2) for sq in range(n_seq)]
    wa_all = [fac(sq, 3) for sq in range(n_seq)]

    ti = lax.broadcasted_iota(jnp.int32, (CHUNK, CHUNK), 0)
    si = lax.broadcasted_iota(jnp.int32, (CHUNK, CHUNK), 1)
    causal = si <= ti
    ones = jnp.ones((CHUNK, E), BF16)

    cts = [[ct_s[sq, h] for h in range(H)] for sq in range(n_seq)]
    nbs = [[nb_s[sq, h] for h in range(H)] for sq in range(n_seq)]
    for i in range(n_chunks):
        r0 = i * CHUNK
        for sq in range(n_seq):
            for h in range(H):
                row = i * 2 * H + h
                col = lambda q_, row=row, sq=sq: cols[sq][:, q_ * n_rows + row:
                                                          q_ * n_rows + row + 1]
                m_in_c, e_w, e_m = col(0), col(1), col(2)
                q = ybuf[h, sq, r0:r0 + CHUNK, :]
                k = ybuf[H + h, sq, r0:r0 + CHUNK, :]
                v1 = jnp.concatenate([v_ref[sq, r0:r0 + CHUNK, h * E:(h + 1) * E], ones],
                                     axis=1)
                qb = q.astype(BF16)

                d_w = jnp.where(causal, jnp.exp(u[sq][row:row + 1, :] - m_in_c), 0.0)
                s_qk = lax.dot_general(qb, k.astype(BF16), (((1,), (1,)), ((), ())),
                                       preferred_element_type=F32) * d_w
                st = jnp.concatenate([cts[sq][h].astype(BF16), nbs[sq][h].astype(BF16)], axis=1)
                inter = jnp.dot(qb, st, preferred_element_type=F32)
                intra = jnp.dot(s_qk.astype(BF16), v1, preferred_element_type=F32)
                num = e_w * inter[:, :E] + intra[:, :E]
                den = e_w * inter[:, E:] + intra[:, E:]
                hh = num / jnp.maximum(jnp.abs(den), e_m)

                hg = og_ref[sq, r0:r0 + CHUNK, h * E:(h + 1) * E] * hh
                mu = jnp.mean(hg, -1, keepdims=True)
                ex2 = jnp.mean(hg * hg, -1, keepdims=True)
                hc = hg - mu
                var = jnp.maximum(ex2 - mu * mu, 0.0)
                yn = hc * lax.rsqrt(var + LN_EPS) * mlg_ref[:, h * E:(h + 1) * E]
                out_ref[sq, r0:r0 + CHUNK, h * E:(h + 1) * E] = yn.astype(out_ref.dtype)

                kw = (k.T * wa_all[sq][row:row + 1, :]).astype(BF16)
                loc = jnp.dot(kw, v1, preferred_element_type=F32)
                sp_h, sl_h = sp_all[sq][row:row + 1, :], sl_all[sq][row:row + 1, :]
                cts[sq][h] = sp_h * cts[sq][h] + sl_h * loc[:, :E]
                nbs[sq][h] = sp_h * nbs[sq][h] + sl_h * loc[:, E:]
    for sq in range(n_seq):
        for h in range(H):
            ct_s[sq, h] = cts[sq][h]
            nb_s[sq, h] = nbs[sq][h]


def _mlstm(qk, vm, og, gatest, gb_col, conv_w, conv_b, ml_g, B, S):
    rows = ML_CHUNKS_PER_STEP * CHUNK
    ns = S // rows
    W = vm.shape[1]
    n_slab, G = qk.shape[0], gatest.shape[0]
    tri = jnp.tril(jnp.ones((CHUNK, CHUNK), F32)).astype(BF16)
    seq = lambda w: pl.BlockSpec((B, rows, w), lambda c: (0, c, 0))
    consts = (gb_col, tri, conv_w, conv_b, ml_g)
    out = pl.pallas_call(
        _mlstm_kernel,
        grid=(ns,),
        in_specs=[pl.BlockSpec((n_slab, B, rows, LANES), lambda c: (0, 0, c, 0)),
                  seq(W), seq(W),
                  pl.BlockSpec((B, G, S), lambda c: (0, 0, 0))]
                 + [_const_spec(a.shape) for a in consts],
        out_specs=seq(W),
        out_shape=jax.ShapeDtypeStruct((B, S, W), BF16),
        scratch_shapes=[pltpu.VMEM((n_slab, B, rows + SUBLANES, LANES), F32),
                        pltpu.VMEM((n_slab, B, rows, LANES), F32),
                        pltpu.VMEM((B, ML_HEADS, LANES, LANES), F32),
                        pltpu.VMEM((B, ML_HEADS, LANES, LANES), F32),
                        pltpu.VMEM((B, ns, LANES, CHUNK), F32),
                        pltpu.VMEM((B, ns, LANES, CHUNK), F32)],
        compiler_params=pltpu.CompilerParams(
            dimension_semantics=("arbitrary",), vmem_limit_bytes=VMEM_LIMIT),
        name="mlstm",
    )(qk.reshape(n_slab, B, S, LANES), vm.reshape(B, S, W),
      og.reshape(B, S, W), jnp.swapaxes(gatest.reshape(G, B, S), 0, 1), *consts)
    return out.reshape(B * S, W)


def _tail_kernel(x1_ref, att_ref, hm_ref, kt_ref, v_ref, wout_ref, wq_ref, wo_ref,
                 wg_ref, wu_ref, wd_ref, g_ref, b_ref, out_ref):
    n_sub = x1_ref.shape[0] // TAIL_SUB_ROWS
    tiles = [slice(t * TAIL_SUB_ROWS, (t + 1) * TAIL_SUB_ROWS) for t in range(n_sub)]
    n_a = att_ref.shape[1]
    hd = wq_ref.shape[1] // XA_HEADS

    def mix(rs):
        return (jnp.dot(att_ref[rs, :], wout_ref[0:n_a, :], preferred_element_type=F32)
                + jnp.dot(hm_ref[rs, :], wout_ref[n_a:, :], preferred_element_type=F32))

    def cross(x2):
        q = jnp.dot(x2.astype(BF16), wq_ref[...], preferred_element_type=F32)
        heads = []
        for h in range(XA_HEADS):
            qh = q[:, h * hd:(h + 1) * hd].astype(BF16)
            lg = jnp.dot(qh, kt_ref[h * hd:(h + 1) * hd, :], preferred_element_type=F32)
            p = jnp.exp(lg - jnp.max(lg, -1, keepdims=True))
            s = jnp.sum(p, -1, keepdims=True)
            o = jnp.dot(p.astype(BF16), v_ref[:, h * hd:(h + 1) * hd],
                        preferred_element_type=F32)
            heads.append((o / s).astype(BF16))
        return jnp.dot(jnp.concatenate(heads, axis=1), wo_ref[...], preferred_element_type=F32)

    mixes = [mix(rs) for rs in tiles]
    x2s = [_layer_norm(ALPHA * x1_ref[rs, :] + m, g_ref[0:1, :], b_ref[0:1, :])
           for rs, m in zip(tiles, mixes)]
    xas = [cross(x2) for x2 in x2s]
    x3s = [_layer_norm(ALPHA * x2 + xa, g_ref[1:2, :], b_ref[1:2, :])
           for x2, xa in zip(x2s, xas)]
    ffs = [_swiglu(x3.astype(BF16), wg_ref, wu_ref, wd_ref) for x3 in x3s]
    for rs, x3, ff in zip(tiles, x3s, ffs):
        out_ref[rs, :] = _layer_norm(ALPHA * x3 + 0.5 * ff, g_ref[2:3, :], b_ref[2:3, :])


def _tail(x1, att, hm, kt, v, wout, wq, wo, wg, wu, wd, ffn_index, g, b, S):
    T, D = x1.shape
    tm = TAIL_ROW_TILE
    per_batch = S // tm
    row = lambda w: pl.BlockSpec((tm, w), lambda i: (i, 0))
    L = v.shape[1]
    weights = (wout, wq, wo, wg, wu, wd, g, b)
    return pl.pallas_call(
        _tail_kernel,
        grid=(T // tm,),
        in_specs=[row(D), row(att.shape[1]), row(hm.shape[1]),
                  pl.BlockSpec((None, D, L), lambda i: (i // per_batch, 0, 0)),
                  pl.BlockSpec((None, L, D), lambda i: (i // per_batch, 0, 0))]
                 + _weight_specs(weights, ffn_index),
        out_specs=row(D),
        out_shape=jax.ShapeDtypeStruct((T, D), F32),
        compiler_params=pltpu.CompilerParams(
            dimension_semantics=("parallel",), vmem_limit_bytes=VMEM_LIMIT),
        name="tail",
    )(x1, att, hm, kt, v, *weights)


def kernel(x, mem, rel_bias, ln_g, ln_b, ffn_w_gate, ffn_w_up, ffn_w_down, w_in, conv_w, conv_b,
           ig_bias, fg_bias, ml_norm_g, w_out, xq_w, xkv_w, xo_w):
    B, S, D = x.shape
    att_w = ATT_HEADS * ATT_HD
    ml_w = ML_HEADS * LANES
    bias = _bias_tables(rel_bias)
    xf = x.reshape(B * S, D)
    bf = lambda a: a.astype(BF16)
    wg_all, wu_all, wd_all = bf(ffn_w_gate), bf(ffn_w_up), bf(ffn_w_down)
    for l in range(DEPTH):
        wint = bf(w_in[l].T)
        gb = jnp.concatenate([ig_bias[l], fg_bias[l]]).astype(F32)

        x1, qa, kv1, kvf1, kvf2, qk, vm, og, gatest = _ffn_in(
            xf, wg_all, wu_all, wd_all, (l, 0),
            ln_g[l, 0][None], ln_b[l, 0][None], wint, 3 * att_w, 2 * ml_w, ml_w, S)
        att = _attention(qa, [kv1.reshape(B, S, -1), kvf1, kvf2], bias, B, S)
        att = att.reshape(B * S, att_w)
        hm = _mlstm(qk, vm, og, gatest, gb[:, None], conv_w[l], conv_b[l][None],
                    ml_norm_g[l][None], B, S)

        hd = D // XA_HEADS
        kt, v = _memkv(mem, bf(xkv_w[l]))
        xf = _tail(x1, att, hm, kt, v, bf(w_out[l]), bf(xq_w[l] * hd ** -0.5), bf(xo_w[l]),
                   wg_all, wu_all, wd_all, (l, 1), ln_g[l, 1:4], ln_b[l, 1:4], S)
    return xf.reshape(B, S, D)
```

```python
import functools
import math

import jax
import jax.numpy as jnp
from jax import lax
from jax.experimental import pallas as pl
from jax.experimental.pallas import tpu as pltpu

F32 = jnp.float32
BF16 = jnp.bfloat16

ATT_HD = 64
ATT_HEADS = 8
DILATED = ((128, 1), (512, 4), (2048, 16))
BLK = 128
ML_HEADS = 4
CHUNK = 128
CONV_K = 4
XA_HEADS = 4
REL_BUCKETS = 32
REL_MAX_DIST = 2048
DEPTH = 1
ALPHA = (2 * DEPTH) ** 0.25
LN_EPS = 1e-5
NEG = -1e30
LOG2E = math.log2(math.e)

LANES = 128
SUBLANES = 8
VMEM_LIMIT = 60 * 1024 * 1024
ROW_TILE = 512
ROW_SUBTILES = 2
TAIL_ROW_TILE = 1024
TAIL_SUB_ROWS = 512
ATT_ORDER = (2, 1, 0)
ATT_UNROLL = 32
ML_CHUNKS_PER_STEP = 8


def _const_spec(shape, lead=()):
    block = (None,) * len(lead) + tuple(shape[len(lead):])
    index = tuple(lead) + (0,) * (len(shape) - len(lead))
    return pl.BlockSpec(block, lambda *_: index, pipeline_mode=pl.Buffered(1))


def _weight_specs(weights, ffn_index):
    return [_const_spec(w.shape, ffn_index if w.ndim == 4 else ()) for w in weights]


def _layer_norm(y, g, b):
    mu = jnp.mean(y, -1, keepdims=True)
    yc = y - mu
    var = jnp.mean(yc * yc, -1, keepdims=True)
    return yc * lax.rsqrt(var + LN_EPS) * g + b


def _silu(x):
    return x * jax.nn.sigmoid(x)


def _swiglu(xb, wg_ref, wu_ref, wd_ref):
    g = jnp.dot(xb, wg_ref[...], preferred_element_type=F32)
    u = jnp.dot(xb, wu_ref[...], preferred_element_type=F32)
    h = (_silu(g) * u).astype(BF16)
    return jnp.dot(h, wd_ref[...], preferred_element_type=F32)


def _bias_kernel(rel_ref, bkt_ref, out_ref):
    bkt = bkt_ref[0]
    prev_half = lax.broadcasted_iota(jnp.int32, bkt.shape, 1) < BLK
    accs = [jnp.full(bkt.shape, NEG, F32) for _ in range(ATT_HEADS)]
    for b in range(REL_BUCKETS):
        hit = bkt == b
        for h in range(ATT_HEADS):
            accs[h] = jnp.where(hit, rel_ref[b, h] * LOG2E, accs[h])
    for h in range(ATT_HEADS):
        out_ref[0, 0, h] = accs[h]
        out_ref[1, 0, h] = jnp.where(prev_half, NEG, accs[h])


def _bias_tables(rel_bias):
    qi = jnp.arange(BLK)[:, None]
    ki = jnp.arange(2 * BLK)[None, :]
    off = qi + BLK - ki
    exact = REL_BUCKETS // 2
    n_log = REL_BUCKETS - exact
    starts = [math.ceil(exact * (REL_MAX_DIST / exact) ** (k / n_log)) for k in range(1, n_log)]
    tabs = []
    for window, dil in DILATED:
        n_keys = window // dil
        dist = dil * jnp.clip(off, 0, n_keys)
        large = exact + sum((dist >= s).astype(jnp.int32) for s in starts)
        bucket = jnp.where(dist < exact, dist, large)
        band = (off >= 0) & (off <= n_keys)
        tabs.append(jnp.where(band, bucket, -1))
    bkt = jnp.stack(tabs, 0).astype(jnp.int32)
    nd = len(DILATED)
    return pl.pallas_call(
        _bias_kernel,
        grid=(nd,),
        in_specs=[pl.BlockSpec(memory_space=pltpu.SMEM),
                  pl.BlockSpec((1, BLK, 2 * BLK), lambda d: (d, 0, 0))],
        out_specs=pl.BlockSpec((2, 1, ATT_HEADS, BLK, 2 * BLK), lambda d: (0, d, 0, 0, 0)),
        out_shape=jax.ShapeDtypeStruct((2, nd, ATT_HEADS, BLK, 2 * BLK), F32),
        name="bias",
    )(rel_bias.astype(F32), bkt)


def _memkv_kernel(mem_ref, wkv_ref, kt_ref, v_ref):
    D = mem_ref.shape[1]
    mb = mem_ref[...].astype(BF16)
    k = jnp.dot(mb, wkv_ref[:, 0:D], preferred_element_type=F32)
    kt_ref[...] = k.T.astype(BF16)
    v_ref[...] = jnp.dot(mb, wkv_ref[:, D:], preferred_element_type=F32).astype(BF16)


def _memkv(mem, wkv):
    B, L, D = mem.shape
    return pl.pallas_call(
        _memkv_kernel,
        grid=(B,),
        in_specs=[pl.BlockSpec((None, L, D), lambda b: (b, 0, 0)),
                  _const_spec((D, 2 * D))],
        out_specs=[pl.BlockSpec((None, D, L), lambda b: (b, 0, 0)),
                   pl.BlockSpec((None, L, D), lambda b: (b, 0, 0))],
        out_shape=[jax.ShapeDtypeStruct((B, D, L), BF16),
                   jax.ShapeDtypeStruct((B, L, D), BF16)],
        compiler_params=pltpu.CompilerParams(vmem_limit_bytes=VMEM_LIMIT),
        name="memkv",
    )(mem, wkv)


def _fold_rows(kv, t, s1, s2, fold_refs):
    (d1, ref1), (d2, ref2) = fold_refs
    ratio = d2 // d1
    sub = kv.shape[0]
    n1, n2 = sub // d1, sub // d2
    for c in range(kv.shape[1] // LANES):
        cl = slice(c * LANES, (c + 1) * LANES)
        s1[c] = kv[:, cl]
        for r1 in range(d1):
            p1 = s1[c, pl.ds(r1, n1, stride=d1), :]
            ref1[r1, t * n1:(t + 1) * n1, cl] = p1.astype(BF16)
            s2[c, r1 * n1:(r1 + 1) * n1, :] = p1
        for r1 in range(d1):
            for rp in range(ratio):
                p2 = s2[c, pl.ds(r1 * n1 + rp, n2, stride=ratio), :]
                ref2[d1 * rp + r1, t * n2:(t + 1) * n2, cl] = p2.astype(BF16)


def _ffn_in_kernel(x_ref, wg_ref, wu_ref, wd_ref, g_ref, b_ref, wint_ref,
                   x1_ref, qa_ref, kv1_ref, kvf1_ref, kvf2_ref, qk_ref, vm_ref, og_ref,
                   gatest_ref, s1, s2):
    n_q = qa_ref.shape[1]
    c1 = n_q + kv1_ref.shape[1]
    c2 = c1 + qk_ref.shape[0] * LANES
    c3 = c2 + vm_ref.shape[1]
    c4 = c3 + og_ref.shape[1]
    nt = (((1,), (1,)), ((), ()))
    sub = x_ref.shape[0] // ROW_SUBTILES
    tiles = [slice(t * sub, (t + 1) * sub) for t in range(ROW_SUBTILES)]
    ffs = [_swiglu(x_ref[rs, :].astype(BF16), wg_ref, wu_ref, wd_ref) for rs in tiles]
    fold_dils = [d for _, d in DILATED if d > 1]
    for t, (rs, ff) in enumerate(zip(tiles, ffs)):
        x1 = _layer_norm(ALPHA * x_ref[rs, :] + 0.5 * ff, g_ref[...], b_ref[...])
        x1_ref[rs, :] = x1
        xb = x1.astype(BF16)
        proj = lambda lo, hi, xb=xb: lax.dot_general(xb, wint_ref[lo:hi, :], nt,
                                                     preferred_element_type=F32)

        qkm = proj(c1, c2)
        for c in range(qk_ref.shape[0]):
            qk_ref[c, rs, :] = qkm[:, c * LANES:(c + 1) * LANES]

        qa_ref[rs, :] = proj(0, n_q) * (ATT_HD ** -0.5 * LOG2E)
        kv = proj(n_q, c1)
        kv1_ref[rs, :] = kv.astype(BF16)
        _fold_rows(kv, t, s1, s2, list(zip(fold_dils, (kvf1_ref, kvf2_ref))))
        vm_ref[rs, :] = proj(c2, c3).astype(BF16)
        og_ref[rs, :] = jax.nn.sigmoid(proj(c3, c4))
        gatest_ref[:, rs] = lax.dot_general(wint_ref[c4:c4 + gatest_ref.shape[0], :], xb, nt,
                                            preferred_element_type=F32)


def _ffn_in(x2d, wg, wu, wd, ffn_index, g, b, wint, n_att, n_qkm, n_ml, S):
    T, D = x2d.shape
    tm = ROW_TILE
    tps = S // tm
    row = lambda w: pl.BlockSpec((tm, w), lambda i: (i, 0))
    n_g = wint.shape[0] - n_att - n_qkm - 2 * n_ml
    n_slab = n_qkm // LANES
    n_q, n_kv = n_att // 3, 2 * n_att // 3
    fold = lambda d: pl.BlockSpec((None, d, tm // d, n_kv), lambda i: (i // tps, 0, i % tps, 0))
    fold_shape = lambda d: jax.ShapeDtypeStruct((T // S, d, S // d, n_kv), BF16)
    d1, d2 = [d for _, d in DILATED if d > 1]
    assert d2 % d1 == 0 and (tm // ROW_SUBTILES) % (2 * SUBLANES * d2) == 0
    weights = (wg, wu, wd, g, b, wint)
    sub = tm // ROW_SUBTILES
    return pl.pallas_call(
        _ffn_in_kernel,
        grid=(T // tm,),
        in_specs=[row(D)] + _weight_specs(weights, ffn_index),
        out_specs=[row(D), row(n_q), row(n_kv), fold(d1), fold(d2),
                   pl.BlockSpec((n_slab, tm, LANES), lambda i: (0, i, 0)),
                   row(n_ml), row(n_ml),
                   pl.BlockSpec((n_g, tm), lambda i: (0, i))],
        out_shape=[jax.ShapeDtypeStruct((T, D), F32),
                   jax.ShapeDtypeStruct((T, n_q), F32),
                   jax.ShapeDtypeStruct((T, n_kv), BF16),
                   fold_shape(d1), fold_shape(d2),
                   jax.ShapeDtypeStruct((n_slab, T, LANES), F32),
                   jax.ShapeDtypeStruct((T, n_ml), BF16),
                   jax.ShapeDtypeStruct((T, n_ml), F32),
                   jax.ShapeDtypeStruct((n_g, T), F32)],
        scratch_shapes=[pltpu.VMEM((n_kv // LANES, sub, LANES), F32)] * 2,
        compiler_params=pltpu.CompilerParams(
            dimension_semantics=("parallel",), vmem_limit_bytes=VMEM_LIMIT),
        name="ffn_in",
    )(x2d, *weights)


def _attn_block(q_ref, kv_refs, bias_ref, out_ref, acc_s, m_s, l_s, j, step):
    di = ATT_ORDER[step]
    dil = DILATED[di][1]
    k_ref, v_ref = kv_refs[di]
    lane = lax.broadcasted_iota(jnp.int32, (BLK, LANES), 1)
    head0 = lane < ATT_HD
    r = j % dil
    n = j // dil

    def rows_of(start, stride):
        if stride == 1:
            return pl.ds(pl.multiple_of(start, BLK), BLK)
        return pl.ds(start, BLK, stride=stride)

    cur = rows_of(n * BLK, 1)
    prev = rows_of(jnp.maximum(n - 1, 0) * BLK, 1)
    if dil == 1:
        take = lambda ref, rows: ref[rows, :]
    else:
        take = lambda ref, rows: ref[r, rows, :]
    rows = rows_of(r + n * (BLK * dil), dil)
    q = q_ref[rows, :]
    kk = jnp.concatenate([take(k_ref, prev), take(k_ref, cur)], axis=0)
    vv = jnp.concatenate([take(v_ref, prev), take(v_ref, cur)], axis=0)
    first = jnp.where(n == 0, 1, 0)
    ms, ls, os_ = [], [], []
    for h in range(2):
        sel = head0 if h == 0 else jnp.logical_not(head0)
        qh = jnp.where(sel, q, 0.0).astype(BF16)
        lg = lax.dot_general(qh, kk, (((1,), (1,)), ((), ())), preferred_element_type=F32)
        lg = lg + bias_ref[first, di, h]
        mh = jnp.max(lg, -1, keepdims=True)
        p = jnp.exp2(lg - mh)
        ls.append(jnp.sum(p, -1, keepdims=True))
        ms.append(mh)
        os_.append(jnp.dot(p.astype(BF16), vv, preferred_element_type=F32))
    m_b = jnp.where(head0, ms[0], ms[1])
    l_b = jnp.where(head0, ls[0], ls[1])
    o_b = jnp.where(head0, os_[0], os_[1])
    if step > 0:
        m_o = m_s[rows, :]
        m_n = jnp.maximum(m_o, m_b)
        e_o = jnp.exp2(m_o - m_n)
        e_b = jnp.exp2(m_b - m_n)
        l_b = l_s[rows, :] * e_o + l_b * e_b
        o_b = acc_s[rows, :] * e_o + o_b * e_b
        m_b = m_n
    if step < len(ATT_ORDER) - 1:
        m_s[rows, :] = m_b
        l_s[rows, :] = l_b
        acc_s[rows, :] = o_b
    else:
        out_ref[rows, :] = (o_b / l_b).astype(out_ref.dtype)


def _attn_kernel(q_ref, *refs):
    n_br = len(DILATED)
    kv_refs = [(refs[2 * i], refs[2 * i + 1]) for i in range(n_br)]
    bias_ref, out_ref, acc_s, m_s, l_s = refs[2 * n_br:]
    n_blocks = q_ref.shape[0] // BLK
    for step in range(len(ATT_ORDER)):
        def body(i, _, step=step):
            for u in range(ATT_UNROLL):
                _attn_block(q_ref, kv_refs, bias_ref, out_ref, acc_s, m_s, l_s,
                            i * ATT_UNROLL + u, step)
            return 0
        lax.fori_loop(0, n_blocks // ATT_UNROLL, body, 0)


def _attention(q, kvs, bias, B, S):
    n_pairs = ATT_HEADS * ATT_HD // LANES
    dils = [d for _, d in DILATED]
    assert dils[ATT_ORDER[-1]] == 1
    in_specs = [pl.BlockSpec((None, S, LANES), lambda b, p: (b, 0, p))]
    operands = [q.reshape(B, S, q.shape[-1])]
    for d, kv in zip(dils, kvs):
        for off in (0, n_pairs):
            if d == 1:
                in_specs.append(pl.BlockSpec((None, S, LANES),
                                             lambda b, p, off=off: (b, 0, off + p)))
            else:
                in_specs.append(pl.BlockSpec((None, d, S // d, LANES),
                                             lambda b, p, off=off: (b, 0, 0, off + p)))
            operands.append(kv)
    in_specs.append(pl.BlockSpec((2, len(DILATED), 2, BLK, 2 * BLK),
                                 lambda b, p: (0, 0, p, 0, 0)))
    return pl.pallas_call(
        _attn_kernel,
        grid=(B, n_pairs),
        in_specs=in_specs,
        out_specs=pl.BlockSpec((None, S, LANES), lambda b, p: (b, 0, p)),
        out_shape=jax.ShapeDtypeStruct((B, S, n_pairs * LANES), BF16),
        scratch_shapes=[pltpu.VMEM((S, LANES), F32)] * 3,
        compiler_params=pltpu.CompilerParams(
            dimension_semantics=("parallel", "parallel"), vmem_limit_bytes=VMEM_LIMIT),
        name="attn",
    )(*operands, bias)


def _split3(a):
    hi = a.astype(BF16)
    r1 = a - hi.astype(F32)
    mid = r1.astype(BF16)
    lo = (r1 - mid.astype(F32)).astype(BF16)
    return hi, mid, lo


def _log_sigmoid(x):
    return jnp.minimum(x, 0.0) - jnp.log1p(jnp.exp(-jnp.abs(x)))


def _prefix_max_lanes(x):
    lane = lax.broadcasted_iota(jnp.int32, x.shape, 1)
    sh = 1
    while sh < x.shape[1]:
        x = jnp.maximum(x, jnp.where(lane >= sh, pltpu.roll(x, sh, axis=1), -jnp.inf))
        sh *= 2
    return x


def _gate_factors(gt_ref, gbc_ref, tri_ref, cols_ref, rows_ref):
    H = ML_HEADS
    n_chunks = gt_ref.shape[1] // CHUNK
    n_rows = 2 * H * n_chunks
    tri = tri_ref[...]
    head_row = (lax.broadcasted_iota(jnp.int32, (n_rows, CHUNK), 0) & (2 * H - 1)) < H
    gr = jnp.concatenate([gt_ref[:, i * CHUNK:(i + 1) * CHUNK] + gbc_ref[...]
                          for i in range(n_chunks)], axis=0)
    b_all = sum(lax.dot_general(part, tri, (((1,), (1,)), ((), ())),
                                preferred_element_type=F32)
                for part in _split3(_log_sigmoid(gr)))
    b = jnp.concatenate([pltpu.roll(b_all[i * 2 * H:(i + 1) * 2 * H], H, axis=0)
                         for i in range(n_chunks)], axis=0)
    b = jnp.where(head_row, b, 0.0)
    u = jnp.where(head_row, gr - b, 0.0)
    cm = _prefix_max_lanes(u)
    u_max = jnp.broadcast_to(cm[:, CHUNK - 1:CHUNK], cm.shape)
    g_tot = jnp.broadcast_to(b[:, CHUNK - 1:CHUNK], b.shape)
    m_prev = jnp.zeros((2 * H, CHUNK), F32)
    m_prevs = []
    for i in range(n_chunks):
        rs = slice(i * 2 * H, (i + 1) * 2 * H)
        m_prevs.append(m_prev)
        m_prev = g_tot[rs] + jnp.maximum(m_prev, u_max[rs])
    m_prev = jnp.concatenate(m_prevs, axis=0)
    mm = jnp.maximum(m_prev, u_max)
    sp = jnp.exp(m_prev - mm)
    sl = jnp.exp(u_max - mm)
    wa = jnp.exp(u - u_max)
    m_in = jnp.maximum(m_prev, cm)
    e_w = jnp.exp(m_prev - m_in)
    e_m = jnp.exp(-(b + m_in))
    per = 2 * H * ML_CHUNKS_PER_STEP
    assert (4 * per) % LANES == 0
    pad = jnp.zeros((per, CHUNK), F32)
    for s in range(n_rows // per):
        rs = slice(s * per, (s + 1) * per)
        cols_ref[s] = jnp.concatenate([m_in[rs], e_w[rs], e_m[rs], pad], axis=0).T
        rows_ref[s] = jnp.concatenate([u[rs], sp[rs], sl[rs], wa[rs]], axis=0)


def _mlstm_kernel(qk_ref, v_ref, og_ref, gt_ref, gbc_ref, tri_ref, cw_ref, cb_ref, mlg_ref,
                  out_ref, xbuf, ybuf, ct_s, nb_s, cols_s, rows_s):
    step = pl.program_id(0)
    E = LANES
    H = ML_HEADS
    n_slab, n_seq, rows = qk_ref.shape[0], qk_ref.shape[1], qk_ref.shape[2]
    half = rows // 2
    n_chunks = rows // CHUNK
    assert n_slab == 2 * H

    @pl.when(step == 0)
    def _():
        xbuf[:, :, 0:SUBLANES, :] = jnp.zeros((n_slab, n_seq, SUBLANES, LANES), F32)
        ct_s[...] = jnp.zeros(ct_s.shape, F32)
        nb_s[...] = jnp.zeros(nb_s.shape, F32)
        for sq in range(n_seq):
            _gate_factors(gt_ref.at[sq], gbc_ref, tri_ref, cols_s.at[sq], rows_s.at[sq])

    @pl.when(step > 0)
    def _():
        xbuf[:, :, 0:SUBLANES, :] = xbuf[:, :, rows:rows + SUBLANES, :]

    xbuf[:, :, SUBLANES:SUBLANES + rows, :] = qk_ref[...]
    for sq in range(n_seq):
        for c in range(n_slab):
            cl = slice(c * LANES, (c + 1) * LANES)
            for par in range(2):
                acc = cb_ref[:, cl]
                for j in range(CONV_K):
                    off = SUBLANES - (CONV_K - 1) + j + par
                    acc = acc + xbuf[c, sq, pl.ds(off, half, stride=2), :] * cw_ref[j:j + 1, cl]
                act = _silu(acc)
                if c >= H:
                    act = act * (E ** -0.5)
                ybuf[c, sq, pl.ds(par, half, stride=2), :] = act

    n_rows = 2 * H * n_chunks
    cols = [cols_s[sq, step] for sq in range(n_seq)]
    fac = lambda sq, q_: rows_s[sq, step, q_ * n_rows:(q_ + 1) * n_rows, :]
    u = [fac(sq, 0) for sq in range(n_seq)]
    sp_all = [fac(sq, 1) for sq in range(n_seq)]
    sl_all = [fac(sq, 2) for sq in range(n_seq)]
    wa_all = [fac(sq, 3) for sq in range(n_seq)]

    ti = lax.broadcasted_iota(jnp.int32, (CHUNK, CHUNK), 0)
    si = lax.broadcasted_iota(jnp.int32, (CHUNK, CHUNK), 1)
    causal = si <= ti
    ones = jnp.ones((CHUNK, E), BF16)

    cts = [[ct_s[sq, h] for h in range(H)] for sq in range(n_seq)]
    nbs = [[nb_s[sq, h] for h in range(H)] for sq in range(n_seq)]
    for i in range(n_chunks):
        r0 = i * CHUNK
        for sq in range(n_seq):
            for h in range(H):
                row = i * 2 * H + h
                col = lambda q_, row=row, sq=sq: cols[sq][:, q_ * n_rows + row:
                                                          q_ * n_rows + row + 1]
                m_in_c, e_w, e_m = col(0), col(1), col(2)
                q = ybuf[h, sq, r0:r0 + CHUNK, :]
                k = ybuf[H + h, sq, r0:r0 + CHUNK, :]
                v1 = jnp.concatenate([v_ref[sq, r0:r0 + CHUNK, h * E:(h + 1) * E], ones],
                                     axis=1)
                qb = q.astype(BF16)

                d_w = jnp.where(causal, jnp.exp(u[sq][row:row + 1, :] - m_in_c), 0.0)
                s_qk = lax.dot_general(qb, k.astype(BF16), (((1,), (1,)), ((), ())),
                                       preferred_element_type=F32) * d_w
                st = jnp.concatenate([cts[sq][h].astype(BF16), nbs[sq][h].astype(BF16)], axis=1)
                inter = jnp.dot(qb, st, preferred_element_type=F32)
                intra = jnp.dot(s_qk.astype(BF16), v1, preferred_element_type=F32)
                num = e_w * inter[:, :E] + intra[:, :E]
                den = e_w * inter[:, E:] + intra[:, E:]
                hh = num / jnp.maximum(jnp.abs(den), e_m)

                hg = og_ref[sq, r0:r0 + CHUNK, h * E:(h + 1) * E] * hh
                mu = jnp.mean(hg, -1, keepdims=True)
                ex2 = jnp.mean(hg * hg, -1, keepdims=True)
                hc = hg - mu
                var = jnp.maximum(ex2 - mu * mu, 0.0)
                yn = hc * lax.rsqrt(var + LN_EPS) * mlg_ref[:, h * E:(h + 1) * E]
                out_ref[sq, r0:r0 + CHUNK, h * E:(h + 1) * E] = yn.astype(out_ref.dtype)

                kw = (k.T * wa_all[sq][row:row + 1, :]).astype(BF16)
                loc = jnp.dot(kw, v1, preferred_element_type=F32)
                sp_h, sl_h = sp_all[sq][row:row + 1, :], sl_all[sq][row:row + 1, :]
                cts[sq][h] = sp_h * cts[sq][h] + sl_h * loc[:, :E]
                nbs[sq][h] = sp_h * nbs[sq][h] + sl_h * loc[:, E:]
    for sq in range(n_seq):
        for h in range(H):
            ct_s[sq, h] = cts[sq][h]
            nb_s[sq, h] = nbs[sq][h]


def _mlstm(qk, vm, og, gatest, gb_col, conv_w, conv_b, ml_g, B, S):
    rows = ML_CHUNKS_PER_STEP * CHUNK
    ns = S // rows
    W = vm.shape[1]
    n_slab, G = qk.shape[0], gatest.shape[0]
    n_fac = 4 * 2 * ML_HEADS * ML_CHUNKS_PER_STEP
    tri = jnp.tril(jnp.ones((CHUNK, CHUNK), F32)).astype(BF16)
    seq = lambda w: pl.BlockSpec((B, rows, w), lambda c: (0, c, 0))
    consts = (gb_col, tri, conv_w, conv_b, ml_g)
    out = pl.pallas_call(
        _mlstm_kernel,
        grid=(ns,),
        in_specs=[pl.BlockSpec((n_slab, B, rows, LANES), lambda c: (0, 0, c, 0)),
                  seq(W), seq(W),
                  pl.BlockSpec((B, G, S), lambda c: (0, 0, 0))]
                 + [_const_spec(a.shape) for a in consts],
        out_specs=seq(W),
        out_shape=jax.ShapeDtypeStruct((B, S, W), BF16),
        scratch_shapes=[pltpu.VMEM((n_slab, B, rows + SUBLANES, LANES), F32),
                        pltpu.VMEM((n_slab, B, rows, LANES), F32),
                        pltpu.VMEM((B, ML_HEADS, LANES, LANES), F32),
                        pltpu.VMEM((B, ML_HEADS, LANES, LANES), F32),
                        pltpu.VMEM((B, ns, CHUNK, n_fac), F32),
                        pltpu.VMEM((B, ns, n_fac, CHUNK), F32)],
        compiler_params=pltpu.CompilerParams(
            dimension_semantics=("arbitrary",), vmem_limit_bytes=VMEM_LIMIT),
        name="mlstm",
    )(qk.reshape(n_slab, B, S, LANES), vm.reshape(B, S, W),
      og.reshape(B, S, W), jnp.swapaxes(gatest.reshape(G, B, S), 0, 1), *consts)
    return out.reshape(B * S, W)


def _tail_kernel(x1_ref, att_ref, hm_ref, kt_ref, v_ref, wout_ref, wq_ref, wo_ref,
                 wg_ref, wu_ref, wd_ref, g_ref, b_ref, out_ref):
    n_sub = x1_ref.shape[0] // TAIL_SUB_ROWS
    tiles = [slice(t * TAIL_SUB_ROWS, (t + 1) * TAIL_SUB_ROWS) for t in range(n_sub)]
    n_a = att_ref.shape[1]
    hd = wq_ref.shape[1] // XA_HEADS

    def mix(rs):
        return (jnp.dot(att_ref[rs, :], wout_ref[0:n_a, :], preferred_element_type=F32)
                + jnp.dot(hm_ref[rs, :], wout_ref[n_a:, :], preferred_element_type=F32))

    def cross(x2):
        q = jnp.dot(x2.astype(BF16), wq_ref[...], preferred_element_type=F32)
        heads = []
        for h in range(XA_HEADS):
            qh = q[:, h * hd:(h + 1) * hd].astype(BF16)
            lg = jnp.dot(qh, kt_ref[h * hd:(h + 1) * hd, :], preferred_element_type=F32)
            p = jnp.exp(lg - jnp.max(lg, -1, keepdims=True))
            s = jnp.sum(p, -1, keepdims=True)
            o = jnp.dot(p.astype(BF16), v_ref[:, h * hd:(h + 1) * hd],
                        preferred_element_type=F32)
            heads.append((o / s).astype(BF16))
        return jnp.dot(jnp.concatenate(heads, axis=1), wo_ref[...], preferred_element_type=F32)

    mixes = [mix(rs) for rs in tiles]
    x2s = [_layer_norm(ALPHA * x1_ref[rs, :] + m, g_ref[0:1, :], b_ref[0:1, :])
           for rs, m in zip(tiles, mixes)]
    xas = [cross(x2) for x2 in x2s]
    x3s = [_layer_norm(ALPHA * x2 + xa, g_ref[1:2, :], b_ref[1:2, :])
           for x2, xa in zip(x2s, xas)]
    ffs = [_swiglu(x3.astype(BF16), wg_ref, wu_ref, wd_ref) for x3 in x3s]
    for rs, x3, ff in zip(tiles, x3s, ffs):
        out_ref[rs, :] = _layer_norm(ALPHA * x3 + 0.5 * ff, g_ref[2:3, :], b_ref[2:3, :])


def _tail(x1, att, hm, kt, v, wout, wq, wo, wg, wu, wd, ffn_index, g, b, S):
    T, D = x1.shape
    tm = TAIL_ROW_TILE
    per_batch = S // tm
    row = lambda w: pl.BlockSpec((tm, w), lambda i: (i, 0))
    L = v.shape[1]
    weights = (wout, wq, wo, wg, wu, wd, g, b)
    return pl.pallas_call(
        _tail_kernel,
        grid=(T // tm,),
        in_specs=[row(D), row(att.shape[1]), row(hm.shape[1]),
                  pl.BlockSpec((None, D, L), lambda i: (i // per_batch, 0, 0)),
                  pl.BlockSpec((None, L, D), lambda i: (i // per_batch, 0, 0))]
                 + _weight_specs(weights, ffn_index),
        out_specs=row(D),
        out_shape=jax.ShapeDtypeStruct((T, D), F32),
        compiler_params=pltpu.CompilerParams(
            dimension_semantics=("parallel",), vmem_limit_bytes=VMEM_LIMIT),
        name="tail",
    )(x1, att, hm, kt, v, *weights)


def kernel(x, mem, rel_bias, ln_g, ln_b, ffn_w_gate, ffn_w_up, ffn_w_down, w_in, conv_w, conv_b,
           ig_bias, fg_bias, ml_norm_g, w_out, xq_w, xkv_w, xo_w):
    B, S, D = x.shape
    att_w = ATT_HEADS * ATT_HD
    ml_w = ML_HEADS * LANES
    bias = _bias_tables(rel_bias)
    xf = x.reshape(B * S, D)
    bf = lambda a: a.astype(BF16)
    wg_all, wu_all, wd_all = bf(ffn_w_gate), bf(ffn_w_up), bf(ffn_w_down)
    for l in range(DEPTH):
        wint = bf(w_in[l].T)
        gb = jnp.concatenate([ig_bias[l], fg_bias[l]]).astype(F32)

        x1, qa, kv1, kvf1, kvf2, qk, vm, og, gatest = _ffn_in(
            xf, wg_all, wu_all, wd_all, (l, 0),
            ln_g[l, 0][None], ln_b[l, 0][None], wint, 3 * att_w, 2 * ml_w, ml_w, S)
        att = _attention(qa, [kv1.reshape(B, S, -1), kvf1, kvf2], bias, B, S)
        att = att.reshape(B * S, att_w)
        hm = _mlstm(qk, vm, og, gatest, gb[:, None], conv_w[l], conv_b[l][None],
                    ml_norm_g[l][None], B, S)

        hd = D // XA_HEADS
        kt, v = _memkv(mem, bf(xkv_w[l]))
        xf = _tail(x1, att, hm, kt, v, bf(w_out[l]), bf(xq_w[l] * hd ** -0.5), bf(xo_w[l]),
                   wg_all, wu_all, wd_all, (l, 1), ln_g[l, 1:4], ln_b[l, 1:4], S)
    return xf.reshape(B, S, D)
```

```python
import functools
import math

import jax
import jax.numpy as jnp
from jax import lax
from jax.experimental import pallas as pl
from jax.experimental.pallas import tpu as pltpu

F32 = jnp.float32
BF16 = jnp.bfloat16

ATT_HD = 64
ATT_HEADS = 8
DILATED = ((128, 1), (512, 4), (2048, 16))
BLK = 128
ML_HEADS = 4
CHUNK = 128
CONV_K = 4
XA_HEADS = 4
REL_BUCKETS = 32
REL_MAX_DIST = 2048
DEPTH = 1
ALPHA = (2 * DEPTH) ** 0.25
LN_EPS = 1e-5
NEG = -1e30
LOG2E = math.log2(math.e)

LANES = 128
SUBLANES = 8
VMEM_LIMIT = 60 * 1024 * 1024
ROW_TILE = 512
ROW_SUBTILES = 2
TAIL_ROW_TILE = 1024
TAIL_SUB_ROWS = 512
ATT_ORDER = (2, 1, 0)
ATT_UNROLL = 32
ML_CHUNKS_PER_STEP = 4


def _const_spec(shape, lead=()):
    block = (None,) * len(lead) + tuple(shape[len(lead):])
    index = tuple(lead) + (0,) * (len(shape) - len(lead))
    return pl.BlockSpec(block, lambda *_: index, pipeline_mode=pl.Buffered(1))


def _weight_specs(weights, ffn_index):
    return [_const_spec(w.shape, ffn_index if w.ndim == 4 else ()) for w in weights]


def _layer_norm(y, g, b):
    mu = jnp.mean(y, -1, keepdims=True)
    yc = y - mu
    var = jnp.mean(yc * yc, -1, keepdims=True)
    return yc * lax.rsqrt(var + LN_EPS) * g + b


def _silu(x):
    return x * jax.nn.sigmoid(x)


def _swiglu(xb, wg_ref, wu_ref, wd_ref):
    g = jnp.dot(xb, wg_ref[...], preferred_element_type=F32)
    u = jnp.dot(xb, wu_ref[...], preferred_element_type=F32)
    h = (_silu(g) * u).astype(BF16)
    return jnp.dot(h, wd_ref[...], preferred_element_type=F32)


def _bias_kernel(rel_ref, bkt_ref, out_ref):
    bkt = bkt_ref[0]
    prev_half = lax.broadcasted_iota(jnp.int32, bkt.shape, 1) < BLK
    accs = [jnp.full(bkt.shape, NEG, F32) for _ in range(ATT_HEADS)]
    for b in range(REL_BUCKETS):
        hit = bkt == b
        for h in range(ATT_HEADS):
            accs[h] = jnp.where(hit, rel_ref[b, h] * LOG2E, accs[h])
    for h in range(ATT_HEADS):
        out_ref[0, 0, h] = accs[h]
        out_ref[1, 0, h] = jnp.where(prev_half, NEG, accs[h])


def _bias_tables(rel_bias):
    qi = jnp.arange(BLK)[:, None]
    ki = jnp.arange(2 * BLK)[None, :]
    off = qi + BLK - ki
    exact = REL_BUCKETS // 2
    n_log = REL_BUCKETS - exact
    starts = [math.ceil(exact * (REL_MAX_DIST / exact) ** (k / n_log)) for k in range(1, n_log)]
    tabs = []
    for window, dil in DILATED:
        n_keys = window // dil
        dist = dil * jnp.clip(off, 0, n_keys)
        large = exact + sum((dist >= s).astype(jnp.int32) for s in starts)
        bucket = jnp.where(dist < exact, dist, large)
        band = (off >= 0) & (off <= n_keys)
        tabs.append(jnp.where(band, bucket, -1))
    bkt = jnp.stack(tabs, 0).astype(jnp.int32)
    nd = len(DILATED)
    return pl.pallas_call(
        _bias_kernel,
        grid=(nd,),
        in_specs=[pl.BlockSpec(memory_space=pltpu.SMEM),
                  pl.BlockSpec((1, BLK, 2 * BLK), lambda d: (d, 0, 0))],
        out_specs=pl.BlockSpec((2, 1, ATT_HEADS, BLK, 2 * BLK), lambda d: (0, d, 0, 0, 0)),
        out_shape=jax.ShapeDtypeStruct((2, nd, ATT_HEADS, BLK, 2 * BLK), F32),
        name="bias",
    )(rel_bias.astype(F32), bkt)


def _memkv_kernel(mem_ref, wkv_ref, kt_ref, v_ref):
    D = mem_ref.shape[1]
    mb = mem_ref[...].astype(BF16)
    k = jnp.dot(mb, wkv_ref[:, 0:D], preferred_element_type=F32)
    kt_ref[...] = k.T.astype(BF16)
    v_ref[...] = jnp.dot(mb, wkv_ref[:, D:], preferred_element_type=F32).astype(BF16)


def _memkv(mem, wkv):
    B, L, D = mem.shape
    return pl.pallas_call(
        _memkv_kernel,
        grid=(B,),
        in_specs=[pl.BlockSpec((None, L, D), lambda b: (b, 0, 0)),
                  _const_spec((D, 2 * D))],
        out_specs=[pl.BlockSpec((None, D, L), lambda b: (b, 0, 0)),
                   pl.BlockSpec((None, L, D), lambda b: (b, 0, 0))],
        out_shape=[jax.ShapeDtypeStruct((B, D, L), BF16),
                   jax.ShapeDtypeStruct((B, L, D), BF16)],
        compiler_params=pltpu.CompilerParams(vmem_limit_bytes=VMEM_LIMIT),
        name="memkv",
    )(mem, wkv)


def _fold_rows(kv, t, s1, s2, fold_refs):
    (d1, ref1), (d2, ref2) = fold_refs
    ratio = d2 // d1
    sub = kv.shape[0]
    n1, n2 = sub // d1, sub // d2
    for c in range(kv.shape[1] // LANES):
        cl = slice(c * LANES, (c + 1) * LANES)
        s1[c] = kv[:, cl]
        for r1 in range(d1):
            p1 = s1[c, pl.ds(r1, n1, stride=d1), :]
            ref1[r1, t * n1:(t + 1) * n1, cl] = p1.astype(BF16)
            s2[c, r1 * n1:(r1 + 1) * n1, :] = p1
        for r1 in range(d1):
            for rp in range(ratio):
                p2 = s2[c, pl.ds(r1 * n1 + rp, n2, stride=ratio), :]
                ref2[d1 * rp + r1, t * n2:(t + 1) * n2, cl] = p2.astype(BF16)


def _ffn_in_kernel(x_ref, wg_ref, wu_ref, wd_ref, g_ref, b_ref, wint_ref,
                   x1_ref, qa_ref, kv1_ref, kvf1_ref, kvf2_ref, qk_ref, vm_ref, og_ref,
                   gatest_ref, s1, s2):
    n_q = qa_ref.shape[1]
    c1 = n_q + kv1_ref.shape[1]
    c2 = c1 + qk_ref.shape[0] * LANES
    c3 = c2 + vm_ref.shape[1]
    c4 = c3 + og_ref.shape[1]
    nt = (((1,), (1,)), ((), ()))
    sub = x_ref.shape[0] // ROW_SUBTILES
    tiles = [slice(t * sub, (t + 1) * sub) for t in range(ROW_SUBTILES)]
    ffs = [_swiglu(x_ref[rs, :].astype(BF16), wg_ref, wu_ref, wd_ref) for rs in tiles]
    fold_dils = [d for _, d in DILATED if d > 1]
    for t, (rs, ff) in enumerate(zip(tiles, ffs)):
        x1 = _layer_norm(ALPHA * x_ref[rs, :] + 0.5 * ff, g_ref[...], b_ref[...])
        x1_ref[rs, :] = x1
        xb = x1.astype(BF16)
        proj = lambda lo, hi, xb=xb: lax.dot_general(xb, wint_ref[lo:hi, :], nt,
                                                     preferred_element_type=F32)

        qkm = proj(c1, c2)
        for c in range(qk_ref.shape[0]):
            qk_ref[c, rs, :] = qkm[:, c * LANES:(c + 1) * LANES]

        qa_ref[rs, :] = proj(0, n_q) * (ATT_HD ** -0.5 * LOG2E)
        kv = proj(n_q, c1)
        kv1_ref[rs, :] = kv.astype(BF16)
        _fold_rows(kv, t, s1, s2, list(zip(fold_dils, (kvf1_ref, kvf2_ref))))
        vm_ref[rs, :] = proj(c2, c3).astype(BF16)
        og_ref[rs, :] = jax.nn.sigmoid(proj(c3, c4))
        gatest_ref[:, rs] = lax.dot_general(wint_ref[c4:c4 + gatest_ref.shape[0], :], xb, nt,
                                            preferred_element_type=F32)


def _ffn_in(x2d, wg, wu, wd, ffn_index, g, b, wint, n_att, n_qkm, n_ml, S):
    T, D = x2d.shape
    tm = ROW_TILE
    tps = S // tm
    row = lambda w: pl.BlockSpec((tm, w), lambda i: (i, 0))
    n_g = wint.shape[0] - n_att - n_qkm - 2 * n_ml
    n_slab = n_qkm // LANES
    n_q, n_kv = n_att // 3, 2 * n_att // 3
    fold = lambda d: pl.BlockSpec((None, d, tm // d, n_kv), lambda i: (i // tps, 0, i % tps, 0))
    fold_shape = lambda d: jax.ShapeDtypeStruct((T // S, d, S // d, n_kv), BF16)
    d1, d2 = [d for _, d in DILATED if d > 1]
    assert d2 % d1 == 0 and (tm // ROW_SUBTILES) % (2 * SUBLANES * d2) == 0
    weights = (wg, wu, wd, g, b, wint)
    sub = tm // ROW_SUBTILES
    return pl.pallas_call(
        _ffn_in_kernel,
        grid=(T // tm,),
        in_specs=[row(D)] + _weight_specs(weights, ffn_index),
        out_specs=[row(D), row(n_q), row(n_kv), fold(d1), fold(d2),
                   pl.BlockSpec((n_slab, tm, LANES), lambda i: (0, i, 0)),
                   row(n_ml), row(n_ml),
                   pl.BlockSpec((n_g, tm), lambda i: (0, i))],
        out_shape=[jax.ShapeDtypeStruct((T, D), F32),
                   jax.ShapeDtypeStruct((T, n_q), F32),
                   jax.ShapeDtypeStruct((T, n_kv), BF16),
                   fold_shape(d1), fold_shape(d2),
                   jax.ShapeDtypeStruct((n_slab, T, LANES), F32),
                   jax.ShapeDtypeStruct((T, n_ml), BF16),
                   jax.ShapeDtypeStruct((T, n_ml), F32),
                   jax.ShapeDtypeStruct((n_g, T), F32)],
        scratch_shapes=[pltpu.VMEM((n_kv // LANES, sub, LANES), F32)] * 2,
        compiler_params=pltpu.CompilerParams(
            dimension_semantics=("parallel",), vmem_limit_bytes=VMEM_LIMIT),
        name="ffn_in",
    )(x2d, *weights)


def _attn_block(q_ref, kv_refs, bias_ref, out_ref, acc_s, m_s, l_s, j, step):
    di = ATT_ORDER[step]
    dil = DILATED[di][1]
    k_ref, v_ref = kv_refs[di]
    lane = lax.broadcasted_iota(jnp.int32, (BLK, LANES), 1)
    head0 = lane < ATT_HD
    r = j % dil
    n = j // dil

    def rows_of(start, stride):
        if stride == 1:
            return pl.ds(pl.multiple_of(start, BLK), BLK)
        return pl.ds(start, BLK, stride=stride)

    cur = rows_of(n * BLK, 1)
    prev = rows_of(jnp.maximum(n - 1, 0) * BLK, 1)
    if dil == 1:
        take = lambda ref, rows: ref[rows, :]
    else:
        take = lambda ref, rows: ref[r, rows, :]
    rows = rows_of(r + n * (BLK * dil), dil)
    q = q_ref[rows, :]
    kk = jnp.concatenate([take(k_ref, prev), take(k_ref, cur)], axis=0)
    vv = jnp.concatenate([take(v_ref, prev), take(v_ref, cur)], axis=0)
    first = jnp.where(n == 0, 1, 0)
    ms, ls, os_ = [], [], []
    for h in range(2):
        sel = head0 if h == 0 else jnp.logical_not(head0)
        qh = jnp.where(sel, q, 0.0).astype(BF16)
        lg = lax.dot_general(qh, kk, (((1,), (1,)), ((), ())), preferred_element_type=F32)
        lg = lg + bias_ref[first, di, h]
        mh = jnp.max(lg, -1, keepdims=True)
        p = jnp.exp2(lg - mh)
        ls.append(jnp.sum(p, -1, keepdims=True))
        ms.append(mh)
        os_.append(jnp.dot(p.astype(BF16), vv, preferred_element_type=F32))
    m_b = jnp.where(head0, ms[0], ms[1])
    l_b = jnp.where(head0, ls[0], ls[1])
    o_b = jnp.where(head0, os_[0], os_[1])
    if step > 0:
        m_o = m_s[rows, :]
        m_n = jnp.maximum(m_o, m_b)
        e_o = jnp.exp2(m_o - m_n)
        e_b = jnp.exp2(m_b - m_n)
        l_b = l_s[rows, :] * e_o + l_b * e_b
        o_b = acc_s[rows, :] * e_o + o_b * e_b
        m_b = m_n
    if step < len(ATT_ORDER) - 1:
        m_s[rows, :] = m_b
        l_s[rows, :] = l_b
        acc_s[rows, :] = o_b
    else:
        out_ref[rows, :] = (o_b / l_b).astype(out_ref.dtype)


def _attn_kernel(q_ref, *refs):
    n_br = len(DILATED)
    kv_refs = [(refs[2 * i], refs[2 * i + 1]) for i in range(n_br)]
    bias_ref, out_ref, acc_s, m_s, l_s = refs[2 * n_br:]
    n_blocks = q_ref.shape[0] // BLK
    for step in range(len(ATT_ORDER)):
        def body(i, _, step=step):
            for u in range(ATT_UNROLL):
                _attn_block(q_ref, kv_refs, bias_ref, out_ref, acc_s, m_s, l_s,
                            i * ATT_UNROLL + u, step)
            return 0
        lax.fori_loop(0, n_blocks // ATT_UNROLL, body, 0)


def _attention(q, kvs, bias, B, S):
    n_pairs = ATT_HEADS * ATT_HD // LANES
    dils = [d for _, d in DILATED]
    assert dils[ATT_ORDER[-1]] == 1
    in_specs = [pl.BlockSpec((None, S, LANES), lambda b, p: (b, 0, p))]
    operands = [q.reshape(B, S, q.shape[-1])]
    for d, kv in zip(dils, kvs):
        for off in (0, n_pairs):
            if d == 1:
                in_specs.append(pl.BlockSpec((None, S, LANES),
                                             lambda b, p, off=off: (b, 0, off + p)))
            else:
                in_specs.append(pl.BlockSpec((None, d, S // d, LANES),
                                             lambda b, p, off=off: (b, 0, 0, off + p)))
            operands.append(kv)
    in_specs.append(pl.BlockSpec((2, len(DILATED), 2, BLK, 2 * BLK),
                                 lambda b, p: (0, 0, p, 0, 0)))
    return pl.pallas_call(
        _attn_kernel,
        grid=(B, n_pairs),
        in_specs=in_specs,
        out_specs=pl.BlockSpec((None, S, LANES), lambda b, p: (b, 0, p)),
        out_shape=jax.ShapeDtypeStruct((B, S, n_pairs * LANES), BF16),
        scratch_shapes=[pltpu.VMEM((S, LANES), F32)] * 3,
        compiler_params=pltpu.CompilerParams(
            dimension_semantics=("parallel", "parallel"), vmem_limit_bytes=VMEM_LIMIT),
        name="attn",
    )(*operands, bias)


def _split3(a):
    hi = a.astype(BF16)
    r1 = a - hi.astype(F32)
    mid = r1.astype(BF16)
    lo = (r1 - mid.astype(F32)).astype(BF16)
    return hi, mid, lo


def _log_sigmoid(x):
    return jnp.minimum(x, 0.0) - jnp.log1p(jnp.exp(-jnp.abs(x)))


def _prefix_max_lanes(x):
    lane = lax.broadcasted_iota(jnp.int32, x.shape, 1)
    sh = 1
    while sh < x.shape[1]:
        x = jnp.maximum(x, jnp.where(lane >= sh, pltpu.roll(x, sh, axis=1), -jnp.inf))
        sh *= 2
    return x


def _gate_factors(gt_ref, gbc_ref, tri_ref, cols_ref, rows_ref):
    H = ML_HEADS
    n_chunks = gt_ref.shape[1] // CHUNK
    n_rows = 2 * H * n_chunks
    tri = tri_ref[...]
    head_row = (lax.broadcasted_iota(jnp.int32, (n_rows, CHUNK), 0) & (2 * H - 1)) < H
    gr = jnp.concatenate([gt_ref[:, i * CHUNK:(i + 1) * CHUNK] + gbc_ref[...]
                          for i in range(n_chunks)], axis=0)
    b_all = sum(lax.dot_general(part, tri, (((1,), (1,)), ((), ())),
                                preferred_element_type=F32)
                for part in _split3(_log_sigmoid(gr)))
    b = jnp.concatenate([pltpu.roll(b_all[i * 2 * H:(i + 1) * 2 * H], H, axis=0)
                         for i in range(n_chunks)], axis=0)
    b = jnp.where(head_row, b, 0.0)
    u = jnp.where(head_row, gr - b, 0.0)
    cm = _prefix_max_lanes(u)
    u_max = jnp.broadcast_to(cm[:, CHUNK - 1:CHUNK], cm.shape)
    g_tot = jnp.broadcast_to(b[:, CHUNK - 1:CHUNK], b.shape)
    m_prev = jnp.zeros((2 * H, CHUNK), F32)
    m_prevs = []
    for i in range(n_chunks):
        rs = slice(i * 2 * H, (i + 1) * 2 * H)
        m_prevs.append(m_prev)
        m_prev = g_tot[rs] + jnp.maximum(m_prev, u_max[rs])
    m_prev = jnp.concatenate(m_prevs, axis=0)
    mm = jnp.maximum(m_prev, u_max)
    sp = jnp.exp(m_prev - mm)
    sl = jnp.exp(u_max - mm)
    wa = jnp.exp(u - u_max)
    m_in = jnp.maximum(m_prev, cm)
    e_w = jnp.exp(m_prev - m_in)
    e_m = jnp.exp(-(b + m_in))
    per = 2 * H * ML_CHUNKS_PER_STEP
    assert 4 * per == LANES
    pad = jnp.zeros((per, CHUNK), F32)
    for s in range(n_rows // per):
        rs = slice(s * per, (s + 1) * per)
        cols_ref[s] = jnp.concatenate([m_in[rs], e_w[rs], e_m[rs], pad], axis=0).T
        rows_ref[s] = jnp.concatenate([u[rs], sp[rs], sl[rs], wa[rs]], axis=0)


def _mlstm_kernel(qk_ref, v_ref, og_ref, gt_ref, gbc_ref, tri_ref, cw_ref, cb_ref, mlg_ref,
                  out_ref, xbuf, ybuf, ct_s, nb_s, cols_s, rows_s):
    step = pl.program_id(0)
    E = LANES
    H = ML_HEADS
    n_slab, n_seq, rows = qk_ref.shape[0], qk_ref.shape[1], qk_ref.shape[2]
    half = rows // 2
    n_chunks = rows // CHUNK
    assert n_slab == 2 * H

    @pl.when(step == 0)
    def _():
        xbuf[:, :, 0:SUBLANES, :] = jnp.zeros((n_slab, n_seq, SUBLANES, LANES), F32)
        ct_s[...] = jnp.zeros(ct_s.shape, F32)
        nb_s[...] = jnp.zeros(nb_s.shape, F32)
        for sq in range(n_seq):
            _gate_factors(gt_ref.at[sq], gbc_ref, tri_ref, cols_s.at[sq], rows_s.at[sq])

    @pl.when(step > 0)
    def _():
        xbuf[:, :, 0:SUBLANES, :] = xbuf[:, :, rows:rows + SUBLANES, :]

    xbuf[:, :, SUBLANES:SUBLANES + rows, :] = qk_ref[...]
    for sq in range(n_seq):
        for c in range(n_slab):
            cl = slice(c * LANES, (c + 1) * LANES)
            for par in range(2):
                acc = cb_ref[:, cl]
                for j in range(CONV_K):
                    off = SUBLANES - (CONV_K - 1) + j + par
                    acc = acc + xbuf[c, sq, pl.ds(off, half, stride=2), :] * cw_ref[j:j + 1, cl]
                act = _silu(acc)
                if c >= H:
                    act = act * (E ** -0.5)
                ybuf[c, sq, pl.ds(par, half, stride=2), :] = act

    n_rows = 2 * H * n_chunks
    cols = [cols_s[sq, step] for sq in range(n_seq)]
    fac = lambda sq, q_: rows_s[sq, step, q_ * n_rows:(q_ + 1) * n_rows, :]
    u = [fac(sq, 0) for sq in range(n_seq)]
    sp_all = [fac(sq, 1) for sq in range(n_seq)]
    sl_all = [fac(sq, 2) for sq in range(n_seq)]
    wa_all = [fac(sq, 3) for sq in range(n_seq)]

    ti = lax.broadcasted_iota(jnp.int32, (CHUNK, CHUNK), 0)
    si = lax.broadcasted_iota(jnp.int32, (CHUNK, CHUNK), 1)
    causal = si <= ti
    ones = jnp.ones((CHUNK, E), BF16)

    cts = [[ct_s[sq, h] for h in range(H)] for sq in range(n_seq)]
    nbs = [[nb_s[sq, h] for h in range(H)] for sq in range(n_seq)]
    for i in range(n_chunks):
        r0 = i * CHUNK
        for h in range(H):
            for sq in range(n_seq):
                row = i * 2 * H + h
                col = lambda q_, row=row, sq=sq: cols[sq][:, q_ * n_rows + row:
                                                          q_ * n_rows + row + 1]
                m_in_c, e_w, e_m = col(0), col(1), col(2)
                q = ybuf[h, sq, r0:r0 + CHUNK, :]
                k = ybuf[H + h, sq, r0:r0 + CHUNK, :]
                v1 = jnp.concatenate([v_ref[sq, r0:r0 + CHUNK, h * E:(h + 1) * E], ones],
                                     axis=1)
                qb = q.astype(BF16)

                d_w = jnp.where(causal, jnp.exp(u[sq][row:row + 1, :] - m_in_c), 0.0)
                s_qk = lax.dot_general(qb, k.astype(BF16), (((1,), (1,)), ((), ())),
                                       preferred_element_type=F32) * d_w
                st = jnp.concatenate([cts[sq][h].astype(BF16), nbs[sq][h].astype(BF16)], axis=1)
                inter = jnp.dot(qb, st, preferred_element_type=F32)
                intra = jnp.dot(s_qk.astype(BF16), v1, preferred_element_type=F32)
                num = e_w * inter[:, :E] + intra[:, :E]
                den = e_w * inter[:, E:] + intra[:, E:]
                hh = num / jnp.maximum(jnp.abs(den), e_m)

                hg = og_ref[sq, r0:r0 + CHUNK, h * E:(h + 1) * E] * hh
                mu = jnp.mean(hg, -1, keepdims=True)
                ex2 = jnp.mean(hg * hg, -1, keepdims=True)
                hc = hg - mu
                var = jnp.maximum(ex2 - mu * mu, 0.0)
                yn = hc * lax.rsqrt(var + LN_EPS) * mlg_ref[:, h * E:(h + 1) * E]
                out_ref[sq, r0:r0 + CHUNK, h * E:(h + 1) * E] = yn.astype(out_ref.dtype)

                kw = (k.T * wa_all[sq][row:row + 1, :]).astype(BF16)
                loc = jnp.dot(kw, v1, preferred_element_type=F32)
                sp_h, sl_h = sp_all[sq][row:row + 1, :], sl_all[sq][row:row + 1, :]
                cts[sq][h] = sp_h * cts[sq][h] + sl_h * loc[:, :E]
                nbs[sq][h] = sp_h * nbs[sq][h] + sl_h * loc[:, E:]
    for sq in range(n_seq):
        for h in range(H):
            ct_s[sq, h] = cts[sq][h]
            nb_s[sq, h] = nbs[sq][h]


def _mlstm(qk, vm, og, gatest, gb_col, conv_w, conv_b, ml_g, B, S):
    rows = ML_CHUNKS_PER_STEP * CHUNK
    ns = S // rows
    W = vm.shape[1]
    n_slab, G = qk.shape[0], gatest.shape[0]
    tri = jnp.tril(jnp.ones((CHUNK, CHUNK), F32)).astype(BF16)
    seq = lambda w: pl.BlockSpec((B, rows, w), lambda c: (0, c, 0))
    consts = (gb_col, tri, conv_w, conv_b, ml_g)
    out = pl.pallas_call(
        _mlstm_kernel,
        grid=(ns,),
        in_specs=[pl.BlockSpec((n_slab, B, rows, LANES), lambda c: (0, 0, c, 0)),
                  seq(W), seq(W),
                  pl.BlockSpec((B, G, S), lambda c: (0, 0, 0))]
                 + [_const_spec(a.shape) for a in consts],
        out_specs=seq(W),
        out_shape=jax.ShapeDtypeStruct((B, S, W), BF16),
        scratch_shapes=[pltpu.VMEM((n_slab, B, rows + SUBLANES, LANES), F32),
                        pltpu.VMEM((n_slab, B, rows, LANES), F32),
                        pltpu.VMEM((B, ML_HEADS, LANES, LANES), F32),
                        pltpu.VMEM((B, ML_HEADS, LANES, LANES), F32),
                        pltpu.VMEM((B, ns, LANES, CHUNK), F32),
                        pltpu.VMEM((B, ns, LANES, CHUNK), F32)],
        compiler_params=pltpu.CompilerParams(
            dimension_semantics=("arbitrary",), vmem_limit_bytes=VMEM_LIMIT),
        name="mlstm",
    )(qk.reshape(n_slab, B, S, LANES), vm.reshape(B, S, W),
      og.reshape(B, S, W), jnp.swapaxes(gatest.reshape(G, B, S), 0, 1), *consts)
    return out.reshape(B * S, W)


def _tail_kernel(x1_ref, att_ref, hm_ref, kt_ref, v_ref, wout_ref, wq_ref, wo_ref,
                 wg_ref, wu_ref, wd_ref, g_ref, b_ref, out_ref):
    n_sub = x1_ref.shape[0] // TAIL_SUB_ROWS
    tiles = [slice(t * TAIL_SUB_ROWS, (t + 1) * TAIL_SUB_ROWS) for t in range(n_sub)]
    n_a = att_ref.shape[1]
    hd = wq_ref.shape[1] // XA_HEADS

    def mix(rs):
        return (jnp.dot(att_ref[rs, :], wout_ref[0:n_a, :], preferred_element_type=F32)
                + jnp.dot(hm_ref[rs, :], wout_ref[n_a:, :], preferred_element_type=F32))

    def cross(x2):
        q = jnp.dot(x2.astype(BF16), wq_ref[...], preferred_element_type=F32)
        heads = []
        for h in range(XA_HEADS):
            qh = q[:, h * hd:(h + 1) * hd].astype(BF16)
            lg = jnp.dot(qh, kt_ref[h * hd:(h + 1) * hd, :], preferred_element_type=F32)
            p = jnp.exp(lg - jnp.max(lg, -1, keepdims=True))
            s = jnp.sum(p, -1, keepdims=True)
            o = jnp.dot(p.astype(BF16), v_ref[:, h * hd:(h + 1) * hd],
                        preferred_element_type=F32)
            heads.append((o / s).astype(BF16))
        return jnp.dot(jnp.concatenate(heads, axis=1), wo_ref[...], preferred_element_type=F32)

    mixes = [mix(rs) for rs in tiles]
    x2s = [_layer_norm(ALPHA * x1_ref[rs, :] + m, g_ref[0:1, :], b_ref[0:1, :])
           for rs, m in zip(tiles, mixes)]
    xas = [cross(x2) for x2 in x2s]
    x3s = [_layer_norm(ALPHA * x2 + xa, g_ref[1:2, :], b_ref[1:2, :])
           for x2, xa in zip(x2s, xas)]
    ffs = [_swiglu(x3.astype(BF16), wg_ref, wu_ref, wd_ref) for x3 in x3s]
    for rs, x3, ff in zip(tiles, x3s, ffs):
        out_ref[rs, :] = _layer_norm(ALPHA * x3 + 0.5 * ff, g_ref[2:3, :], b_ref[2:3, :])


def _tail(x1, att, hm, kt, v, wout, wq, wo, wg, wu, wd, ffn_index, g, b, S):
    T, D = x1.shape
    tm = TAIL_ROW_TILE
    per_batch = S // tm
    row = lambda w: pl.BlockSpec((tm, w), lambda i: (i, 0))
    L = v.shape[1]
    weights = (wout, wq, wo, wg, wu, wd, g, b)
    return pl.pallas_call(
        _tail_kernel,
        grid=(T // tm,),
        in_specs=[row(D), row(att.shape[1]), row(hm.shape[1]),
                  pl.BlockSpec((None, D, L), lambda i: (i // per_batch, 0, 0)),
                  pl.BlockSpec((None, L, D), lambda i: (i // per_batch, 0, 0))]
                 + _weight_specs(weights, ffn_index),
        out_specs=row(D),
        out_shape=jax.ShapeDtypeStruct((T, D), F32),
        compiler_params=pltpu.CompilerParams(
            dimension_semantics=("parallel",), vmem_limit_bytes=VMEM_LIMIT),
        name="tail",
    )(x1, att, hm, kt, v, *weights)


def kernel(x, mem, rel_bias, ln_g, ln_b, ffn_w_gate, ffn_w_up, ffn_w_down, w_in, conv_w, conv_b,
           ig_bias, fg_bias, ml_norm_g, w_out, xq_w, xkv_w, xo_w):
    B, S, D = x.shape
    att_w = ATT_HEADS * ATT_HD
    ml_w = ML_HEADS * LANES
    bias = _bias_tables(rel_bias)
    xf = x.reshape(B * S, D)
    bf = lambda a: a.astype(BF16)
    wg_all, wu_all, wd_all = bf(ffn_w_gate), bf(ffn_w_up), bf(ffn_w_down)
    for l in range(DEPTH):
        wint = bf(w_in[l].T)
        gb = jnp.concatenate([ig_bias[l], fg_bias[l]]).astype(F32)

        x1, qa, kv1, kvf1, kvf2, qk, vm, og, gatest = _ffn_in(
            xf, wg_all, wu_all, wd_all, (l, 0),
            ln_g[l, 0][None], ln_b[l, 0][None], wint, 3 * att_w, 2 * ml_w, ml_w, S)
        att = _attention(qa, [kv1.reshape(B, S, -1), kvf1, kvf2], bias, B, S)
        att = att.reshape(B * S, att_w)
        hm = _mlstm(qk, vm, og, gatest, gb[:, None], conv_w[l], conv_b[l][None],
                    ml_norm_g[l][None], B, S)

        hd = D // XA_HEADS
        kt, v = _memkv(mem, bf(xkv_w[l]))
        xf = _tail(x1, att, hm, kt, v, bf(w_out[l]), bf(xq_w[l] * hd ** -0.5), bf(xo_w[l]),
                   wg_all, wu_all, wd_all, (l, 1), ln_g[l, 1:4], ln_b[l, 1:4], S)
    return xf.reshape(B, S, D)
```

```python
import functools
import math

import jax
import jax.numpy as jnp
from jax import lax
from jax.experimental import pallas as pl
from jax.experimental.pallas import tpu as pltpu

F32 = jnp.float32
BF16 = jnp.bfloat16

ATT_HD = 64
ATT_HEADS = 8
DILATED = ((128, 1), (512, 4), (2048, 16))
BLK = 128
ML_HEADS = 4
CHUNK = 128
CONV_K = 4
XA_HEADS = 4
REL_BUCKETS = 32
REL_MAX_DIST = 2048
DEPTH = 1
ALPHA = (2 * DEPTH) ** 0.25
LN_EPS = 1e-5
NEG = -1e30
LOG2E = math.log2(math.e)

LANES = 128
SUBLANES = 8
VMEM_LIMIT = 60 * 1024 * 1024
ROW_TILE = 512
ROW_SUBTILES = 2
TAIL_ROW_TILE = 1024
TAIL_SUB_ROWS = 512
FFN_CHUNKS = 2
ATT_ORDER = (2, 1, 0)
ATT_UNROLL = 32
ML_CHUNKS_PER_STEP = 4


def _const_spec(shape, lead=()):
    block = (None,) * len(lead) + tuple(shape[len(lead):])
    index = tuple(lead) + (0,) * (len(shape) - len(lead))
    return pl.BlockSpec(block, lambda *_: index, pipeline_mode=pl.Buffered(1))


def _weight_specs(weights, ffn_index):
    return [_const_spec(w.shape, ffn_index if w.ndim == 4 else ()) for w in weights]


def _layer_norm(y, g, b):
    mu = jnp.mean(y, -1, keepdims=True)
    yc = y - mu
    var = jnp.mean(yc * yc, -1, keepdims=True)
    return yc * lax.rsqrt(var + LN_EPS) * g + b


def _silu(x):
    return x * jax.nn.sigmoid(x)


def _swiglu(xb, wg_ref, wu_ref, wd_ref):
    n_f = wg_ref.shape[1]
    step = n_f // FFN_CHUNKS
    out = None
    for c in range(FFN_CHUNKS):
        fs = slice(c * step, (c + 1) * step)
        g = jnp.dot(xb, wg_ref[:, fs], preferred_element_type=F32)
        u = jnp.dot(xb, wu_ref[:, fs], preferred_element_type=F32)
        h = (_silu(g) * u).astype(BF16)
        part = jnp.dot(h, wd_ref[fs, :], preferred_element_type=F32)
        out = part if out is None else out + part
    return out


def _bias_kernel(rel_ref, bkt_ref, out_ref):
    bkt = bkt_ref[0]
    prev_half = lax.broadcasted_iota(jnp.int32, bkt.shape, 1) < BLK
    accs = [jnp.full(bkt.shape, NEG, F32) for _ in range(ATT_HEADS)]
    for b in range(REL_BUCKETS):
        hit = bkt == b
        for h in range(ATT_HEADS):
            accs[h] = jnp.where(hit, rel_ref[b, h] * LOG2E, accs[h])
    for h in range(ATT_HEADS):
        out_ref[0, 0, h] = accs[h]
        out_ref[1, 0, h] = jnp.where(prev_half, NEG, accs[h])


def _bias_tables(rel_bias):
    qi = jnp.arange(BLK)[:, None]
    ki = jnp.arange(2 * BLK)[None, :]
    off = qi + BLK - ki
    exact = REL_BUCKETS // 2
    n_log = REL_BUCKETS - exact
    starts = [math.ceil(exact * (REL_MAX_DIST / exact) ** (k / n_log)) for k in range(1, n_log)]
    tabs = []
    for window, dil in DILATED:
        n_keys = window // dil
        dist = dil * jnp.clip(off, 0, n_keys)
        large = exact + sum((dist >= s).astype(jnp.int32) for s in starts)
        bucket = jnp.where(dist < exact, dist, large)
        band = (off >= 0) & (off <= n_keys)
        tabs.append(jnp.where(band, bucket, -1))
    bkt = jnp.stack(tabs, 0).astype(jnp.int32)
    nd = len(DILATED)
    return pl.pallas_call(
        _bias_kernel,
        grid=(nd,),
        in_specs=[pl.BlockSpec(memory_space=pltpu.SMEM),
                  pl.BlockSpec((1, BLK, 2 * BLK), lambda d: (d, 0, 0))],
        out_specs=pl.BlockSpec((2, 1, ATT_HEADS, BLK, 2 * BLK), lambda d: (0, d, 0, 0, 0)),
        out_shape=jax.ShapeDtypeStruct((2, nd, ATT_HEADS, BLK, 2 * BLK), F32),
        name="bias",
    )(rel_bias.astype(F32), bkt)


def _memkv_kernel(mem_ref, wkv_ref, kt_ref, v_ref):
    D = mem_ref.shape[1]
    mb = mem_ref[...].astype(BF16)
    k = jnp.dot(mb, wkv_ref[:, 0:D], preferred_element_type=F32)
    kt_ref[...] = k.T.astype(BF16)
    v_ref[...] = jnp.dot(mb, wkv_ref[:, D:], preferred_element_type=F32).astype(BF16)


def _memkv(mem, wkv):
    B, L, D = mem.shape
    return pl.pallas_call(
        _memkv_kernel,
        grid=(B,),
        in_specs=[pl.BlockSpec((None, L, D), lambda b: (b, 0, 0)),
                  _const_spec((D, 2 * D))],
        out_specs=[pl.BlockSpec((None, D, L), lambda b: (b, 0, 0)),
                   pl.BlockSpec((None, L, D), lambda b: (b, 0, 0))],
        out_shape=[jax.ShapeDtypeStruct((B, D, L), BF16),
                   jax.ShapeDtypeStruct((B, L, D), BF16)],
        compiler_params=pltpu.CompilerParams(vmem_limit_bytes=VMEM_LIMIT),
        name="memkv",
    )(mem, wkv)


def _fold_rows(kv, t, s1, s2, fold_refs):
    (d1, ref1), (d2, ref2) = fold_refs
    ratio = d2 // d1
    sub = kv.shape[0]
    n1, n2 = sub // d1, sub // d2
    for c in range(kv.shape[1] // LANES):
        cl = slice(c * LANES, (c + 1) * LANES)
        s1[c] = kv[:, cl]
        for r1 in range(d1):
            p1 = s1[c, pl.ds(r1, n1, stride=d1), :]
            ref1[r1, t * n1:(t + 1) * n1, cl] = p1.astype(BF16)
            s2[c, r1 * n1:(r1 + 1) * n1, :] = p1
        for r1 in range(d1):
            for rp in range(ratio):
                p2 = s2[c, pl.ds(r1 * n1 + rp, n2, stride=ratio), :]
                ref2[d1 * rp + r1, t * n2:(t + 1) * n2, cl] = p2.astype(BF16)


def _ffn_in_kernel(x_ref, wg_ref, wu_ref, wd_ref, g_ref, b_ref, wint_ref,
                   x1_ref, qa_ref, kv1_ref, kvf1_ref, kvf2_ref, qk_ref, vm_ref, og_ref,
                   gatest_ref, s1, s2):
    n_q = qa_ref.shape[1]
    c1 = n_q + kv1_ref.shape[1]
    c2 = c1 + qk_ref.shape[0] * LANES
    c3 = c2 + vm_ref.shape[1]
    c4 = c3 + og_ref.shape[1]
    nt = (((1,), (1,)), ((), ()))
    sub = x_ref.shape[0] // ROW_SUBTILES
    tiles = [slice(t * sub, (t + 1) * sub) for t in range(ROW_SUBTILES)]
    ffs = [_swiglu(x_ref[rs, :].astype(BF16), wg_ref, wu_ref, wd_ref) for rs in tiles]
    fold_dils = [d for _, d in DILATED if d > 1]
    for t, (rs, ff) in enumerate(zip(tiles, ffs)):
        x1 = _layer_norm(ALPHA * x_ref[rs, :] + 0.5 * ff, g_ref[...], b_ref[...])
        x1_ref[rs, :] = x1
        xb = x1.astype(BF16)
        proj = lambda lo, hi, xb=xb: lax.dot_general(xb, wint_ref[lo:hi, :], nt,
                                                     preferred_element_type=F32)

        qkm = proj(c1, c2)
        for c in range(qk_ref.shape[0]):
            qk_ref[c, rs, :] = qkm[:, c * LANES:(c + 1) * LANES]

        qa_ref[rs, :] = proj(0, n_q) * (ATT_HD ** -0.5 * LOG2E)
        kv = proj(n_q, c1)
        kv1_ref[rs, :] = kv.astype(BF16)
        _fold_rows(kv, t, s1, s2, list(zip(fold_dils, (kvf1_ref, kvf2_ref))))
        vm_ref[rs, :] = proj(c2, c3).astype(BF16)
        og_ref[rs, :] = jax.nn.sigmoid(proj(c3, c4))
        gatest_ref[:, rs] = lax.dot_general(wint_ref[c4:c4 + gatest_ref.shape[0], :], xb, nt,
                                            preferred_element_type=F32)


def _ffn_in(x2d, wg, wu, wd, ffn_index, g, b, wint, n_att, n_qkm, n_ml, S):
    T, D = x2d.shape
    tm = ROW_TILE
    tps = S // tm
    row = lambda w: pl.BlockSpec((tm, w), lambda i: (i, 0))
    n_g = wint.shape[0] - n_att - n_qkm - 2 * n_ml
    n_slab = n_qkm // LANES
    n_q, n_kv = n_att // 3, 2 * n_att // 3
    fold = lambda d: pl.BlockSpec((None, d, tm // d, n_kv), lambda i: (i // tps, 0, i % tps, 0))
    fold_shape = lambda d: jax.ShapeDtypeStruct((T // S, d, S // d, n_kv), BF16)
    d1, d2 = [d for _, d in DILATED if d > 1]
    assert d2 % d1 == 0 and (tm // ROW_SUBTILES) % (2 * SUBLANES * d2) == 0
    weights = (wg, wu, wd, g, b, wint)
    sub = tm // ROW_SUBTILES
    return pl.pallas_call(
        _ffn_in_kernel,
        grid=(T // tm,),
        in_specs=[row(D)] + _weight_specs(weights, ffn_index),
        out_specs=[row(D), row(n_q), row(n_kv), fold(d1), fold(d2),
                   pl.BlockSpec((n_slab, tm, LANES), lambda i: (0, i, 0)),
                   row(n_ml), row(n_ml),
                   pl.BlockSpec((n_g, tm), lambda i: (0, i))],
        out_shape=[jax.ShapeDtypeStruct((T, D), F32),
                   jax.ShapeDtypeStruct((T, n_q), F32),
                   jax.ShapeDtypeStruct((T, n_kv), BF16),
                   fold_shape(d1), fold_shape(d2),
                   jax.ShapeDtypeStruct((n_slab, T, LANES), F32),
                   jax.ShapeDtypeStruct((T, n_ml), BF16),
                   jax.ShapeDtypeStruct((T, n_ml), F32),
                   jax.ShapeDtypeStruct((n_g, T), F32)],
        scratch_shapes=[pltpu.VMEM((n_kv // LANES, sub, LANES), F32)] * 2,
        compiler_params=pltpu.CompilerParams(
            dimension_semantics=("parallel",), vmem_limit_bytes=VMEM_LIMIT),
        name="ffn_in",
    )(x2d, *weights)


def _attn_block(q_ref, kv_refs, bias_ref, out_ref, acc_s, m_s, l_s, j, step):
    di = ATT_ORDER[step]
    dil = DILATED[di][1]
    k_ref, v_ref = kv_refs[di]
    lane = lax.broadcasted_iota(jnp.int32, (BLK, LANES), 1)
    head0 = lane < ATT_HD
    r = j % dil
    n = j // dil

    def rows_of(start, stride):
        if stride == 1:
            return pl.ds(pl.multiple_of(start, BLK), BLK)
        return pl.ds(start, BLK, stride=stride)

    cur = rows_of(n * BLK, 1)
    prev = rows_of(jnp.maximum(n - 1, 0) * BLK, 1)
    if dil == 1:
        take = lambda ref, rows: ref[rows, :]
    else:
        take = lambda ref, rows: ref[r, rows, :]
    rows = rows_of(r + n * (BLK * dil), dil)
    q = q_ref[rows, :]
    kk = jnp.concatenate([take(k_ref, prev), take(k_ref, cur)], axis=0)
    vv = jnp.concatenate([take(v_ref, prev), take(v_ref, cur)], axis=0)
    first = jnp.where(n == 0, 1, 0)
    ms, ls, os_ = [], [], []
    for h in range(2):
        sel = head0 if h == 0 else jnp.logical_not(head0)
        qh = jnp.where(sel, q, 0.0).astype(BF16)
        lg = lax.dot_general(qh, kk, (((1,), (1,)), ((), ())), preferred_element_type=F32)
        lg = lg + bias_ref[first, di, h]
        mh = jnp.max(lg, -1, keepdims=True)
        p = jnp.exp2(lg - mh)
        ls.append(jnp.sum(p, -1, keepdims=True))
        ms.append(mh)
        os_.append(jnp.dot(p.astype(BF16), vv, preferred_element_type=F32))
    m_b = jnp.where(head0, ms[0], ms[1])
    l_b = jnp.where(head0, ls[0], ls[1])
    o_b = jnp.where(head0, os_[0], os_[1])
    if step > 0:
        m_o = m_s[rows, :]
        m_n = jnp.maximum(m_o, m_b)
        e_o = jnp.exp2(m_o - m_n)
        e_b = jnp.exp2(m_b - m_n)
        l_b = l_s[rows, :] * e_o + l_b * e_b
        o_b = acc_s[rows, :] * e_o + o_b * e_b
        m_b = m_n
    if step < len(ATT_ORDER) - 1:
        m_s[rows, :] = m_b
        l_s[rows, :] = l_b
        acc_s[rows, :] = o_b
    else:
        out_ref[rows, :] = (o_b / l_b).astype(out_ref.dtype)


def _attn_kernel(q_ref, *refs):
    n_br = len(DILATED)
    kv_refs = [(refs[2 * i], refs[2 * i + 1]) for i in range(n_br)]
    bias_ref, out_ref, acc_s, m_s, l_s = refs[2 * n_br:]
    n_blocks = q_ref.shape[0] // BLK
    for step in range(len(ATT_ORDER)):
        def body(i, _, step=step):
            for u in range(ATT_UNROLL):
                _attn_block(q_ref, kv_refs, bias_ref, out_ref, acc_s, m_s, l_s,
                            i * ATT_UNROLL + u, step)
            return 0
        lax.fori_loop(0, n_blocks // ATT_UNROLL, body, 0)


def _attention(q, kvs, bias, B, S):
    n_pairs = ATT_HEADS * ATT_HD // LANES
    dils = [d for _, d in DILATED]
    assert dils[ATT_ORDER[-1]] == 1
    in_specs = [pl.BlockSpec((None, S, LANES), lambda b, p: (b, 0, p))]
    operands = [q.reshape(B, S, q.shape[-1])]
    for d, kv in zip(dils, kvs):
        for off in (0, n_pairs):
            if d == 1:
                in_specs.append(pl.BlockSpec((None, S, LANES),
                                             lambda b, p, off=off: (b, 0, off + p)))
            else:
                in_specs.append(pl.BlockSpec((None, d, S // d, LANES),
                                             lambda b, p, off=off: (b, 0, 0, off + p)))
            operands.append(kv)
    in_specs.append(pl.BlockSpec((2, len(DILATED), 2, BLK, 2 * BLK),
                                 lambda b, p: (0, 0, p, 0, 0)))
    return pl.pallas_call(
        _attn_kernel,
        grid=(B, n_pairs),
        in_specs=in_specs,
        out_specs=pl.BlockSpec((None, S, LANES), lambda b, p: (b, 0, p)),
        out_shape=jax.ShapeDtypeStruct((B, S, n_pairs * LANES), BF16),
        scratch_shapes=[pltpu.VMEM((S, LANES), F32)] * 3,
        compiler_params=pltpu.CompilerParams(
            dimension_semantics=("parallel", "parallel"), vmem_limit_bytes=VMEM_LIMIT),
        name="attn",
    )(*operands, bias)


def _split3(a):
    hi = a.astype(BF16)
    r1 = a - hi.astype(F32)
    mid = r1.astype(BF16)
    lo = (r1 - mid.astype(F32)).astype(BF16)
    return hi, mid, lo


def _log_sigmoid(x):
    return jnp.minimum(x, 0.0) - jnp.log1p(jnp.exp(-jnp.abs(x)))


def _prefix_max_lanes(x):
    lane = lax.broadcasted_iota(jnp.int32, x.shape, 1)
    sh = 1
    while sh < x.shape[1]:
        x = jnp.maximum(x, jnp.where(lane >= sh, pltpu.roll(x, sh, axis=1), -jnp.inf))
        sh *= 2
    return x


def _gate_factors(gt_ref, gbc_ref, tri_ref, cols_ref, rows_ref):
    H = ML_HEADS
    n_chunks = gt_ref.shape[1] // CHUNK
    n_rows = 2 * H * n_chunks
    tri = tri_ref[...]
    head_row = (lax.broadcasted_iota(jnp.int32, (n_rows, CHUNK), 0) & (2 * H - 1)) < H
    gr = jnp.concatenate([gt_ref[:, i * CHUNK:(i + 1) * CHUNK] + gbc_ref[...]
                          for i in range(n_chunks)], axis=0)
    b_all = sum(lax.dot_general(part, tri, (((1,), (1,)), ((), ())),
                                preferred_element_type=F32)
                for part in _split3(_log_sigmoid(gr)))
    b = jnp.concatenate([pltpu.roll(b_all[i * 2 * H:(i + 1) * 2 * H], H, axis=0)
                         for i in range(n_chunks)], axis=0)
    b = jnp.where(head_row, b, 0.0)
    u = jnp.where(head_row, gr - b, 0.0)
    cm = _prefix_max_lanes(u)
    u_max = jnp.broadcast_to(cm[:, CHUNK - 1:CHUNK], cm.shape)
    g_tot = jnp.broadcast_to(b[:, CHUNK - 1:CHUNK], b.shape)
    m_prev = jnp.zeros((2 * H, CHUNK), F32)
    m_prevs = []
    for i in range(n_chunks):
        rs = slice(i * 2 * H, (i + 1) * 2 * H)
        m_prevs.append(m_prev)
        m_prev = g_tot[rs] + jnp.maximum(m_prev, u_max[rs])
    m_prev = jnp.concatenate(m_prevs, axis=0)
    mm = jnp.maximum(m_prev, u_max)
    sp = jnp.exp(m_prev - mm)
    sl = jnp.exp(u_max - mm)
    wa = jnp.exp(u - u_max)
    m_in = jnp.maximum(m_prev, cm)
    e_w = jnp.exp(m_prev - m_in)
    e_m = jnp.exp(-(b + m_in))
    per = 2 * H * ML_CHUNKS_PER_STEP
    assert 4 * per == LANES
    pad = jnp.zeros((per, CHUNK), F32)
    for s in range(n_rows // per):
        rs = slice(s * per, (s + 1) * per)
        cols_ref[s] = jnp.concatenate([m_in[rs], e_w[rs], e_m[rs], pad], axis=0).T
        rows_ref[s] = jnp.concatenate([u[rs], sp[rs], sl[rs], wa[rs]], axis=0)


def _mlstm_kernel(qk_ref, v_ref, og_ref, gt_ref, gbc_ref, tri_ref, cw_ref, cb_ref, mlg_ref,
                  out_ref, xbuf, ybuf, ct_s, nb_s, cols_s, rows_s):
    step = pl.program_id(0)
    E = LANES
    H = ML_HEADS
    n_slab, n_seq, rows = qk_ref.shape[0], qk_ref.shape[1], qk_ref.shape[2]
    half = rows // 2
    n_chunks = rows // CHUNK
    assert n_slab == 2 * H

    @pl.when(step == 0)
    def _():
        xbuf[:, :, 0:SUBLANES, :] = jnp.zeros((n_slab, n_seq, SUBLANES, LANES), F32)
        ct_s[...] = jnp.zeros(ct_s.shape, F32)
        nb_s[...] = jnp.zeros(nb_s.shape, F32)
        for sq in range(n_seq):
            _gate_factors(gt_ref.at[sq], gbc_ref, tri_ref, cols_s.at[sq], rows_s.at[sq])

    @pl.when(step > 0)
    def _():
        xbuf[:, :, 0:SUBLANES, :] = xbuf[:, :, rows:rows + SUBLANES, :]

    xbuf[:, :, SUBLANES:SUBLANES + rows, :] = qk_ref[...]
    for sq in range(n_seq):
        for c in range(n_slab):
            cl = slice(c * LANES, (c + 1) * LANES)
            for par in range(2):
                acc = cb_ref[:, cl]
                for j in range(CONV_K):
                    off = SUBLANES - (CONV_K - 1) + j + par
                    acc = acc + xbuf[c, sq, pl.ds(off, half, stride=2), :] * cw_ref[j:j + 1, cl]
                act = _silu(acc)
                if c >= H:
                    act = act * (E ** -0.5)
                ybuf[c, sq, pl.ds(par, half, stride=2), :] = act

    n_rows = 2 * H * n_chunks
    cols = [cols_s[sq, step] for sq in range(n_seq)]
    fac = lambda sq, q_: rows_s[sq, step, q_ * n_rows:(q_ + 1) * n_rows, :]
    u = [fac(sq, 0) for sq in range(n_seq)]
    sp_all = [fac(sq, 1) for sq in range(n_seq)]
    sl_all = [fac(sq, 2) for sq in range(n_seq)]
    wa_all = [fac(sq, 3) for sq in range(n_seq)]

    ti = lax.broadcasted_iota(jnp.int32, (CHUNK, CHUNK), 0)
    si = lax.broadcasted_iota(jnp.int32, (CHUNK, CHUNK), 1)
    causal = si <= ti
    ones = jnp.ones((CHUNK, E), BF16)

    cts = [[ct_s[sq, h] for h in range(H)] for sq in range(n_seq)]
    nbs = [[nb_s[sq, h] for h in range(H)] for sq in range(n_seq)]
    for i in range(n_chunks):
        r0 = i * CHUNK
        for sq in range(n_seq):
            for h in range(H):
                row = i * 2 * H + h
                col = lambda q_, row=row, sq=sq: cols[sq][:, q_ * n_rows + row:
                                                          q_ * n_rows + row + 1]
                m_in_c, e_w, e_m = col(0), col(1), col(2)
                q = ybuf[h, sq, r0:r0 + CHUNK, :]
                k = ybuf[H + h, sq, r0:r0 + CHUNK, :]
                v1 = jnp.concatenate([v_ref[sq, r0:r0 + CHUNK, h * E:(h + 1) * E], ones],
                                     axis=1)
                qb = q.astype(BF16)

                d_w = jnp.where(causal, jnp.exp(u[sq][row:row + 1, :] - m_in_c), 0.0)
                s_qk = lax.dot_general(qb, k.astype(BF16), (((1,), (1,)), ((), ())),
                                       preferred_element_type=F32) * d_w
                st = jnp.concatenate([cts[sq][h].astype(BF16), nbs[sq][h].astype(BF16)], axis=1)
                inter = jnp.dot(qb, st, preferred_element_type=F32)
                intra = jnp.dot(s_qk.astype(BF16), v1, preferred_element_type=F32)
                num = e_w * inter[:, :E] + intra[:, :E]
                den = e_w * inter[:, E:] + intra[:, E:]
                hh = num / jnp.maximum(jnp.abs(den), e_m)

                hg = og_ref[sq, r0:r0 + CHUNK, h * E:(h + 1) * E] * hh
                mu = jnp.mean(hg, -1, keepdims=True)
                ex2 = jnp.mean(hg * hg, -1, keepdims=True)
                hc = hg - mu
                var = jnp.maximum(ex2 - mu * mu, 0.0)
                yn = hc * lax.rsqrt(var + LN_EPS) * mlg_ref[:, h * E:(h + 1) * E]
                out_ref[sq, r0:r0 + CHUNK, h * E:(h + 1) * E] = yn.astype(out_ref.dtype)

                kw = (k.T * wa_all[sq][row:row + 1, :]).astype(BF16)
                loc = jnp.dot(kw, v1, preferred_element_type=F32)
                sp_h, sl_h = sp_all[sq][row:row + 1, :], sl_all[sq][row:row + 1, :]
                cts[sq][h] = sp_h * cts[sq][h] + sl_h * loc[:, :E]
                nbs[sq][h] = sp_h * nbs[sq][h] + sl_h * loc[:, E:]
    for sq in range(n_seq):
        for h in range(H):
            ct_s[sq, h] = cts[sq][h]
            nb_s[sq, h] = nbs[sq][h]


def _mlstm(qk, vm, og, gatest, gb_col, conv_w, conv_b, ml_g, B, S):
    rows = ML_CHUNKS_PER_STEP * CHUNK
    ns = S // rows
    W = vm.shape[1]
    n_slab, G = qk.shape[0], gatest.shape[0]
    tri = jnp.tril(jnp.ones((CHUNK, CHUNK), F32)).astype(BF16)
    seq = lambda w: pl.BlockSpec((B, rows, w), lambda c: (0, c, 0))
    consts = (gb_col, tri, conv_w, conv_b, ml_g)
    out = pl.pallas_call(
        _mlstm_kernel,
        grid=(ns,),
        in_specs=[pl.BlockSpec((n_slab, B, rows, LANES), lambda c: (0, 0, c, 0)),
                  seq(W), seq(W),
                  pl.BlockSpec((B, G, S), lambda c: (0, 0, 0))]
                 + [_const_spec(a.shape) for a in consts],
        out_specs=seq(W),
        out_shape=jax.ShapeDtypeStruct((B, S, W), BF16),
        scratch_shapes=[pltpu.VMEM((n_slab, B, rows + SUBLANES, LANES), F32),
                        pltpu.VMEM((n_slab, B, rows, LANES), F32),
                        pltpu.VMEM((B, ML_HEADS, LANES, LANES), F32),
                        pltpu.VMEM((B, ML_HEADS, LANES, LANES), F32),
                        pltpu.VMEM((B, ns, LANES, CHUNK), F32),
                        pltpu.VMEM((B, ns, LANES, CHUNK), F32)],
        compiler_params=pltpu.CompilerParams(
            dimension_semantics=("arbitrary",), vmem_limit_bytes=VMEM_LIMIT),
        name="mlstm",
    )(qk.reshape(n_slab, B, S, LANES), vm.reshape(B, S, W),
      og.reshape(B, S, W), jnp.swapaxes(gatest.reshape(G, B, S), 0, 1), *consts)
    return out.reshape(B * S, W)


def _tail_kernel(x1_ref, att_ref, hm_ref, kt_ref, v_ref, wout_ref, wq_ref, wo_ref,
                 wg_ref, wu_ref, wd_ref, g_ref, b_ref, out_ref):
    n_sub = x1_ref.shape[0] // TAIL_SUB_ROWS
    tiles = [slice(t * TAIL_SUB_ROWS, (t + 1) * TAIL_SUB_ROWS) for t in range(n_sub)]
    n_a = att_ref.shape[1]
    hd = wq_ref.shape[1] // XA_HEADS

    def mix(rs):
        return (jnp.dot(att_ref[rs, :], wout_ref[0:n_a, :], preferred_element_type=F32)
                + jnp.dot(hm_ref[rs, :], wout_ref[n_a:, :], preferred_element_type=F32))

    def cross(x2):
        q = jnp.dot(x2.astype(BF16), wq_ref[...], preferred_element_type=F32)
        heads = []
        for h in range(XA_HEADS):
            qh = q[:, h * hd:(h + 1) * hd].astype(BF16)
            lg = jnp.dot(qh, kt_ref[h * hd:(h + 1) * hd, :], preferred_element_type=F32)
            p = jnp.exp(lg - jnp.max(lg, -1, keepdims=True))
            s = jnp.sum(p, -1, keepdims=True)
            o = jnp.dot(p.astype(BF16), v_ref[:, h * hd:(h + 1) * hd],
                        preferred_element_type=F32)
            heads.append((o / s).astype(BF16))
        return jnp.dot(jnp.concatenate(heads, axis=1), wo_ref[...], preferred_element_type=F32)

    mixes = [mix(rs) for rs in tiles]
    x2s = [_layer_norm(ALPHA * x1_ref[rs, :] + m, g_ref[0:1, :], b_ref[0:1, :])
           for rs, m in zip(tiles, mixes)]
    xas = [cross(x2) for x2 in x2s]
    x3s = [_layer_norm(ALPHA * x2 + xa, g_ref[1:2, :], b_ref[1:2, :])
           for x2, xa in zip(x2s, xas)]
    ffs = [_swiglu(x3.astype(BF16), wg_ref, wu_ref, wd_ref) for x3 in x3s]
    for rs, x3, ff in zip(tiles, x3s, ffs):
        out_ref[rs, :] = _layer_norm(ALPHA * x3 + 0.5 * ff, g_ref[2:3, :], b_ref[2:3, :])


def _tail(x1, att, hm, kt, v, wout, wq, wo, wg, wu, wd, ffn_index, g, b, S):
    T, D = x1.shape
    tm = TAIL_ROW_TILE
    per_batch = S // tm
    row = lambda w: pl.BlockSpec((tm, w), lambda i: (i, 0))
    L = v.shape[1]
    weights = (wout, wq, wo, wg, wu, wd, g, b)
    return pl.pallas_call(
        _tail_kernel,
        grid=(T // tm,),
        in_specs=[row(D), row(att.shape[1]), row(hm.shape[1]),
                  pl.BlockSpec((None, D, L), lambda i: (i // per_batch, 0, 0)),
                  pl.BlockSpec((None, L, D), lambda i: (i // per_batch, 0, 0))]
                 + _weight_specs(weights, ffn_index),
        out_specs=row(D),
        out_shape=jax.ShapeDtypeStruct((T, D), F32),
        compiler_params=pltpu.CompilerParams(
            dimension_semantics=("parallel",), vmem_limit_bytes=VMEM_LIMIT),
        name="tail",
    )(x1, att, hm, kt, v, *weights)


def kernel(x, mem, rel_bias, ln_g, ln_b, ffn_w_gate, ffn_w_up, ffn_w_down, w_in, conv_w, conv_b,
           ig_bias, fg_bias, ml_norm_g, w_out, xq_w, xkv_w, xo_w):
    B, S, D = x.shape
    att_w = ATT_HEADS * ATT_HD
    ml_w = ML_HEADS * LANES
    bias = _bias_tables(rel_bias)
    xf = x.reshape(B * S, D)
    bf = lambda a: a.astype(BF16)
    wg_all, wu_all, wd_all = bf(ffn_w_gate), bf(ffn_w_up), bf(ffn_w_down)
    for l in range(DEPTH):
        wint = bf(w_in[l].T)
        gb = jnp.concatenate([ig_bias[l], fg_bias[l]]).astype(F32)

        x1, qa, kv1, kvf1, kvf2, qk, vm, og, gatest = _ffn_in(
            xf, wg_all, wu_all, wd_all, (l, 0),
            ln_g[l, 0][None], ln_b[l, 0][None], wint, 3 * att_w, 2 * ml_w, ml_w, S)
        att = _attention(qa, [kv1.reshape(B, S, -1), kvf1, kvf2], bias, B, S)
        att = att.reshape(B * S, att_w)
        hm = _mlstm(qk, vm, og, gatest, gb[:, None], conv_w[l], conv_b[l][None],
                    ml_norm_g[l][None], B, S)

        hd = D // XA_HEADS
        kt, v = _memkv(mem, bf(xkv_w[l]))
        xf = _tail(x1, att, hm, kt, v, bf(w_out[l]), bf(xq_w[l] * hd ** -0.5), bf(xo_w[l]),
                   wg_all, wu_all, wd_all, (l, 1), ln_g[l, 1:4], ln_b[l, 1:4], S)
    return xf.reshape(B, S, D)
```
